```python
import jax, jax.numpy as jnp
from jax import lax
import numpy as np

D_MODEL = 1024
BATCH = 8
SEQ = 4096
DEPTH = 1
DEC_BATCH = 128
DEC_SEQ = 1
PAST_LEN = 8192
PAGE_SIZE = 128

GROUP_WINDOWS = (128, 512, 2048)
GROUP_DILATIONS = (1, 4, 16)
N_GROUPS = 3
HEADS_PER_GROUP = 4
HEAD_DIM = 64
A_WIDTH = HEADS_PER_GROUP * HEAD_DIM
A_COLS = N_GROUPS * 3 * A_WIDTH
RWKV_HEADS = 8
RWKV_HEAD_DIM = 64
B_WIDTH = RWKV_HEADS * RWKV_HEAD_DIM
DECAY_LORA = 64
AAA_LORA = 64
GATE_LORA = 128
SHIFT_WIDTH = 3 * B_WIDTH + DECAY_LORA + AAA_LORA + GATE_LORA
D_IN = A_COLS + SHIFT_WIDTH + 2 * D_MODEL
N_EXPERTS = 32
TOP_K = 4
D_FF = D_MODEL
SWIGLU_LIMIT = 7.0
SWIGLU_ALPHA = 1.702
MOE_BLOCK = 256
RMS_EPS = 1e-6
GN_EPS = 64e-5

kernel_name = "hybrid_dilated_attn_rwkv7_moe_step"


def _rmsnorm(x, g):
    xf = x.astype(jnp.float32)
    y = xf * lax.rsqrt(jnp.mean(xf * xf, axis=-1, keepdims=True) + RMS_EPS)
    return (y * g).astype(x.dtype)


def _headnorm(x, g):
    return x * lax.rsqrt(jnp.mean(x * x, axis=-1, keepdims=True) + RMS_EPS) * g


def _in_proj(x, norm1_g, w_in):
    proj = _rmsnorm(x, norm1_g) @ w_in
    return (proj[..., :A_COLS], proj[..., A_COLS:A_COLS + SHIFT_WIDTH],
            proj[..., A_COLS + SHIFT_WIDTH:])


def _group_qkv(pa, q_norm_g, k_norm_g):
    t = pa.astype(jnp.float32).reshape(*pa.shape[:-1], N_GROUPS, 3, HEADS_PER_GROUP, HEAD_DIM)
    q = _headnorm(t[..., 0, :, :], q_norm_g[:, None, :]) * (HEAD_DIM ** -0.5)
    k = _headnorm(t[..., 1, :, :], k_norm_g[:, None, :])
    v = t[..., 2, :, :]
    return q, k, v


def _dilated_prompt(q, k, v, dil, n_back):
    Bn, S, H, Dh = q.shape
    L = S // dil
    nb = -(-L // n_back)
    Lp = nb * n_back

    def blocks(t):
        t = t.reshape(Bn, L, dil, H, Dh)
        t = jnp.pad(t, ((0, 0), (0, Lp - L), (0, 0), (0, 0), (0, 0)))
        return t.reshape(Bn, nb, n_back, dil, H, Dh)

    def band(t):
        prev = jnp.pad(t, ((0, 0), (1, 0), (0, 0), (0, 0), (0, 0), (0, 0)))[:, :nb]
        return jnp.concatenate([prev, t], axis=2)

    qb = blocks(q)
    kw = band(blocks(k))
    vw = band(blocks(v))
    s = jnp.einsum('bnqrhd,bnkrhd->bnrhqk', qb, kw)
    qi = jnp.arange(n_back)[:, None]
    ki = jnp.arange(2 * n_back)[None, :]
    dist = n_back + qi - ki
    in_band = (dist >= 0) & (dist <= n_back)
    has_prev = (jnp.arange(nb) > 0)[:, None, None] | (ki >= n_back)[None]
    mask = (in_band[None] & has_prev)[None, :, None, None]
    s = jnp.where(mask, s, -jnp.inf)
    m = jnp.max(s, axis=-1)
    p = jnp.exp(s - m[..., None])
    den = jnp.sum(p, axis=-1)
    num = jnp.einsum('bnrhqk,bnkrhd->bnqrhd', p, vw)

    def unblock(t):
        return t.reshape(Bn, Lp * dil, *t.shape[4:])[:, :S]

    return unblock(num), unblock(jnp.moveaxis(den, 4, 2)), unblock(jnp.moveaxis(m, 4, 2))


def _dilated_sample(q, k_all, v_all, dil, n_back, n_buf):
    T = q.shape[1]
    idx = n_buf + jnp.arange(T)[:, None] - dil * jnp.arange(n_back + 1)[None, :]
    valid = idx >= 0
    idx = jnp.maximum(idx, 0)
    kg = k_all[:, idx]
    vg = v_all[:, idx]
    s = jnp.einsum('bthd,btmhd->bthm', q, kg)
    s = jnp.where(valid[None, :, None, :], s, -jnp.inf)
    m = jnp.max(s, axis=-1)
    p = jnp.exp(s - m[..., None])
    den = jnp.sum(p, axis=-1)
    num = jnp.einsum('bthm,btmhd->bthd', p, vg)
    return num, den, m


def _merge_groups(stats):
    M = jnp.max(jnp.stack([st[2] for st in stats]), axis=0)
    num = sum(n * jnp.exp(mx - M)[..., None] for n, _, mx in stats)
    den = sum(d * jnp.exp(mx - M) for _, d, mx in stats)
    return num / den[..., None]


def _rwkv(ps, prev, wkv0, mu_shift, w0, w2, a0, a2, g2, k_k, k_a, r_k, ln_x_w, ln_x_b):
    Bn, T, _ = ps.shape
    pf = ps.astype(jnp.float32)
    shifted = jnp.concatenate([prev.astype(jnp.float32), pf[:, :-1]], axis=1)
    z = pf + mu_shift * (shifted - pf)
    c1, c2, c3 = B_WIDTH, 2 * B_WIDTH, 3 * B_WIDTH
    r, k, v = z[..., :c1], z[..., c1:c2], z[..., c2:c3]
    wl = z[..., c3:c3 + DECAY_LORA]
    al = z[..., c3 + DECAY_LORA:c3 + DECAY_LORA + AAA_LORA]
    gl = z[..., c3 + DECAY_LORA + AAA_LORA:]
    w_log = -jax.nn.softplus(-(w0 + jnp.tanh(wl) @ w2)) - 0.5
    decay = jnp.exp(-jnp.exp(w_log))
    a = jax.nn.sigmoid(a0 + al @ a2)
    g = jax.nn.sigmoid(gl) @ g2
    hs = (Bn, T, RWKV_HEADS, RWKV_HEAD_DIM)
    kk = (k * k_k).reshape(hs)
    kk = kk / jnp.maximum(jnp.linalg.norm(kk, axis=-1, keepdims=True), 1e-12)
    k = k * (1.0 + (a - 1.0) * k_a)
    rh, kh, vh = r.reshape(hs), k.reshape(hs), v.reshape(hs)
    ah = a.reshape(hs)

    def step(S, inp):
        r_t, w_t, k_t, v_t, a_t, b_t = inp
        sa = jnp.einsum('bhvk,bhk->bhv', S, a_t)
        S = (S * w_t[:, :, None, :] + sa[..., None] * b_t[:, :, None, :]
             + v_t[..., None] * k_t[:, :, None, :])
        return S, jnp.einsum('bhvk,bhk->bhv', S, r_t)

    xs = tuple(jnp.moveaxis(t, 1, 0) for t in (rh, decay.reshape(hs), kh, vh, -kk, kk * ah))
    S_fin, ys = lax.scan(step, wkv0.astype(jnp.float32), xs)
    y = jnp.moveaxis(ys, 0, 1)
    mu = jnp.mean(y, axis=-1, keepdims=True)
    var = jnp.mean(jnp.square(y - mu), axis=-1, keepdims=True)
    yn = ((y - mu) * lax.rsqrt(var + GN_EPS)).reshape(Bn, T, B_WIDTH) * ln_x_w + ln_x_b
    bonus = (jnp.sum(rh * kh * r_k, axis=-1, keepdims=True) * vh).reshape(Bn, T, B_WIDTH)
    return (yn + bonus) * g, S_fin


def _expert_block(xb, e, w_gu, b_gu, w_down, b_down):
    gu = xb @ w_gu[e] + b_gu[e]
    gate = jnp.minimum(gu[:, :D_FF], SWIGLU_LIMIT)
    up = jnp.clip(gu[:, D_FF:], -SWIGLU_LIMIT, SWIGLU_LIMIT)
    hid = (up + 1.0) * gate * jax.nn.sigmoid(gate * SWIGLU_ALPHA)
    return hid @ w_down[e] + b_down[e]


def _moe(h, router_w, router_b, w_gu, b_gu, w_down, b_down):
    N, D = h.shape
    logits = (h @ router_w + router_b).astype(jnp.float32)
    top_v, top_e = lax.top_k(logits, TOP_K)
    gates = jax.nn.softmax(top_v, axis=-1)
    NK = N * TOP_K
    flat_e = top_e.reshape(NK)
    flat_g = gates.reshape(NK)
    flat_t = jnp.arange(NK, dtype=jnp.int32) // TOP_K
    order = jnp.argsort(flat_e)
    se = flat_e[order]
    counts = jnp.bincount(flat_e, length=N_EXPERTS)
    starts = jnp.cumsum(counts) - counts
    pcounts = (counts + MOE_BLOCK - 1) // MOE_BLOCK * MOE_BLOCK
    pends = jnp.cumsum(pcounts)
    pstarts = pends - pcounts
    dest = pstarts[se] + jnp.arange(NK) - starts[se]
    n_blocks = -(-NK // MOE_BLOCK) + N_EXPERTS
    P = n_blocks * MOE_BLOCK
    row_tok = jnp.full((P,), N, jnp.int32).at[dest].set(flat_t[order])
    row_gate = jnp.zeros((P,), jnp.float32).at[dest].set(flat_g[order])
    blk_e = jnp.minimum(jnp.searchsorted(pends, jnp.arange(n_blocks) * MOE_BLOCK, side='right'),
                        N_EXPERTS - 1)
    xs = jnp.take(h, row_tok, axis=0, mode='fill', fill_value=0).reshape(n_blocks, MOE_BLOCK, D)
    ys = lax.map(lambda a: _expert_block(a[0], a[1], w_gu, b_gu, w_down, b_down), (xs, blk_e))
    ys = ys.reshape(P, D).astype(jnp.float32) * row_gate[:, None]
    return jnp.zeros((N, D), jnp.float32).at[row_tok].add(ys, mode='drop')


def _finish(x, o_a, o_b, pg, w_pa, w_pb, w_o, norm2_g, router_w, router_b, w_gu, b_gu, w_down, b_down):
    dt = x.dtype
    g_a = jax.nn.sigmoid(pg[..., :D_MODEL].astype(jnp.float32))
    g_b = jax.nn.sigmoid(pg[..., D_MODEL:].astype(jnp.float32))
    br_a = o_a.reshape(*o_a.shape[:-2], A_WIDTH).astype(dt) @ w_pa
    br_b = o_b.astype(dt) @ w_pb
    mix = (g_a * br_a + g_b * br_b).astype(dt)
    h = x + mix @ w_o
    hn = _rmsnorm(h, norm2_g)
    ff = _moe(hn.reshape(-1, D_MODEL), router_w, router_b, w_gu, b_gu, w_down, b_down)
    return h + ff.reshape(h.shape).astype(dt)


def setup_inputs(seed: int = 0) -> dict:
    key = jax.random.key(seed)
    ks = jax.random.split(key, 40)
    f32 = jnp.float32

    def nrm(i, shape, scale):
        return jax.random.normal(ks[i], shape, f32) * scale

    lens = [min(w, PAST_LEN) for w in GROUP_WINDOWS]
    kvs = (HEADS_PER_GROUP, HEAD_DIM)
    return {
        "x_prompt": nrm(0, (BATCH, SEQ, D_MODEL), 1.0),
        "x_sample": nrm(1, (DEC_BATCH, DEC_SEQ, D_MODEL), 1.0),
        "cache_kv_w128": nrm(2, (DEC_BATCH, lens[0], 2) + kvs, 1.0),
        "cache_kv_w512": nrm(3, (DEC_BATCH, lens[1], 2) + kvs, 1.0),
        "cache_kv_w2048": nrm(4, (DEC_BATCH, lens[2], 2) + kvs, 1.0),
        "state_wkv": nrm(5, (DEC_BATCH, RWKV_HEADS, RWKV_HEAD_DIM, RWKV_HEAD_DIM), 0.1),
        "state_shift": nrm(6, (DEC_BATCH, 1, SHIFT_WIDTH), 1.0),
        "norm1_g": 1.0 + nrm(7, (D_MODEL,), 0.02),
        "w_in": nrm(8, (D_MODEL, D_IN), D_MODEL ** -0.5),
        "q_norm_g": 1.0 + nrm(9, (N_GROUPS, HEAD_DIM), 0.02),
        "k_norm_g": 1.0 + nrm(10, (N_GROUPS, HEAD_DIM), 0.02),
        "mu_shift": jax.random.uniform(ks[11], (SHIFT_WIDTH,), f32),
        "w0": -1.0 + nrm(12, (B_WIDTH,), 0.5),
        "w2": nrm(13, (DECAY_LORA, B_WIDTH), 0.5 * DECAY_LORA ** -0.5),
        "a0": nrm(14, (B_WIDTH,), 0.1),
        "a2": nrm(15, (AAA_LORA, B_WIDTH), 0.5 * AAA_LORA ** -0.5),
        "g2": nrm(16, (GATE_LORA, B_WIDTH), GATE_LORA ** -0.5),
        "k_k": 0.85 + nrm(17, (B_WIDTH,), 0.02),
        "k_a": 1.0 + nrm(18, (B_WIDTH,), 0.02),
        "r_k": nrm(19, (RWKV_HEADS, RWKV_HEAD_DIM), 0.1),
        "ln_x_w": 1.0 + nrm(20, (B_WIDTH,), 0.02),
        "ln_x_b": nrm(21, (B_WIDTH,), 0.01),
        "w_pa": nrm(22, (A_WIDTH, D_MODEL), A_WIDTH ** -0.5),
        "w_pb": nrm(23, (B_WIDTH, D_MODEL), B_WIDTH ** -0.5),
        "w_o": nrm(24, (D_MODEL, D_MODEL), D_MODEL ** -0.5),
        "norm2_g": 1.0 + nrm(25, (D_MODEL,), 0.02),
        "router_w": nrm(26, (D_MODEL, N_EXPERTS), D_MODEL ** -0.5),
        "router_b": nrm(27, (N_EXPERTS,), 0.01),
        "w_gu": nrm(28, (N_EXPERTS, D_MODEL, 2 * D_FF), D_MODEL ** -0.5),
        "b_gu": nrm(29, (N_EXPERTS, 2 * D_FF), 0.01),
        "w_down": nrm(30, (N_EXPERTS, D_FF, D_MODEL), D_FF ** -0.5),
        "b_down": nrm(31, (N_EXPERTS, D_MODEL), 0.01),
    }


def reference(x_prompt, x_sample, cache_kv_w128, cache_kv_w512, cache_kv_w2048, state_wkv, state_shift,
              norm1_g, w_in, q_norm_g, k_norm_g, mu_shift, w0, w2, a0, a2, g2, k_k, k_a, r_k,
              ln_x_w, ln_x_b, w_pa, w_pb, w_o, norm2_g, router_w, router_b, w_gu, b_gu, w_down, b_down):
    rwkv_p = (mu_shift, w0, w2, a0, a2, g2, k_k, k_a, r_k, ln_x_w, ln_x_b)
    out_p = (w_pa, w_pb, w_o, norm2_g, router_w, router_b, w_gu, b_gu, w_down, b_down)
    f32 = jnp.float32

    Bp, S, _ = x_prompt.shape
    pa, ps, pg = _in_proj(x_prompt, norm1_g, w_in)
    q, k, v = _group_qkv(pa, q_norm_g, k_norm_g)
    stats, kv_p = [], []
    for g in range(N_GROUPS):
        dil = GROUP_DILATIONS[g]
        n_back = GROUP_WINDOWS[g] // dil
        stats.append(_dilated_prompt(q[:, :, g], k[:, :, g], v[:, :, g], dil, n_back))
        keep = min(GROUP_WINDOWS[g], S)
        kv_p.append(jnp.stack([k[:, S - keep:, g], v[:, S - keep:, g]], axis=2).astype(x_prompt.dtype))
    o_a = _merge_groups(stats)
    o_b, wkv_p = _rwkv(ps, jnp.zeros((Bp, 1, SHIFT_WIDTH), ps.dtype),
                       jnp.zeros((Bp, RWKV_HEADS, RWKV_HEAD_DIM, RWKV_HEAD_DIM), f32), *rwkv_p)
    y_prompt = _finish(x_prompt, o_a, o_b, pg, *out_p)
    wkv_p = wkv_p.astype(state_wkv.dtype)
    shift_p = ps[:, -1:].astype(state_shift.dtype)

    T = x_sample.shape[1]
    caches = (cache_kv_w128, cache_kv_w512, cache_kv_w2048)
    pa, ps, pg = _in_proj(x_sample, norm1_g, w_in)
    q, k, v = _group_qkv(pa, q_norm_g, k_norm_g)
    stats, kv_s = [], []
    for g in range(N_GROUPS):
        dil = GROUP_DILATIONS[g]
        n_back = GROUP_WINDOWS[g] // dil
        cache = caches[g]
        n_buf = cache.shape[1]
        new = jnp.stack([k[:, :, g], v[:, :, g]], axis=2).astype(cache.dtype)
        kv_all = jnp.concatenate([cache, new], axis=1)
        kv_f = kv_all.astype(f32)
        stats.append(_dilated_sample(q[:, :, g], kv_f[:, :, 0], kv_f[:, :, 1], dil, n_back, n_buf))
        keep = min(GROUP_WINDOWS[g], n_buf + T)
        kv_s.append(kv_all[:, n_buf + T - keep:])
    o_a = _merge_groups(stats)
    o_b, wkv_s = _rwkv(ps, state_shift, state_wkv, *rwkv_p)
    y_sample = _finish(x_sample, o_a, o_b, pg, *out_p)
    wkv_s = wkv_s.astype(state_wkv.dtype)
    shift_s = ps[:, -1:].astype(state_shift.dtype)

    return (y_prompt, y_sample, kv_p[0], kv_p[1], kv_p[2], wkv_p, shift_p,
            kv_s[0], kv_s[1], kv_s[2], wkv_s, shift_s)
```

```python
import functools

import jax
import jax.numpy as jnp
from jax import lax
from jax.experimental import pallas as pl
from jax.experimental.pallas import tpu as pltpu

F32 = jnp.float32
BF16 = jnp.bfloat16

D_MODEL = 1024
GROUP_WINDOWS = (128, 512, 2048)
GROUP_DILATIONS = (1, 4, 16)
N_GROUPS = 3
HEADS_PER_GROUP = 4
HEAD_DIM = 64
A_WIDTH = HEADS_PER_GROUP * HEAD_DIM
A_COLS = N_GROUPS * 3 * A_WIDTH
RWKV_HEADS = 8
RWKV_HEAD_DIM = 64
B_WIDTH = RWKV_HEADS * RWKV_HEAD_DIM
DECAY_LORA = 64
AAA_LORA = 64
GATE_LORA = 128
SHIFT_WIDTH = 3 * B_WIDTH + DECAY_LORA + AAA_LORA + GATE_LORA
D_IN = A_COLS + SHIFT_WIDTH + 2 * D_MODEL
N_EXPERTS = 32
TOP_K = 4
D_FF = D_MODEL
SWIGLU_LIMIT = 7.0
SWIGLU_ALPHA = 1.702
MOE_BLOCK = 256
RMS_EPS = 1e-6
GN_EPS = 64e-5

N_BACK = 128
COL_CHUNK = 256
SCAN_CHUNK = 64
LOGIT_PAD = 128
VMEM_LIMIT = 48 * 1024 * 1024


def _cparams(*sem):
    return pltpu.CompilerParams(dimension_semantics=sem, vmem_limit_bytes=VMEM_LIMIT)


def _sigmoid(x):
    return 1.0 / (1.0 + jnp.exp(-x))


def _split(x):
    hi = x.astype(BF16)
    lo = (x - hi.astype(F32)).astype(BF16)
    return hi, lo


def _dg(a, b, dims):
    return lax.dot_general(a, b, (dims, ((), ())), preferred_element_type=F32)


NN = ((1,), (0,))
NT = ((1,), (1,))
TN = ((0,), (0,))


def _dot1(a, b, dims=NN):
    return _dg(a.astype(BF16), b.astype(BF16), dims)


def _dot3(a, b, dims=NN):
    ah, al = _split(a)
    bh, bl = _split(b)
    return _dg(ah, bh, dims) + _dg(al, bh, dims) + _dg(ah, bl, dims)


def _dot_exact_rhs(a, b_bf16, dims=NN):
    hi = a.astype(BF16)
    r1 = a - hi.astype(F32)
    mid = r1.astype(BF16)
    lo = (r1 - mid.astype(F32)).astype(BF16)
    return _dg(hi, b_bf16, dims) + _dg(mid, b_bf16, dims) + _dg(lo, b_bf16, dims)


def _seg_ones(width, seg, scale):
    i = jnp.arange(width)[:, None] // seg
    j = jnp.arange(width)[None, :] // seg
    return jnp.where(i == j, scale, 0.0).astype(BF16)


def _in_proj_kernel(x_ref, g1_ref, w_ref, mult_ref, bd_ref, pa_ref, ps_ref, pg_ref):
    x = x_ref[...]
    xn = x * lax.rsqrt(jnp.mean(x * x, axis=-1, keepdims=True) + RMS_EPS) * g1_ref[...]
    xb = xn.astype(BF16)
    bd = bd_ref[...]
    for c in range(A_COLS // COL_CHUNK):
        acc = _dg(xb, w_ref[:, c * COL_CHUNK:(c + 1) * COL_CHUNK], NN)
        if c % 3 != 2:
            ms = _dot_exact_rhs(acc * acc, bd)
            acc = acc * lax.rsqrt(ms + RMS_EPS) * mult_ref[c:c + 1, :]
        pa_ref[:, c * COL_CHUNK:(c + 1) * COL_CHUNK] = acc
    for c in range(SHIFT_WIDTH // COL_CHUNK):
        lo = A_COLS + c * COL_CHUNK
        ps_ref[:, c * COL_CHUNK:(c + 1) * COL_CHUNK] = _dg(xb, w_ref[:, lo:lo + COL_CHUNK], NN)
    for c in range(2 * D_MODEL // COL_CHUNK):
        lo = A_COLS + SHIFT_WIDTH + c * COL_CHUNK
        pg_ref[:, c * COL_CHUNK:(c + 1) * COL_CHUNK] = _sigmoid(_dg(xb, w_ref[:, lo:lo + COL_CHUNK], NN))


def _in_proj(x2, norm1_g, w_in_b, mult, bd, tm):
    n = x2.shape[0]
    const = lambda i: (0, 0)
    return pl.pallas_call(
        _in_proj_kernel,
        grid=(n // tm,),
        in_specs=[
            pl.BlockSpec((tm, D_MODEL), lambda i: (i, 0)),
            pl.BlockSpec((1, D_MODEL), const),
            pl.BlockSpec((D_MODEL, D_IN), const, pipeline_mode=pl.Buffered(1)),
            pl.BlockSpec(mult.shape, const),
            pl.BlockSpec(bd.shape, const),
        ],
        out_specs=[
            pl.BlockSpec((tm, A_COLS), lambda i: (i, 0)),
            pl.BlockSpec((tm, SHIFT_WIDTH), lambda i: (i, 0)),
            pl.BlockSpec((tm, 2 * D_MODEL), lambda i: (i, 0)),
        ],
        out_shape=[
            jax.ShapeDtypeStruct((n, A_COLS), F32),
            jax.ShapeDtypeStruct((n, SHIFT_WIDTH), F32),
            jax.ShapeDtypeStruct((n, 2 * D_MODEL), F32),
        ],
        compiler_params=_cparams("parallel"),
        name="in_proj",
    )(x2, norm1_g.reshape(1, D_MODEL), w_in_b, mult, bd)


def _attn_kernel(q_ref, kp_ref, kc_ref, vp_ref, vc_ref, num_ref, den_ref, m_ref):
    n = pl.program_id(2)
    qi = lax.broadcasted_iota(jnp.int32, (N_BACK, N_BACK), 0)
    kj = lax.broadcasted_iota(jnp.int32, (N_BACK, N_BACK), 1)
    mask_prev = kj >= qi + jnp.where(n > 0, 0, N_BACK)
    mask_cur = kj <= qi
    for h in range(HEADS_PER_GROUP):
        sl = slice(h * HEAD_DIM, (h + 1) * HEAD_DIM)
        q = q_ref[:, sl]
        s_p = jnp.where(mask_prev, _dot1(q, kp_ref[:, sl], NT), -jnp.inf)
        s_c = jnp.where(mask_cur, _dot1(q, kc_ref[:, sl], NT), -jnp.inf)
        m = jnp.maximum(jnp.max(s_p, axis=-1, keepdims=True), jnp.max(s_c, axis=-1, keepdims=True))
        p_p = jnp.exp(s_p - m)
        p_c = jnp.exp(s_c - m)
        den = jnp.sum(p_p, axis=-1, keepdims=True) + jnp.sum(p_c, axis=-1, keepdims=True)
        num = _dot1(p_p, vp_ref[:, sl]) + _dot1(p_c, vc_ref[:, sl])
        num_ref[:, sl] = num
        den_ref[:, sl] = jnp.broadcast_to(den, (N_BACK, HEAD_DIM))
        m_ref[:, sl] = jnp.broadcast_to(m, (N_BACK, HEAD_DIM))


def _attn_prompt(pa3, g):
    bsz, seq, _ = pa3.shape
    dil = GROUP_DILATIONS[g]
    length = seq // dil
    nb = length // N_BACK
    slabs = A_COLS // COL_CHUNK
    pav = pa3.reshape(bsz, length, dil * A_COLS)

    def spec(slab, prev):
        if prev:
            return pl.BlockSpec((None, N_BACK, COL_CHUNK),
                                lambda b, r, n: (b, jnp.maximum(n - 1, 0), r * slabs + 3 * g + slab))
        return pl.BlockSpec((None, N_BACK, COL_CHUNK), lambda b, r, n: (b, n, r * slabs + 3 * g + slab))

    out_spec = pl.BlockSpec((None, N_BACK, A_WIDTH), lambda b, r, n: (b, n, r))
    out_sds = jax.ShapeDtypeStruct((bsz, length, dil * A_WIDTH), F32)
    outs = pl.pallas_call(
        _attn_kernel,
        grid=(bsz, dil, nb),
        in_specs=[spec(0, False), spec(1, True), spec(1, False), spec(2, True), spec(2, False)],
        out_specs=[out_spec] * 3,
        out_shape=[out_sds] * 3,
        compiler_params=_cparams("parallel", "parallel", "arbitrary"),
        name=f"attn_prompt_g{g}",
    )(pav, pav, pav, pav, pav)
    return [o.reshape(bsz * seq, A_WIDTH) for o in outs]


def _attn_sample_kernel(pa_ref, kv0_ref, kv1_ref, kv2_ref, *out_refs):
    kv_refs = (kv0_ref, kv1_ref, kv2_ref)
    bt = pa_ref.shape[0]
    for g in range(N_GROUPS):
        base = g * 3 * A_WIDTH
        num_ref, den_ref, m_ref = out_refs[3 * g:3 * g + 3]
        for h in range(HEADS_PER_GROUP):
            o = h * HEAD_DIM
            sl = slice(o, o + HEAD_DIM)
            q = pa_ref[:, :, base + o:base + o + HEAD_DIM]
            kn = pa_ref[:, :, base + A_WIDTH + o:base + A_WIDTH + o + HEAD_DIM]
            vn = pa_ref[:, :, base + 2 * A_WIDTH + o:base + 2 * A_WIDTH + o + HEAD_DIM]
            kc = kv_refs[g][:, :, o:o + HEAD_DIM]
            vc = kv_refs[g][:, :, A_WIDTH + o:A_WIDTH + o + HEAD_DIM]
            s_c = jnp.sum(kc * q, axis=-1, keepdims=True)
            s_n = jnp.sum(kn * q, axis=-1, keepdims=True)
            m = jnp.maximum(jnp.max(s_c, axis=1, keepdims=True), s_n)
            p_c = jnp.exp(s_c - m)
            p_n = jnp.exp(s_n - m)
            den = jnp.sum(p_c, axis=1, keepdims=True) + p_n
            num = jnp.sum(p_c * vc, axis=1, keepdims=True) + p_n * vn
            num_ref[:, :, sl] = num
            den_ref[:, :, sl] = jnp.broadcast_to(den, (bt, 1, HEAD_DIM))
            m_ref[:, :, sl] = jnp.broadcast_to(m, (bt, 1, HEAD_DIM))


def _attn_sample(pa_s, kvs, bt=8):
    bd = pa_s.shape[0]
    out_spec = pl.BlockSpec((bt, 1, A_WIDTH), lambda i: (i, 0, 0))
    out_sds = jax.ShapeDtypeStruct((bd, 1, A_WIDTH), F32)
    outs = pl.pallas_call(
        _attn_sample_kernel,
        grid=(bd // bt,),
        in_specs=[pl.BlockSpec((bt, 1, A_COLS), lambda i: (i, 0, 0))]
                 + [pl.BlockSpec((bt, N_BACK, 2 * A_WIDTH), lambda i: (i, 0, 0))] * N_GROUPS,
        out_specs=[out_spec] * 9,
        out_shape=[out_sds] * 9,
        compiler_params=_cparams("parallel"),
        name="attn_sample",
    )(pa_s.reshape(bd, 1, A_COLS), *kvs)
    return [o.reshape(bd, A_WIDTH) for o in outs]


def _rwkv_prep_kernel(ps_ref, pv_ref, p0_ref, mu_ref, w0_ref, w2_ref, a0_ref, a2_ref, g2_ref,
                      kk_ref, ka_ref, bd_ref,
                      r_ref, lw_ref, k_ref, v_ref, a_ref, b_ref, g_ref):
    i = pl.program_id(1)
    pf = ps_ref[...]
    prev_row = jnp.where(i == 0, p0_ref[...], pv_ref[7:8, :])
    row = lax.broadcasted_iota(jnp.int32, pf.shape, 0)
    shifted = jnp.where(row == 0, prev_row, pltpu.roll(pf, 1, axis=0))
    z = pf + mu_ref[...] * (shifted - pf)
    c1, c2, c3 = B_WIDTH, 2 * B_WIDTH, 3 * B_WIDTH
    r, k, v = z[:, :c1], z[:, c1:c2], z[:, c2:c3]
    wl = z[:, c3:c3 + DECAY_LORA]
    al = z[:, c3 + DECAY_LORA:c3 + DECAY_LORA + AAA_LORA]
    gl = z[:, c3 + DECAY_LORA + AAA_LORA:]
    xw = w0_ref[...] + _dot3(jnp.tanh(wl), w2_ref[...])
    w_log = -(jnp.maximum(-xw, 0.0) + jnp.log(1.0 + jnp.exp(-jnp.abs(xw)))) - 0.5
    lw_ref[...] = -jnp.exp(w_log)
    a = _sigmoid(a0_ref[...] + _dot3(al, a2_ref[...]))
    g_ref[...] = _dot3(_sigmoid(gl), g2_ref[...])
    kk = k * kk_ref[...]
    bd = bd_ref[...]
    for c in range(B_WIDTH // 128):
        sl = slice(c * 128, (c + 1) * 128)
        kc = kk[:, sl]
        nrm = jnp.sqrt(_dot_exact_rhs(kc * kc, bd))
        kc = kc / jnp.maximum(nrm, 1e-12)
        a_ref[:, sl] = -kc
        b_ref[:, sl] = kc * a[:, sl]
    r_ref[...] = r
    v_ref[...] = v
    k_ref[...] = k * (1.0 + (a - 1.0) * ka_ref[...])


def _rwkv_prep(ps3, prev0, p, tm):
    bsz, t, _ = ps3.shape
    const = lambda b, i: (0, 0)
    row = lambda a: a.reshape(1, -1)
    params = [row(p["mu_shift"]), row(p["w0"]), p["w2"], row(p["a0"]), p["a2"], p["g2"],
              row(p["k_k"]), row(p["k_a"]), _seg_ones(128, RWKV_HEAD_DIM, 1.0)]
    out_spec = pl.BlockSpec((None, tm, B_WIDTH), lambda b, i: (b, i, 0))
    out_sds = jax.ShapeDtypeStruct((bsz, t, B_WIDTH), F32)
    return pl.pallas_call(
        _rwkv_prep_kernel,
        grid=(bsz, t // tm),
        in_specs=[
            pl.BlockSpec((None, tm, SHIFT_WIDTH), lambda b, i: (b, i, 0)),
            pl.BlockSpec((None, 8, SHIFT_WIDTH), lambda b, i: (b, jnp.maximum(i * (tm // 8) - 1, 0), 0)),
            pl.BlockSpec((None, 1, SHIFT_WIDTH), lambda b, i: (b, 0, 0)),
        ] + [pl.BlockSpec(a.shape, const) for a in params],
        out_specs=[out_spec] * 7,
        out_shape=[out_sds] * 7,
        compiler_params=_cparams("parallel", "arbitrary"),
        name="rwkv_prep",
    )(ps3, ps3, prev0, *params)


def _rwkv_scan_kernel(r_ref, lw_ref, k_ref, v_ref, a_ref, b_ref, g_ref, s0_ref, rk_ref, lnw_ref, lnb_ref,
                      tril_ref, o_ref, s_ref, *, chunk, t_valid):
    c = pl.program_id(1)

    @pl.when(c == 0)
    def _():
        s_ref[...] = s0_ref[...]

    lw = lw_ref[...]
    a, b, k, v = a_ref[...], b_ref[...], k_ref[...], v_ref[...]
    if t_valid < chunk:
        live = lax.broadcasted_iota(jnp.int32, lw.shape, 0) < t_valid
        lw, a, b, k, v = (jnp.where(live, t, 0.0) for t in (lw, a, b, k, v))
    r = r_ref[...]
    cum = _dot_exact_rhs_lhs(tril_ref[...], lw)
    e_in = jnp.exp(cum)
    e_inv = jnp.exp(-cum)
    at = a * jnp.exp(cum - lw)
    rt = r * e_in
    bt = b * e_inv
    kt = k * e_inv
    g_last = e_in[chunk - 1:chunk, :]

    ti = lax.broadcasted_iota(jnp.int32, (chunk, chunk), 0)
    si = lax.broadcasted_iota(jnp.int32, (chunk, chunk), 1)
    strict = si < ti
    incl = si <= ti
    eye = jnp.where(si == ti, 1.0, 0.0).astype(F32)
    n_sq = max(chunk.bit_length() - 2, 0)

    for h in range(RWKV_HEADS):
        sl = slice(h * RWKV_HEAD_DIM, (h + 1) * RWKV_HEAD_DIM)
        s_h = s_ref[h]
        ar = jnp.concatenate([at[:, sl], rt[:, sl]], axis=0)
        m_b = _dot3(ar, bt[:, sl], NT)
        m_k = _dot3(ar, kt[:, sl], NT)
        a_ab = jnp.where(strict, m_b[:chunk], 0.0)
        a_ak = jnp.where(strict, m_k[:chunk], 0.0)
        r_ab = jnp.where(incl, m_b[chunk:], 0.0)
        r_ak = jnp.where(incl, m_k[chunk:], 0.0)
        x = eye + a_ab
        pw = a_ab
        for _ in range(n_sq):
            pw = _dot3(pw, pw)
            x = x + _dot3(x, pw)
        gs = _dot3(ar, s_h, NT)
        v_h = v[:, sl]
        u = _dot3(x, gs[:chunk] + _dot3(a_ak, v_h))
        y = gs[chunk:] + _dot3(r_ab, u) + _dot3(r_ak, v_h)
        s_ref[h] = (s_h + _dot3(u, bt[:, sl], TN) + _dot3(v_h, kt[:, sl], TN)) * g_last[:, sl]
        mu = jnp.mean(y, axis=-1, keepdims=True)
        var = jnp.mean(jnp.square(y - mu), axis=-1, keepdims=True)
        yn = (y - mu) * lax.rsqrt(var + GN_EPS) * lnw_ref[:, sl] + lnb_ref[:, sl]
        bonus = jnp.sum(r[:, sl] * k[:, sl] * rk_ref[:, sl], axis=-1, keepdims=True) * v_h
        o_ref[:, sl] = (yn + bonus) * g_ref[:, sl]


def _dot_exact_rhs_lhs(l_bf16, x):
    hi = x.astype(BF16)
    r1 = x - hi.astype(F32)
    mid = r1.astype(BF16)
    lo = (r1 - mid.astype(F32)).astype(BF16)
    return _dg(l_bf16, hi, NN) + _dg(l_bf16, mid, NN) + _dg(l_bf16, lo, NN)


def _rwkv_scan(streams, s0, p, chunk, t_valid):
    bsz, t, _ = streams[0].shape
    row = lambda a: a.reshape(1, B_WIDTH)
    tril = (jnp.arange(chunk)[None, :] <= jnp.arange(chunk)[:, None]).astype(BF16)
    const = lambda b, c: (0, 0)
    blk = pl.BlockSpec((None, chunk, B_WIDTH), lambda b, c: (b, c, 0))
    st = pl.BlockSpec((None, RWKV_HEADS, RWKV_HEAD_DIM, RWKV_HEAD_DIM), lambda b, c: (b, 0, 0, 0))
    vec = pl.BlockSpec((1, B_WIDTH), const)
    return pl.pallas_call(
        functools.partial(_rwkv_scan_kernel, chunk=chunk, t_valid=t_valid),
        grid=(bsz, t // chunk),
        in_specs=[blk] * 7 + [st, vec, vec, vec, pl.BlockSpec((chunk, chunk), const)],
        out_specs=[blk, st],
        out_shape=[jax.ShapeDtypeStruct((bsz, t, B_WIDTH), F32), jax.ShapeDtypeStruct(s0.shape, F32)],
        compiler_params=_cparams("parallel", "arbitrary"),
        name=f"rwkv_scan_c{chunk}",
    )(*streams, s0, row(p["r_k"]), row(p["ln_x_w"]), row(p["ln_x_b"]), tril)


def _finish_kernel(x_ref, n0, d0, m0, n1, d1, m1, n2, d2, m2, ob_ref, g_ref,
                   wpa_ref, wpb_ref, wo_ref, g2_ref, rw_ref, rb_ref, h_ref, hn_ref, lg_ref):
    mx = jnp.maximum(jnp.maximum(m0[...], m1[...]), m2[...])
    e0, e1, e2 = jnp.exp(m0[...] - mx), jnp.exp(m1[...] - mx), jnp.exp(m2[...] - mx)
    num = n0[...] * e0 + n1[...] * e1 + n2[...] * e2
    den = d0[...] * e0 + d1[...] * e1 + d2[...] * e2
    o_a = num / den
    br_a = _dot1(o_a, wpa_ref[...])
    br_b = _dot1(ob_ref[...], wpb_ref[...])
    mix = g_ref[:, :D_MODEL] * br_a + g_ref[:, D_MODEL:] * br_b
    h = x_ref[...] + _dot1(mix, wo_ref[...])
    h_ref[...] = h
    hn = h * lax.rsqrt(jnp.mean(h * h, axis=-1, keepdims=True) + RMS_EPS) * g2_ref[...]
    hn_ref[...] = hn
    lg_ref[...] = _dot3(hn, rw_ref[...]) + rb_ref[...]


def _finish(x2, stats, o_b, gates, wb, tm):
    n = x2.shape[0]
    const = lambda i: (0, 0)
    rows = lambda w: pl.BlockSpec((tm, w), lambda i: (i, 0))
    params = [wb["w_pa"], wb["w_pb"], wb["w_o"], wb["norm2_g"], wb["router_w"], wb["router_b"]]
    return pl.pallas_call(
        _finish_kernel,
        grid=(n // tm,),
        in_specs=[rows(D_MODEL)] + [rows(A_WIDTH)] * 9 + [rows(B_WIDTH), rows(2 * D_MODEL)]
                 + [pl.BlockSpec(a.shape, const) for a in params],
        out_specs=[rows(D_MODEL), rows(D_MODEL), rows(LOGIT_PAD)],
        out_shape=[jax.ShapeDtypeStruct((n, D_MODEL), F32), jax.ShapeDtypeStruct((n, D_MODEL), F32),
                   jax.ShapeDtypeStruct((n, LOGIT_PAD), F32)],
        compiler_params=_cparams("parallel"),
        name="finish",
    )(x2, *stats, o_b, gates, *params)


def _row_copy(src_hbm, src_row, buf, slot, dst_row, sem):
    return pltpu.make_async_copy(src_hbm.at[pl.ds(src_row, 1)], buf.at[slot, pl.ds(dst_row, 1)], sem.at[slot])


def _moe_kernel(blk_e_ref, nused_ref, tok_ref, tok_next_ref, hn_hbm, wgu_ref, bgu_ref, wd_ref, bdn_ref,
                ys_ref, xbuf, sem):
    i = pl.program_id(0)
    nused = nused_ref[0]
    slot = i % 2

    def issue(idx_ref, s):
        def body(rw, carry):
            _row_copy(hn_hbm, idx_ref[0, 0, rw], xbuf, s, rw, sem).start()
            return carry
        lax.fori_loop(0, MOE_BLOCK, body, 0)

    @pl.when(jnp.logical_and(i == 0, nused > 0))
    def _():
        issue(tok_ref, 0)

    @pl.when(i + 1 < nused)
    def _():
        issue(tok_next_ref, 1 - slot)

    @pl.when(i < nused)
    def _():
        def wait_body(rw, carry):
            _row_copy(hn_hbm, 0, xbuf, slot, rw, sem).wait()
            return carry
        lax.fori_loop(0, MOE_BLOCK, wait_body, 0)
        x = xbuf[slot].astype(BF16)
        acc = jnp.zeros((MOE_BLOCK, D_MODEL), F32)
        for c in range(D_FF // COL_CHUNK):
            lo = c * COL_CHUNK
            gate = _dg(x, wgu_ref[:, lo:lo + COL_CHUNK], NN) + bgu_ref[:, lo:lo + COL_CHUNK]
            up = _dg(x, wgu_ref[:, D_FF + lo:D_FF + lo + COL_CHUNK], NN) + bgu_ref[:, D_FF + lo:D_FF + lo + COL_CHUNK]
            gate = jnp.minimum(gate, SWIGLU_LIMIT)
            up = jnp.clip(up, -SWIGLU_LIMIT, SWIGLU_LIMIT)
            hid = (up + 1.0) * gate * _sigmoid(gate * SWIGLU_ALPHA)
            acc = acc + _dg(hid.astype(BF16), wd_ref[lo:lo + COL_CHUNK, :], NN)
        ys_ref[...] = acc + bdn_ref[...]

    @pl.when(i >= nused)
    def _():
        ys_ref[...] = jnp.zeros_like(ys_ref)


def _moe_blocks(hn, row_tok, blk_e, nused, wb):
    n_blocks = row_tok.shape[0]
    last = n_blocks - 1
    grid_spec = pltpu.PrefetchScalarGridSpec(
        num_scalar_prefetch=2,
        grid=(n_blocks,),
        in_specs=[
            pl.BlockSpec((1, 1, MOE_BLOCK), lambda i, be, nu: (i, 0, 0), memory_space=pltpu.SMEM),
            pl.BlockSpec((1, 1, MOE_BLOCK), lambda i, be, nu: (jnp.minimum(i + 1, last), 0, 0),
                         memory_space=pltpu.SMEM),
            pl.BlockSpec(memory_space=pl.ANY),
            pl.BlockSpec((None, D_MODEL, 2 * D_FF), lambda i, be, nu: (be[i], 0, 0)),
            pl.BlockSpec((None, 1, 2 * D_FF), lambda i, be, nu: (be[i], 0, 0)),
            pl.BlockSpec((None, D_FF, D_MODEL), lambda i, be, nu: (be[i], 0, 0)),
            pl.BlockSpec((None, 1, D_MODEL), lambda i, be, nu: (be[i], 0, 0)),
        ],
        out_specs=pl.BlockSpec((MOE_BLOCK, D_MODEL), lambda i, be, nu: (i, 0)),
        scratch_shapes=[pltpu.VMEM((2, MOE_BLOCK, D_MODEL), F32), pltpu.SemaphoreType.DMA((2,))],
    )
    return pl.pallas_call(
        _moe_kernel,
        grid_spec=grid_spec,
        out_shape=jax.ShapeDtypeStruct((n_blocks * MOE_BLOCK, D_MODEL), F32),
        compiler_params=_cparams("arbitrary"),
        name="moe_blocks",
    )(blk_e, nused, row_tok, row_tok, hn, wb["w_gu"], wb["b_gu"], wb["w_down"], wb["b_down"])


def _combine_kernel(pos_ref, pos_next_ref, h_ref, gate_ref, ys_hbm, y_ref, gbuf, sem, *, tm):
    i = pl.program_id(0)
    n = pl.num_programs(0)
    slot = i % 2
    rows = TOP_K * tm

    def issue(idx_ref, s):
        def body(kk, carry):
            _row_copy(ys_hbm, idx_ref[0, 0, kk], gbuf, s, (kk % TOP_K) * tm + kk // TOP_K, sem).start()
            return carry
        lax.fori_loop(0, rows, body, 0)

    @pl.when(i == 0)
    def _():
        issue(pos_ref, 0)

    @pl.when(i + 1 < n)
    def _():
        issue(pos_next_ref, 1 - slot)

    def wait_body(rw, carry):
        _row_copy(ys_hbm, 0, gbuf, slot, rw, sem).wait()
        return carry
    lax.fori_loop(0, rows, wait_body, 0)
    gv = gate_ref[...]
    ff = gv[:, 0:1] * gbuf[slot, 0:tm, :]
    for j in range(1, TOP_K):
        ff = ff + gv[:, j:j + 1] * gbuf[slot, j * tm:(j + 1) * tm, :]
    y_ref[...] = h_ref[...] + ff


def _combine(h, gates_pad, pos, ys, tm=64):
    n = h.shape[0]
    steps = n // tm
    rows = TOP_K * tm
    pos3 = pos.reshape(steps, 1, rows)
    return pl.pallas_call(
        functools.partial(_combine_kernel, tm=tm),
        grid=(steps,),
        in_specs=[
            pl.BlockSpec((1, 1, rows), lambda i: (i, 0, 0), memory_space=pltpu.SMEM),
            pl.BlockSpec((1, 1, rows), lambda i: (jnp.minimum(i + 1, steps - 1), 0, 0), memory_space=pltpu.SMEM),
            pl.BlockSpec((tm, D_MODEL), lambda i: (i, 0)),
            pl.BlockSpec((tm, LOGIT_PAD), lambda i: (i, 0)),
            pl.BlockSpec(memory_space=pl.ANY),
        ],
        out_specs=pl.BlockSpec((tm, D_MODEL), lambda i: (i, 0)),
        out_shape=jax.ShapeDtypeStruct((n, D_MODEL), F32),
        scratch_shapes=[pltpu.VMEM((2, rows, D_MODEL), F32), pltpu.SemaphoreType.DMA((2,))],
        compiler_params=_cparams("arbitrary"),
        name="moe_combine",
    )(pos3, pos3, h, gates_pad, ys)


def _route(logits):
    n = logits.shape[0]
    nk = n * TOP_K
    top_v, top_e = lax.top_k(logits, TOP_K)
    gates = jax.nn.softmax(top_v, axis=-1)
    flat_e = top_e.reshape(nk)
    order = jnp.argsort(flat_e)
    se = flat_e[order]
    counts = jnp.bincount(flat_e, length=N_EXPERTS)
    starts = jnp.cumsum(counts) - counts
    pcounts = (counts + MOE_BLOCK - 1) // MOE_BLOCK * MOE_BLOCK
    pends = jnp.cumsum(pcounts)
    pstarts = pends - pcounts
    dest = (pstarts[se] + jnp.arange(nk) - starts[se]).astype(jnp.int32)
    n_blocks = -(-nk // MOE_BLOCK) + N_EXPERTS
    row_tok = jnp.zeros((n_blocks * MOE_BLOCK,), jnp.int32).at[dest].set((order // TOP_K).astype(jnp.int32))
    pos = jnp.zeros((nk,), jnp.int32).at[order].set(dest)
    blk_e = jnp.minimum(jnp.searchsorted(pends, jnp.arange(n_blocks) * MOE_BLOCK, side='right'),
                        N_EXPERTS - 1).astype(jnp.int32)
    nused = (pends[-1:] // MOE_BLOCK).astype(jnp.int32)
    gates_pad = jnp.pad(gates, ((0, 0), (0, LOGIT_PAD - TOP_K)))
    return row_tok.reshape(n_blocks, 1, MOE_BLOCK), blk_e, nused, pos, gates_pad


def _moe(h, hn, logits_pad, wb):
    row_tok, blk_e, nused, pos, gates_pad = _route(logits_pad[:, :N_EXPERTS])
    ys = _moe_blocks(hn, row_tok, blk_e, nused, wb)
    return _combine(h, gates_pad, pos, ys)


def kernel(x_prompt, x_sample, cache_kv_w128, cache_kv_w512, cache_kv_w2048, state_wkv, state_shift,
           norm1_g, w_in, q_norm_g, k_norm_g, mu_shift, w0, w2, a0, a2, g2, k_k, k_a, r_k,
           ln_x_w, ln_x_b, w_pa, w_pb, w_o, norm2_g, router_w, router_b, w_gu, b_gu, w_down, b_down):
    bp, seq, _ = x_prompt.shape
    bd, t_s, _ = x_sample.shape
    assert t_s == 1
    rw = dict(mu_shift=mu_shift, w0=w0, w2=w2, a0=a0, a2=a2, g2=g2, k_k=k_k, k_a=k_a, r_k=r_k,
              ln_x_w=ln_x_w, ln_x_b=ln_x_b)
    wb = dict(
        w_pa=w_pa.astype(BF16), w_pb=w_pb.astype(BF16), w_o=w_o.astype(BF16),
        norm2_g=norm2_g.reshape(1, D_MODEL),
        router_w=jnp.pad(router_w, ((0, 0), (0, LOGIT_PAD - N_EXPERTS))),
        router_b=jnp.pad(router_b, (0, LOGIT_PAD - N_EXPERTS)).reshape(1, LOGIT_PAD),
        w_gu=w_gu.astype(BF16), b_gu=b_gu.reshape(N_EXPERTS, 1, 2 * D_FF),
        w_down=w_down.astype(BF16), b_down=b_down.reshape(N_EXPERTS, 1, D_MODEL),
    )
    w_in_b = w_in.astype(BF16)
    mult = jnp.stack([jnp.tile(q_norm_g[c // 3] * (HEAD_DIM ** -0.5) if c % 3 == 0 else
                               (k_norm_g[c // 3] if c % 3 == 1 else jnp.ones((HEAD_DIM,), F32)),
                               HEADS_PER_GROUP) for c in range(A_COLS // COL_CHUNK)])
    bd_head = _seg_ones(COL_CHUNK, HEAD_DIM, 1.0 / HEAD_DIM)

    xp2 = x_prompt.reshape(bp * seq, D_MODEL)
    pa, ps, pg = _in_proj(xp2, norm1_g, w_in_b, mult, bd_head, tm=256)
    pa3 = pa.reshape(bp, seq, A_COLS)
    stats = []
    kv_p = []
    for g in range(N_GROUPS):
        stats += _attn_prompt(pa3, g)
        keep = min(GROUP_WINDOWS[g], seq)
        lo = g * 3 * A_WIDTH + A_WIDTH
        kv_p.append(pa3[:, seq - keep:, lo:lo + 2 * A_WIDTH].reshape(bp, keep, 2, HEADS_PER_GROUP, HEAD_DIM))
    ps3 = ps.reshape(bp, seq, SHIFT_WIDTH)
    streams = _rwkv_prep(ps3, jnp.zeros((bp, 1, SHIFT_WIDTH), F32), rw, tm=512)
    o_b, wkv_p = _rwkv_scan(streams, jnp.zeros((bp, RWKV_HEADS, RWKV_HEAD_DIM, RWKV_HEAD_DIM), F32), rw,
                            chunk=SCAN_CHUNK, t_valid=SCAN_CHUNK)
    h, hn, lg = _finish(xp2, stats, o_b.reshape(bp * seq, B_WIDTH), pg, wb, tm=256)
    y_prompt = _moe(h, hn, lg, wb).reshape(bp, seq, D_MODEL)
    shift_p = ps3[:, -1:]

    caches = (cache_kv_w128, cache_kv_w512, cache_kv_w2048)
    xs2 = x_sample.reshape(bd, D_MODEL)
    pa_s, ps_s, pg_s = _in_proj(xs2, norm1_g, w_in_b, mult, bd_head, tm=bd)
    kvs = [caches[g][:, ::GROUP_DILATIONS[g]].reshape(bd, N_BACK, 2 * A_WIDTH) for g in range(N_GROUPS)]
    stats_s = _attn_sample(pa_s, kvs)
    kv_s = []
    for g in range(N_GROUPS):
        lo = g * 3 * A_WIDTH + A_WIDTH
        new = pa_s[:, lo:lo + 2 * A_WIDTH].reshape(bd, 1, 2, HEADS_PER_GROUP, HEAD_DIM).astype(caches[g].dtype)
        kv_s.append(jnp.concatenate([caches[g][:, 1:], new], axis=1))
    pad_t = 8
    ps_s3 = jnp.pad(ps_s.reshape(bd, 1, SHIFT_WIDTH), ((0, 0), (0, pad_t - 1), (0, 0)))
    streams_s = _rwkv_prep(ps_s3, state_shift.astype(F32), rw, tm=pad_t)
    o_b_s, wkv_s = _rwkv_scan(streams_s, state_wkv.astype(F32), rw, chunk=pad_t, t_valid=1)
    h_s, hn_s, lg_s = _finish(xs2, stats_s, o_b_s[:, 0], pg_s, wb, tm=bd)
    y_sample = _moe(h_s, hn_s, lg_s, wb).reshape(bd, 1, D_MODEL)
    shift_s = ps_s.reshape(bd, 1, SHIFT_WIDTH)

    return (y_prompt, y_sample, kv_p[0], kv_p[1], kv_p[2], wkv_p.astype(state_wkv.dtype), shift_p,
            kv_s[0], kv_s[1], kv_s[2], wkv_s.astype(state_wkv.dtype), shift_s.astype(state_shift.dtype))
```

```python
import functools

import jax
import jax.numpy as jnp
from jax import lax
from jax.experimental import pallas as pl
from jax.experimental.pallas import tpu as pltpu

F32 = jnp.float32
BF16 = jnp.bfloat16

D_MODEL = 1024
GROUP_WINDOWS = (128, 512, 2048)
GROUP_DILATIONS = (1, 4, 16)
N_GROUPS = 3
HEADS_PER_GROUP = 4
HEAD_DIM = 64
A_WIDTH = HEADS_PER_GROUP * HEAD_DIM
A_COLS = N_GROUPS * 3 * A_WIDTH
RWKV_HEADS = 8
RWKV_HEAD_DIM = 64
B_WIDTH = RWKV_HEADS * RWKV_HEAD_DIM
DECAY_LORA = 64
AAA_LORA = 64
GATE_LORA = 128
SHIFT_WIDTH = 3 * B_WIDTH + DECAY_LORA + AAA_LORA + GATE_LORA
D_IN = A_COLS + SHIFT_WIDTH + 2 * D_MODEL
N_EXPERTS = 32
TOP_K = 4
D_FF = D_MODEL
SWIGLU_LIMIT = 7.0
SWIGLU_ALPHA = 1.702
MOE_BLOCK = 256
RMS_EPS = 1e-6
GN_EPS = 64e-5

N_BACK = 128
COL_CHUNK = 256
SCAN_CHUNK = 64
LANE_TILE = 128
LOGIT_PAD = LANE_TILE
DISPATCH_TOKENS = 256
COMBINE_TOKENS = 128
VMEM_LIMIT = 48 * 1024 * 1024


def _cparams(*sem):
    return pltpu.CompilerParams(dimension_semantics=sem, vmem_limit_bytes=VMEM_LIMIT)


def _sigmoid(x):
    return 1.0 / (1.0 + jnp.exp(-x))


def _split(x):
    hi = x.astype(BF16)
    lo = (x - hi.astype(F32)).astype(BF16)
    return hi, lo


def _dg(a, b, dims):
    return lax.dot_general(a, b, (dims, ((), ())), preferred_element_type=F32)


NN = ((1,), (0,))
NT = ((1,), (1,))
TN = ((0,), (0,))


def _dot1(a, b, dims=NN):
    return _dg(a.astype(BF16), b.astype(BF16), dims)


def _dot3(a, b, dims=NN):
    ah, al = _split(a)
    bh, bl = _split(b)
    return _dg(ah, bh, dims) + _dg(al, bh, dims) + _dg(ah, bl, dims)


def _split3(x):
    hi = x.astype(BF16)
    r1 = x - hi.astype(F32)
    mid = r1.astype(BF16)
    lo = (r1 - mid.astype(F32)).astype(BF16)
    return hi, mid, lo


def _dot_exact_rhs(a, b_bf16):
    return sum(_dg(t, b_bf16, NN) for t in _split3(a))


def _dot_exact_lhs(l_bf16, x):
    return sum(_dg(l_bf16, t, NN) for t in _split3(x))


def _seg_ones(width, seg, scale):
    i = jnp.arange(width)[:, None] // seg
    j = jnp.arange(width)[None, :] // seg
    return jnp.where(i == j, scale, 0.0).astype(BF16)


def _in_proj_kernel(x_ref, g1_ref, w_ref, mult_ref, bd_ref, pa_ref, ps_ref, pg_ref):
    x = x_ref[...]
    xn = x * lax.rsqrt(jnp.mean(x * x, axis=-1, keepdims=True) + RMS_EPS) * g1_ref[...]
    xb = xn.astype(BF16)
    bd = bd_ref[...]
    for c in range(A_COLS // COL_CHUNK):
        acc = _dg(xb, w_ref[:, c * COL_CHUNK:(c + 1) * COL_CHUNK], NN)
        if c % 3 != 2:
            ms = _dot_exact_rhs(acc * acc, bd)
            acc = acc * lax.rsqrt(ms + RMS_EPS) * mult_ref[c:c + 1, :]
        pa_ref[:, c * COL_CHUNK:(c + 1) * COL_CHUNK] = acc
    for c in range(SHIFT_WIDTH // COL_CHUNK):
        lo = A_COLS + c * COL_CHUNK
        ps_ref[:, c * COL_CHUNK:(c + 1) * COL_CHUNK] = _dg(xb, w_ref[:, lo:lo + COL_CHUNK], NN)
    for c in range(2 * D_MODEL // COL_CHUNK):
        lo = A_COLS + SHIFT_WIDTH + c * COL_CHUNK
        pg_ref[:, c * COL_CHUNK:(c + 1) * COL_CHUNK] = _sigmoid(_dg(xb, w_ref[:, lo:lo + COL_CHUNK], NN))


def _in_proj(x2, norm1_g, w_in_b, mult, bd, tm):
    n = x2.shape[0]
    const = lambda i: (0, 0)
    return pl.pallas_call(
        _in_proj_kernel,
        grid=(n // tm,),
        in_specs=[
            pl.BlockSpec((tm, D_MODEL), lambda i: (i, 0)),
            pl.BlockSpec((1, D_MODEL), const),
            pl.BlockSpec((D_MODEL, D_IN), const, pipeline_mode=pl.Buffered(1)),
            pl.BlockSpec(mult.shape, const),
            pl.BlockSpec(bd.shape, const),
        ],
        out_specs=[
            pl.BlockSpec((tm, A_COLS), lambda i: (i, 0)),
            pl.BlockSpec((tm, SHIFT_WIDTH), lambda i: (i, 0)),
            pl.BlockSpec((tm, 2 * D_MODEL), lambda i: (i, 0)),
        ],
        out_shape=[
            jax.ShapeDtypeStruct((n, A_COLS), F32),
            jax.ShapeDtypeStruct((n, SHIFT_WIDTH), F32),
            jax.ShapeDtypeStruct((n, 2 * D_MODEL), F32),
        ],
        compiler_params=_cparams("parallel"),
        name="in_proj",
    )(x2, norm1_g.reshape(1, D_MODEL), w_in_b, mult, bd)


def _attn_kernel(*refs, dil):
    halves = A_WIDTH // LANE_TILE
    q_refs, kp_refs, kc_refs, vp_refs, vc_refs = (refs[i * halves:(i + 1) * halves] for i in range(5))
    out_refs = refs[5 * halves:5 * halves + 3]
    qs, ks, vs = refs[5 * halves + 3:5 * halves + 6]
    stat_s = refs[5 * halves + 6:5 * halves + 9]
    stage = refs[5 * halves + 9:]
    n = pl.program_id(1)
    qi = lax.broadcasted_iota(jnp.int32, (N_BACK, 2 * N_BACK), 0)
    kj = lax.broadcasted_iota(jnp.int32, (N_BACK, 2 * N_BACK), 1)
    first_prev = jnp.where(n > 0, 0, N_BACK)
    lo = jnp.where(kj < N_BACK, qi + first_prev, N_BACK)
    hi = jnp.where(kj < N_BACK, N_BACK - 1, qi + N_BACK)
    mask = jnp.logical_and(kj >= lo, kj <= hi)

    def stream(r, carry):
        rows = pl.ds(r, N_BACK, stride=dil) if dil > 1 else slice(None)
        for c in range(halves):
            lanes = slice(c * LANE_TILE, (c + 1) * LANE_TILE)
            qs[:, lanes] = q_refs[c][rows, :]
            ks[0:N_BACK, lanes] = kp_refs[c][rows, :]
            ks[N_BACK:, lanes] = kc_refs[c][rows, :]
            vs[0:N_BACK, lanes] = vp_refs[c][rows, :]
            vs[N_BACK:, lanes] = vc_refs[c][rows, :]
        sls = [slice(h * HEAD_DIM, (h + 1) * HEAD_DIM) for h in range(HEADS_PER_GROUP)]
        s = [jnp.where(mask, _dot1(qs[:, sl], ks[:, sl], NT), -jnp.inf) for sl in sls]
        m = [jnp.max(t, axis=-1, keepdims=True) for t in s]
        p = [jnp.exp(t - mm) for t, mm in zip(s, m)]
        den = [jnp.sum(t, axis=-1, keepdims=True) for t in p]
        num = [_dot1(t, vs[:, sl]) for t, sl in zip(p, sls)]
        for h, sl in enumerate(sls):
            stat_s[0][:, sl] = num[h]
            stat_s[1][:, sl] = jnp.broadcast_to(den[h], (N_BACK, HEAD_DIM))
            stat_s[2][:, sl] = jnp.broadcast_to(m[h], (N_BACK, HEAD_DIM))
        for i in range(3):
            if dil > 1:
                for c in range(halves):
                    stage[i * halves + c][rows, :] = stat_s[i][:, c * LANE_TILE:(c + 1) * LANE_TILE]
            else:
                out_refs[i][...] = stat_s[i][...]
        return carry

    if dil > 1:
        lax.fori_loop(0, dil, stream, 0)
        for i in range(3):
            for c in range(halves):
                out_refs[i][:, c * LANE_TILE:(c + 1) * LANE_TILE] = stage[i * halves + c][...]
    else:
        stream(0, 0)


def _attn_prompt(pa3, g):
    bsz, seq, _ = pa3.shape
    dil = GROUP_DILATIONS[g]
    rows = N_BACK * dil
    nb = seq // rows
    halves = A_WIDTH // LANE_TILE

    def specs(slab, prev):
        def one(c):
            col = (3 * g + slab) * halves + c
            if prev:
                return pl.BlockSpec((None, rows, LANE_TILE), lambda b, n: (b, jnp.maximum(n - 1, 0), col))
            return pl.BlockSpec((None, rows, LANE_TILE), lambda b, n: (b, n, col))
        return [one(c) for c in range(halves)]

    in_specs = specs(0, False) + specs(1, True) + specs(1, False) + specs(2, True) + specs(2, False)
    out_spec = pl.BlockSpec((None, rows, A_WIDTH), lambda b, n: (b, n, 0))
    out_sds = jax.ShapeDtypeStruct((bsz, seq, A_WIDTH), F32)
    scratch = [pltpu.VMEM((N_BACK, A_WIDTH), F32), pltpu.VMEM((2 * N_BACK, A_WIDTH), F32),
               pltpu.VMEM((2 * N_BACK, A_WIDTH), F32)] + [pltpu.VMEM((N_BACK, A_WIDTH), F32)] * 3
    if dil > 1:
        scratch += [pltpu.VMEM((rows, LANE_TILE), F32)] * (3 * halves)
    outs = pl.pallas_call(
        functools.partial(_attn_kernel, dil=dil),
        grid=(bsz, nb),
        in_specs=in_specs,
        out_specs=[out_spec] * 3,
        out_shape=[out_sds] * 3,
        scratch_shapes=scratch,
        compiler_params=_cparams("parallel", "arbitrary"),
        name=f"attn_prompt_g{g}",
    )(*([pa3] * len(in_specs)))
    return [o.reshape(bsz * seq, A_WIDTH) for o in outs]


def _sample_cache_kernel(pat_ref, c0_ref, c1_ref, c2_ref, o0_ref, o1_ref, o2_ref, oa_ref):
    b = pl.program_id(0)
    n_b = pat_ref.shape[1]

    @pl.when(b == 0)
    def _():
        oa_ref[...] = jnp.zeros_like(oa_ref)

    lane_b = lax.broadcasted_iota(jnp.int32, pat_ref.shape, 1) == b
    col = jnp.sum(jnp.where(lane_b, pat_ref[...], 0.0), axis=1, keepdims=True)
    c_refs = (c0_ref, c1_ref, c2_ref)
    o_refs = (o0_ref, o1_ref, o2_ref)
    stats = []
    for g in range(N_GROUPS):
        dil = GROUP_DILATIONS[g]
        win = GROUP_WINDOWS[g]
        base = g * 3 * A_WIDTH
        lane = lax.broadcasted_iota(jnp.int32, (HEAD_DIM, win), 1)
        last = lane == win - 1
        live = lax.broadcasted_iota(jnp.int32, (1, win), 1) % dil == 0
        for h in range(HEADS_PER_GROUP):
            o = h * HEAD_DIM
            q = col[base + o:base + o + HEAD_DIM]
            kn = col[base + A_WIDTH + o:base + A_WIDTH + o + HEAD_DIM]
            vn = col[base + 2 * A_WIDTH + o:base + 2 * A_WIDTH + o + HEAD_DIM]
            kt = c_refs[g][0, 0, h]
            vt = c_refs[g][0, 1, h]
            o_refs[g][0, 0, h] = jnp.where(last, kn, pltpu.roll(kt, win - 1, axis=1))
            o_refs[g][0, 1, h] = jnp.where(last, vn, pltpu.roll(vt, win - 1, axis=1))
            s_c = jnp.where(live, jnp.sum(kt * q, axis=0, keepdims=True), -jnp.inf)
            s_n = jnp.sum(kn * q, axis=0, keepdims=True)
            m = jnp.maximum(jnp.max(s_c, axis=1, keepdims=True), s_n)
            p_c = jnp.exp(s_c - m)
            p_n = jnp.exp(s_n - m)
            den = jnp.sum(p_c, axis=1, keepdims=True) + p_n
            num = jnp.sum(vt * p_c, axis=1, keepdims=True) + p_n * vn
            stats.append((num, den, m))
    outs = []
    for h in range(HEADS_PER_GROUP):
        per_g = [stats[g * HEADS_PER_GROUP + h] for g in range(N_GROUPS)]
        mx = functools.reduce(jnp.maximum, [m for _, _, m in per_g])
        num = sum(n_ * jnp.exp(m - mx) for n_, _, m in per_g)
        den = sum(d_ * jnp.exp(m - mx) for _, d_, m in per_g)
        outs.append(num / den)
    o_col = jnp.concatenate(outs, axis=0)
    lane_o = lax.broadcasted_iota(jnp.int32, (A_WIDTH, n_b), 1) == b
    oa_ref[...] = jnp.where(lane_o, o_col, oa_ref[...])


def _sample_cache(pa_t, caches_t):
    bd = pa_t.shape[1]
    cspecs = [pl.BlockSpec((1, 2, HEADS_PER_GROUP, HEAD_DIM, w), lambda b: (b, 0, 0, 0, 0)) for w in GROUP_WINDOWS]
    return pl.pallas_call(
        _sample_cache_kernel,
        grid=(bd,),
        in_specs=[pl.BlockSpec(pa_t.shape, lambda b: (0, 0))] + cspecs,
        out_specs=cspecs + [pl.BlockSpec((A_WIDTH, bd), lambda b: (0, 0))],
        out_shape=[jax.ShapeDtypeStruct(c.shape, c.dtype) for c in caches_t]
                  + [jax.ShapeDtypeStruct((A_WIDTH, bd), F32)],
        compiler_params=_cparams("arbitrary"),
        name="sample_cache",
    )(pa_t, *caches_t)


def _rwkv_prep_math(pf, shifted, mu_ref, w0_ref, w2_ref, a0_ref, a2_ref, g2_ref, kk_ref, ka_ref, bd_ref):
    z = pf + mu_ref[...] * (shifted - pf)
    c1, c2, c3 = B_WIDTH, 2 * B_WIDTH, 3 * B_WIDTH
    r, k, v = z[:, :c1], z[:, c1:c2], z[:, c2:c3]
    wl = z[:, c3:c3 + DECAY_LORA]
    al = z[:, c3 + DECAY_LORA:c3 + DECAY_LORA + AAA_LORA]
    gl = z[:, c3 + DECAY_LORA + AAA_LORA:]
    xw = w0_ref[...] + _dot3(jnp.tanh(wl), w2_ref[...])
    w_log = -(jnp.maximum(-xw, 0.0) + jnp.log(1.0 + jnp.exp(-jnp.abs(xw)))) - 0.5
    lw = -jnp.exp(w_log)
    a = _sigmoid(a0_ref[...] + _dot3(al, a2_ref[...]))
    g = _dot3(_sigmoid(gl), g2_ref[...])
    kk = k * kk_ref[...]
    bd = bd_ref[...]
    kks = []
    for c in range(B_WIDTH // 128):
        kc = kk[:, c * 128:(c + 1) * 128]
        nrm = jnp.sqrt(_dot_exact_rhs(kc * kc, bd))
        kks.append(kc / jnp.maximum(nrm, 1e-12))
    kkn = jnp.concatenate(kks, axis=1)
    return r, lw, k * (1.0 + (a - 1.0) * ka_ref[...]), v, -kkn, kkn * a, g


def _rwkv_prep_kernel(ps_ref, pv_ref, p0_ref, *refs):
    param_refs, out_refs = refs[:9], refs[9:]
    i = pl.program_id(1)
    pf = ps_ref[...]
    prev_row = jnp.where(i == 0, p0_ref[...], pv_ref[7:8, :])
    row = lax.broadcasted_iota(jnp.int32, pf.shape, 0)
    shifted = jnp.where(row == 0, prev_row, pltpu.roll(pf, 1, axis=0))
    for o_ref, val in zip(out_refs, _rwkv_prep_math(pf, shifted, *param_refs)):
        o_ref[...] = val


def _rwkv_prep_t_kernel(ps_ref, prev_ref, *refs):
    param_refs, out_refs = refs[:9], refs[9:]
    for o_ref, val in zip(out_refs, _rwkv_prep_math(ps_ref[...], prev_ref[...], *param_refs)):
        o_ref[...] = val.T


def _prep_params(p):
    row = lambda a: a.reshape(1, -1)
    return [row(p["mu_shift"]), row(p["w0"]), p["w2"], row(p["a0"]), p["a2"], p["g2"],
            row(p["k_k"]), row(p["k_a"]), _seg_ones(128, RWKV_HEAD_DIM, 1.0)]


def _rwkv_prep(ps3, prev0, p, tm):
    bsz, t, _ = ps3.shape
    const = lambda b, i: (0, 0)
    params = _prep_params(p)
    out_spec = pl.BlockSpec((None, tm, B_WIDTH), lambda b, i: (b, i, 0))
    out_sds = jax.ShapeDtypeStruct((bsz, t, B_WIDTH), F32)
    return pl.pallas_call(
        _rwkv_prep_kernel,
        grid=(bsz, t // tm),
        in_specs=[
            pl.BlockSpec((None, tm, SHIFT_WIDTH), lambda b, i: (b, i, 0)),
            pl.BlockSpec((None, 8, SHIFT_WIDTH), lambda b, i: (b, jnp.maximum(i * (tm // 8) - 1, 0), 0)),
            pl.BlockSpec((None, 1, SHIFT_WIDTH), lambda b, i: (b, 0, 0)),
        ] + [pl.BlockSpec(a.shape, const) for a in params],
        out_specs=[out_spec] * 7,
        out_shape=[out_sds] * 7,
        compiler_params=_cparams("parallel", "arbitrary"),
        name="rwkv_prep",
    )(ps3, ps3, prev0, *params)


def _rwkv_prep_t(ps2, prev2, p):
    bd = ps2.shape[0]
    params = _prep_params(p)
    full = lambda a: pl.BlockSpec(a.shape, lambda i: (0, 0))
    out_sds = jax.ShapeDtypeStruct((B_WIDTH, bd), F32)
    return pl.pallas_call(
        _rwkv_prep_t_kernel,
        grid=(1,),
        in_specs=[full(ps2), full(prev2)] + [full(a) for a in params],
        out_specs=[pl.BlockSpec((B_WIDTH, bd), lambda i: (0, 0))] * 7,
        out_shape=[out_sds] * 7,
        compiler_params=_cparams("arbitrary"),
        name="rwkv_prep_t",
    )(ps2, prev2, *params)


def _rwkv_scan_kernel(r_ref, lw_ref, k_ref, v_ref, a_ref, b_ref, g_ref, rk_ref, lnw_ref, lnb_ref,
                      tril_ref, o_ref, s_ref):
    chunk = r_ref.shape[0]

    @pl.when(pl.program_id(1) == 0)
    def _():
        s_ref[...] = jnp.zeros_like(s_ref)

    lw = lw_ref[...]
    r, k, v = r_ref[...], k_ref[...], v_ref[...]
    cum = _dot_exact_lhs(tril_ref[...], lw)
    e_in = jnp.exp(cum)
    e_inv = jnp.exp(-cum)
    at = a_ref[...] * jnp.exp(cum - lw)
    rt = r * e_in
    bt = b_ref[...] * e_inv
    kt = k * e_inv
    g_last = e_in[chunk - 1:chunk, :]

    ti = lax.broadcasted_iota(jnp.int32, (chunk, 2 * chunk), 0)
    si = lax.broadcasted_iota(jnp.int32, (chunk, 2 * chunk), 1)
    si = jnp.where(si >= chunk, si - chunk, si)
    strict = si < ti
    incl = si <= ti
    eye = jnp.where(lax.broadcasted_iota(jnp.int32, (chunk, chunk), 0)
                    == lax.broadcasted_iota(jnp.int32, (chunk, chunk), 1), 1.0, 0.0).astype(F32)
    n_sq = max(chunk.bit_length() - 2, 0)

    heads = range(RWKV_HEADS)
    sls = [slice(h * RWKV_HEAD_DIM, (h + 1) * RWKV_HEAD_DIM) for h in heads]
    ar = [jnp.concatenate([at[:, sl], rt[:, sl]], axis=0) for sl in sls]
    bk = [jnp.concatenate([bt[:, sl], kt[:, sl]], axis=0) for sl in sls]
    s_old = [s_ref[h] for h in heads]
    m_all = [_dot1(ar[h], bk[h], NT) for h in heads]
    gs = [_dot1(ar[h], s_old[h], NT) for h in heads]
    a_m = [jnp.where(strict, m_all[h][:chunk], 0.0) for h in heads]
    r_m = [jnp.where(incl, m_all[h][chunk:], 0.0) for h in heads]
    pw = [a_m[h][:, :chunk] for h in heads]
    x = [eye + pw[h] for h in heads]
    for _ in range(n_sq):
        pw = [_dot1(pw[h], pw[h]) for h in heads]
        x = [x[h] + _dot1(x[h], pw[h]) for h in heads]
    v_h = [v[:, sl] for sl in sls]
    av = [_dot1(a_m[h][:, chunk:], v_h[h]) for h in heads]
    u = [_dot1(x[h], gs[h][:chunk] + av[h]) for h in heads]
    uv = [jnp.concatenate([u[h], v_h[h]], axis=0) for h in heads]
    y = [gs[h][chunk:] + _dot1(r_m[h], uv[h]) for h in heads]
    for h in heads:
        s_ref[h] = (s_old[h] + _dot3(uv[h], bk[h], TN)) * g_last[:, sls[h]]
    for h, sl in enumerate(sls):
        mu = jnp.mean(y[h], axis=-1, keepdims=True)
        var = jnp.mean(jnp.square(y[h] - mu), axis=-1, keepdims=True)
        yn = (y[h] - mu) * lax.rsqrt(var + GN_EPS) * lnw_ref[:, sl] + lnb_ref[:, sl]
        bonus = jnp.sum(r[:, sl] * k[:, sl] * rk_ref[:, sl], axis=-1, keepdims=True) * v_h[h]
        o_ref[:, sl] = (yn + bonus) * g_ref[:, sl]


def _rwkv_scan(streams, p, chunk):
    bsz, t, _ = streams[0].shape
    row = lambda a: a.reshape(1, B_WIDTH)
    tril = (jnp.arange(chunk)[None, :] <= jnp.arange(chunk)[:, None]).astype(BF16)
    const = lambda b, c: (0, 0)
    blk = pl.BlockSpec((None, chunk, B_WIDTH), lambda b, c: (b, c, 0))
    st_shape = (bsz, RWKV_HEADS, RWKV_HEAD_DIM, RWKV_HEAD_DIM)
    st = pl.BlockSpec((None,) + st_shape[1:], lambda b, c: (b, 0, 0, 0))
    vec = pl.BlockSpec((1, B_WIDTH), const)
    return pl.pallas_call(
        _rwkv_scan_kernel,
        grid=(bsz, t // chunk),
        in_specs=[blk] * 7 + [vec, vec, vec, pl.BlockSpec((chunk, chunk), const)],
        out_specs=[blk, st],
        out_shape=[jax.ShapeDtypeStruct((bsz, t, B_WIDTH), F32), jax.ShapeDtypeStruct(st_shape, F32)],
        compiler_params=_cparams("parallel", "arbitrary"),
        name="rwkv_scan",
    )(*streams, row(p["r_k"]), row(p["ln_x_w"]), row(p["ln_x_b"]), tril)


def _rwkv_step_kernel(r_ref, lw_ref, k_ref, v_ref, a_ref, b_ref, g_ref, rk_ref, lnw_ref, lnb_ref, s_ref,
                      o_ref, so_ref, y_ref):
    w = jnp.exp(lw_ref[...])
    a, b, k, r = a_ref[...], b_ref[...], k_ref[...], r_ref[...]

    def value_row(i, carry):
        s = s_ref[i]
        sa = jnp.sum(s * a, axis=0, keepdims=True)
        s_new = s * w + sa * b + v_ref[pl.ds(i, 1), :] * k
        so_ref[i] = s_new
        y_ref[pl.ds(i, 1), :] = jnp.sum(s_new * r, axis=0, keepdims=True)
        return carry

    lax.fori_loop(0, RWKV_HEAD_DIM, value_row, 0, unroll=4)
    y = y_ref[...]
    mu = jnp.mean(y, axis=0, keepdims=True)
    var = jnp.mean(jnp.square(y - mu), axis=0, keepdims=True)
    yn = (y - mu) * lax.rsqrt(var + GN_EPS) * lnw_ref[...] + lnb_ref[...]
    bonus = jnp.sum(r_ref[...] * k_ref[...] * rk_ref[...], axis=0, keepdims=True) * v_ref[...]
    o_ref[...] = (yn + bonus) * g_ref[...]


def _rwkv_step(streams_t, s_t, p):
    bd = s_t.shape[-1]
    n = RWKV_HEAD_DIM
    col = lambda a: a.reshape(B_WIDTH, 1)
    vec = pl.BlockSpec((n, bd), lambda h: (h, 0))
    par = pl.BlockSpec((n, 1), lambda h: (h, 0))
    st = pl.BlockSpec((None, n, n, bd), lambda h: (h, 0, 0, 0))
    return pl.pallas_call(
        _rwkv_step_kernel,
        grid=(RWKV_HEADS,),
        in_specs=[vec] * 7 + [par] * 3 + [st],
        out_specs=[vec, st],
        out_shape=[jax.ShapeDtypeStruct((B_WIDTH, bd), F32), jax.ShapeDtypeStruct(s_t.shape, F32)],
        scratch_shapes=[pltpu.VMEM((n, bd), F32)],
        compiler_params=_cparams("parallel"),
        name="rwkv_step",
    )(*streams_t, col(p["r_k"]), col(p["ln_x_w"]), col(p["ln_x_b"]), s_t)


def _finish_kernel(x_ref, *refs, n_stats):
    stat_refs = refs[:n_stats]
    ob_ref, g_ref, wpa_ref, wpb_ref, wo_ref, g2_ref, rw_ref, rb_ref, h_ref, hn_ref, lg_ref = refs[n_stats:]
    if n_stats == 1:
        o_a = stat_refs[0][...]
    else:
        nums, dens, maxes = stat_refs[0::3], stat_refs[1::3], stat_refs[2::3]
        mx = functools.reduce(jnp.maximum, [m[...] for m in maxes])
        es = [jnp.exp(m[...] - mx) for m in maxes]
        num = sum(n_[...] * e for n_, e in zip(nums, es))
        den = sum(d_[...] * e for d_, e in zip(dens, es))
        o_a = num / den
    br_a = _dot1(o_a, wpa_ref[...])
    br_b = _dot1(ob_ref[...], wpb_ref[...])
    mix = g_ref[:, :D_MODEL] * br_a + g_ref[:, D_MODEL:] * br_b
    h = x_ref[...] + _dot1(mix, wo_ref[...])
    h_ref[...] = h
    hn = h * lax.rsqrt(jnp.mean(h * h, axis=-1, keepdims=True) + RMS_EPS) * g2_ref[...]
    hn_ref[...] = hn
    lg_ref[...] = _dot3(hn, rw_ref[...]) + rb_ref[...]


def _finish(x2, stats, o_b, gates, wb, tm):
    n = x2.shape[0]
    const = lambda i: (0, 0)
    rows = lambda w: pl.BlockSpec((tm, w), lambda i: (i, 0))
    params = [wb["w_pa"], wb["w_pb"], wb["w_o"], wb["norm2_g"], wb["router_w"], wb["router_b"]]
    return pl.pallas_call(
        functools.partial(_finish_kernel, n_stats=len(stats)),
        grid=(n // tm,),
        in_specs=[rows(D_MODEL)] + [rows(A_WIDTH)] * len(stats) + [rows(B_WIDTH), rows(2 * D_MODEL)]
                 + [pl.BlockSpec(a.shape, const) for a in params],
        out_specs=[rows(D_MODEL), rows(D_MODEL), rows(LOGIT_PAD)],
        out_shape=[jax.ShapeDtypeStruct((n, D_MODEL), F32), jax.ShapeDtypeStruct((n, D_MODEL), F32),
                   jax.ShapeDtypeStruct((n, LOGIT_PAD), F32)],
        compiler_params=_cparams("parallel"),
        name="finish",
    )(x2, *stats, o_b, gates, *params)


def _row(ref, i):
    return ref.at[pl.ds(i, 1)]


def _dispatch_kernel(pad_ref, pos_ref, hn_ref, xs_hbm, zrow, sem, zsem):
    tm = hn_ref.shape[0]
    n_pad = pad_ref.shape[0]

    @pl.when(pl.program_id(0) == 0)
    def _():
        zrow[...] = jnp.zeros_like(zrow)

        def batch(j, c):
            def start(i, c2):
                pltpu.make_async_copy(zrow, _row(xs_hbm, pad_ref[j * MOE_BLOCK + i]), zsem).start()
                return c2
            lax.fori_loop(0, MOE_BLOCK, start, 0)

            def wait(i, c2):
                pltpu.make_async_copy(zrow, _row(xs_hbm, 0), zsem).wait()
                return c2
            lax.fori_loop(0, MOE_BLOCK, wait, 0)
            return c
        lax.fori_loop(0, n_pad // MOE_BLOCK, batch, 0)

    for kk in range(TOP_K * tm):
        pltpu.make_async_copy(_row(hn_ref, kk // TOP_K), _row(xs_hbm, pos_ref[0, 0, kk]), sem).start(priority=kk % 2)
    for kk in range(TOP_K * tm):
        pltpu.make_async_copy(_row(hn_ref, 0), _row(xs_hbm, 0), sem).wait()


def _dispatch(hn, pos, pad_rows, n_rows):
    n = hn.shape[0]
    tm = min(DISPATCH_TOKENS, n)
    steps = n // tm
    grid_spec = pltpu.PrefetchScalarGridSpec(
        num_scalar_prefetch=1,
        grid=(steps,),
        in_specs=[
            pl.BlockSpec((1, 1, TOP_K * tm), lambda i, pad: (i, 0, 0), memory_space=pltpu.SMEM),
            pl.BlockSpec((tm, D_MODEL), lambda i, pad: (i, 0)),
        ],
        out_specs=pl.BlockSpec(memory_space=pl.ANY),
        scratch_shapes=[pltpu.VMEM((1, D_MODEL), F32), pltpu.SemaphoreType.DMA(()), pltpu.SemaphoreType.DMA(())],
    )
    return pl.pallas_call(
        _dispatch_kernel,
        grid_spec=grid_spec,
        out_shape=jax.ShapeDtypeStruct((n_rows, D_MODEL), F32),
        compiler_params=_cparams("arbitrary"),
        name="moe_dispatch",
    )(pad_rows, pos.reshape(steps, 1, TOP_K * tm), hn)


def _moe_kernel(blk_e_ref, nused_ref, xs_ref, wgu_ref, bgu_ref, wd_ref, bdn_ref, ys_ref):
    i = pl.program_id(0)

    @pl.when(i < nused_ref[0])
    def _():
        x = xs_ref[...].astype(BF16)
        acc = jnp.zeros((MOE_BLOCK, D_MODEL), F32)
        for c in range(D_FF // COL_CHUNK):
            lo = c * COL_CHUNK
            gate = _dg(x, wgu_ref[:, lo:lo + COL_CHUNK], NN) + bgu_ref[:, lo:lo + COL_CHUNK]
            up = _dg(x, wgu_ref[:, D_FF + lo:D_FF + lo + COL_CHUNK], NN) + bgu_ref[:, D_FF + lo:D_FF + lo + COL_CHUNK]
            gate = jnp.minimum(gate, SWIGLU_LIMIT)
            up = jnp.clip(up, -SWIGLU_LIMIT, SWIGLU_LIMIT)
            hid = (up + 1.0) * gate * _sigmoid(gate * SWIGLU_ALPHA)
            acc = acc + _dg(hid.astype(BF16), wd_ref[lo:lo + COL_CHUNK, :], NN)
        ys_ref[...] = acc + bdn_ref[...]

    @pl.when(i >= nused_ref[0])
    def _():
        ys_ref[...] = jnp.zeros_like(ys_ref)


def _moe_blocks(xs, blk_e, nused, wb):
    n_blocks = xs.shape[0] // MOE_BLOCK
    grid_spec = pltpu.PrefetchScalarGridSpec(
        num_scalar_prefetch=2,
        grid=(n_blocks,),
        in_specs=[
            pl.BlockSpec((MOE_BLOCK, D_MODEL), lambda i, be, nu: (jnp.minimum(i, jnp.maximum(nu[0] - 1, 0)), 0)),
            pl.BlockSpec((None, D_MODEL, 2 * D_FF), lambda i, be, nu: (be[i], 0, 0)),
            pl.BlockSpec((None, 1, 2 * D_FF), lambda i, be, nu: (be[i], 0, 0)),
            pl.BlockSpec((None, D_FF, D_MODEL), lambda i, be, nu: (be[i], 0, 0)),
            pl.BlockSpec((None, 1, D_MODEL), lambda i, be, nu: (be[i], 0, 0)),
        ],
        out_specs=pl.BlockSpec((MOE_BLOCK, D_MODEL), lambda i, be, nu: (i, 0)),
    )
    return pl.pallas_call(
        _moe_kernel,
        grid_spec=grid_spec,
        out_shape=jax.ShapeDtypeStruct(xs.shape, F32),
        compiler_params=_cparams("arbitrary"),
        name="moe_blocks",
    )(blk_e, nused, xs, wb["w_gu"], wb["b_gu"], wb["w_down"], wb["b_down"])


def _combine_kernel(pos_ref, pos_next_ref, h_ref, gate_ref, ys_hbm, y_ref, gbuf, sem):
    tm = h_ref.shape[0]
    rows = TOP_K * tm
    i = pl.program_id(0)
    n = pl.num_programs(0)
    slot = i % 2

    def issue(idx_ref, s):
        for kk in range(rows):
            pltpu.make_async_copy(_row(ys_hbm, idx_ref[0, 0, kk]), gbuf.at[s, pl.ds(kk, 1)], sem.at[s]).start(
                priority=kk % 2)

    @pl.when(i == 0)
    def _():
        issue(pos_ref, 0)

    @pl.when(i + 1 < n)
    def _():
        issue(pos_next_ref, 1 - slot)

    for kk in range(rows):
        pltpu.make_async_copy(_row(ys_hbm, 0), gbuf.at[slot, pl.ds(0, 1)], sem.at[slot]).wait()
    gv = gate_ref[...]
    ff = gv[:, 0:1] * gbuf[slot, 0:tm, :]
    for j in range(1, TOP_K):
        ff = ff + gv[:, j:j + 1] * gbuf[slot, j * tm:(j + 1) * tm, :]
    y_ref[...] = h_ref[...] + ff


def _combine(h, gates_pad, pos, ys):
    n = h.shape[0]
    tm = min(COMBINE_TOKENS, n)
    steps = n // tm
    rows = TOP_K * tm
    pos3 = pos.reshape(steps, tm, TOP_K).transpose(0, 2, 1).reshape(steps, 1, rows)
    return pl.pallas_call(
        _combine_kernel,
        grid=(steps,),
        in_specs=[
            pl.BlockSpec((1, 1, rows), lambda i: (i, 0, 0), memory_space=pltpu.SMEM),
            pl.BlockSpec((1, 1, rows), lambda i: (jnp.minimum(i + 1, steps - 1), 0, 0), memory_space=pltpu.SMEM),
            pl.BlockSpec((tm, D_MODEL), lambda i: (i, 0)),
            pl.BlockSpec((tm, LOGIT_PAD), lambda i: (i, 0)),
            pl.BlockSpec(memory_space=pl.ANY),
        ],
        out_specs=pl.BlockSpec((tm, D_MODEL), lambda i: (i, 0)),
        out_shape=jax.ShapeDtypeStruct((n, D_MODEL), F32),
        scratch_shapes=[pltpu.VMEM((2, rows, D_MODEL), F32), pltpu.SemaphoreType.DMA((2,))],
        compiler_params=_cparams("arbitrary"),
        name="moe_combine",
    )(pos3, pos3, h, gates_pad, ys)


def _route(logits):
    n = logits.shape[0]
    nk = n * TOP_K
    top_v, top_e = lax.top_k(logits, TOP_K)
    gates = jax.nn.softmax(top_v, axis=-1)
    onehot = (top_e.reshape(nk, 1) == jnp.arange(N_EXPERTS, dtype=top_e.dtype)[None, :]).astype(jnp.int32)
    csum = jnp.cumsum(onehot, axis=0)
    counts = csum[-1]
    pcounts = (counts + MOE_BLOCK - 1) // MOE_BLOCK * MOE_BLOCK
    pends = jnp.cumsum(pcounts)
    pstarts = pends - pcounts
    pos = jnp.sum(onehot * (csum - 1 + pstarts[None, :]), axis=1).astype(jnp.int32)
    n_blocks = -(-nk // MOE_BLOCK) + N_EXPERTS
    blk_start = jnp.arange(n_blocks, dtype=jnp.int32) * MOE_BLOCK
    blk_e = jnp.minimum(jnp.sum(blk_start[:, None] >= pends[None, :], axis=1), N_EXPERTS - 1).astype(jnp.int32)
    nused = (pends[-1:] // MOE_BLOCK).astype(jnp.int32)
    n_pad = n_blocks * MOE_BLOCK - nk
    pad_cnt = pcounts - counts
    pad_end = jnp.cumsum(pad_cnt)
    m = jnp.arange(n_pad, dtype=jnp.int32)
    e_of = jnp.sum(m[:, None] >= pad_end[None, :], axis=1)
    e_c = jnp.minimum(e_of, N_EXPERTS - 1)
    in_expert = (pstarts + counts - (pad_end - pad_cnt))[e_c] + m
    pad_rows = jnp.where(e_of < N_EXPERTS, in_expert, pends[-1] + m - pad_end[-1]).astype(jnp.int32)
    gates_pad = jnp.pad(gates, ((0, 0), (0, LOGIT_PAD - TOP_K)))
    return pos, pad_rows, blk_e, nused, gates_pad, n_blocks * MOE_BLOCK


def _moe(h, hn, logits_pad, wb):
    pos, pad_rows, blk_e, nused, gates_pad, n_rows = _route(logits_pad[:, :N_EXPERTS])
    xs = _dispatch(hn, pos, pad_rows, n_rows)
    ys = _moe_blocks(xs, blk_e, nused, wb)
    return _combine(h, gates_pad, pos, ys)


def kernel(x_prompt, x_sample, cache_kv_w128, cache_kv_w512, cache_kv_w2048, state_wkv, state_shift,
           norm1_g, w_in, q_norm_g, k_norm_g, mu_shift, w0, w2, a0, a2, g2, k_k, k_a, r_k,
           ln_x_w, ln_x_b, w_pa, w_pb, w_o, norm2_g, router_w, router_b, w_gu, b_gu, w_down, b_down):
    bp, seq, _ = x_prompt.shape
    bd, t_s, _ = x_sample.shape
    assert t_s == 1
    rw = dict(mu_shift=mu_shift, w0=w0, w2=w2, a0=a0, a2=a2, g2=g2, k_k=k_k, k_a=k_a, r_k=r_k,
              ln_x_w=ln_x_w, ln_x_b=ln_x_b)
    wb = dict(
        w_pa=w_pa.astype(BF16), w_pb=w_pb.astype(BF16), w_o=w_o.astype(BF16),
        norm2_g=norm2_g.reshape(1, D_MODEL),
        router_w=jnp.pad(router_w, ((0, 0), (0, LOGIT_PAD - N_EXPERTS))),
        router_b=jnp.pad(router_b, (0, LOGIT_PAD - N_EXPERTS)).reshape(1, LOGIT_PAD),
        w_gu=w_gu.astype(BF16), b_gu=b_gu.reshape(N_EXPERTS, 1, 2 * D_FF),
        w_down=w_down.astype(BF16), b_down=b_down.reshape(N_EXPERTS, 1, D_MODEL),
    )
    w_in_b = w_in.astype(BF16)
    mult = jnp.stack([jnp.tile(q_norm_g[c // 3] * (HEAD_DIM ** -0.5) if c % 3 == 0 else
                               (k_norm_g[c // 3] if c % 3 == 1 else jnp.ones((HEAD_DIM,), F32)),
                               HEADS_PER_GROUP) for c in range(A_COLS // COL_CHUNK)])
    bd_head = _seg_ones(COL_CHUNK, HEAD_DIM, 1.0 / HEAD_DIM)

    xp2 = x_prompt.reshape(bp * seq, D_MODEL)
    pa, ps, pg = _in_proj(xp2, norm1_g, w_in_b, mult, bd_head, tm=256)
    pa3 = pa.reshape(bp, seq, A_COLS)
    stats = []
    kv_p = []
    for g in range(N_GROUPS):
        stats += _attn_prompt(pa3, g)
        keep = min(GROUP_WINDOWS[g], seq)
        lo = g * 3 * A_WIDTH + A_WIDTH
        kv_p.append(pa3[:, seq - keep:, lo:lo + 2 * A_WIDTH].reshape(bp, keep, 2, HEADS_PER_GROUP, HEAD_DIM))
    ps3 = ps.reshape(bp, seq, SHIFT_WIDTH)
    streams = _rwkv_prep(ps3, jnp.zeros((bp, 1, SHIFT_WIDTH), F32), rw, tm=512)
    o_b, wkv_p = _rwkv_scan(streams, rw, chunk=SCAN_CHUNK)
    h, hn, lg = _finish(xp2, stats, o_b.reshape(bp * seq, B_WIDTH), pg, wb, tm=256)
    y_prompt = _moe(h, hn, lg, wb).reshape(bp, seq, D_MODEL)
    shift_p = ps3[:, -1:]

    caches = (cache_kv_w128, cache_kv_w512, cache_kv_w2048)
    xs2 = x_sample.reshape(bd, D_MODEL)
    pa_s, ps_s, pg_s = _in_proj(xs2, norm1_g, w_in_b, mult, bd_head, tm=bd)
    caches_t = [jnp.transpose(c.astype(F32), (0, 2, 3, 4, 1)) for c in caches]
    *kv_t, oa_t = _sample_cache(pa_s.T, caches_t)
    kv_s = [jnp.transpose(t, (0, 4, 1, 2, 3)).astype(c.dtype) for t, c in zip(kv_t, caches)]
    streams_t = _rwkv_prep_t(ps_s, state_shift.reshape(bd, SHIFT_WIDTH).astype(F32), rw)
    ob_t, s_t = _rwkv_step(streams_t, jnp.transpose(state_wkv.astype(F32), (1, 2, 3, 0)), rw)
    wkv_s = jnp.transpose(s_t, (3, 0, 1, 2))
    h_s, hn_s, lg_s = _finish(xs2, [oa_t.T], ob_t.T, pg_s, wb, tm=bd)
    y_sample = _moe(h_s, hn_s, lg_s, wb).reshape(bd, 1, D_MODEL)
    shift_s = ps_s.reshape(bd, 1, SHIFT_WIDTH)

    return (y_prompt, y_sample, kv_p[0], kv_p[1], kv_p[2], wkv_p.astype(state_wkv.dtype), shift_p,
            kv_s[0], kv_s[1], kv_s[2], wkv_s.astype(state_wkv.dtype), shift_s.astype(state_shift.dtype))
```

```python
import functools

import jax
import jax.numpy as jnp
from jax import lax
from jax.experimental import pallas as pl
from jax.experimental.pallas import tpu as pltpu

F32 = jnp.float32
BF16 = jnp.bfloat16

D_MODEL = 1024
GROUP_WINDOWS = (128, 512, 2048)
GROUP_DILATIONS = (1, 4, 16)
N_GROUPS = 3
HEADS_PER_GROUP = 4
HEAD_DIM = 64
A_WIDTH = HEADS_PER_GROUP * HEAD_DIM
A_COLS = N_GROUPS * 3 * A_WIDTH
RWKV_HEADS = 8
RWKV_HEAD_DIM = 64
B_WIDTH = RWKV_HEADS * RWKV_HEAD_DIM
DECAY_LORA = 64
AAA_LORA = 64
GATE_LORA = 128
SHIFT_WIDTH = 3 * B_WIDTH + DECAY_LORA + AAA_LORA + GATE_LORA
D_IN = A_COLS + SHIFT_WIDTH + 2 * D_MODEL
N_EXPERTS = 32
TOP_K = 4
D_FF = D_MODEL
SWIGLU_LIMIT = 7.0
SWIGLU_ALPHA = 1.702
MOE_BLOCK = 256
RMS_EPS = 1e-6
GN_EPS = 64e-5

N_BACK = 128
COL_CHUNK = 256
SCAN_CHUNK = 64
LANE_TILE = 128
LOGIT_PAD = LANE_TILE
COMBINE_TOKENS = 128
VMEM_LIMIT = 48 * 1024 * 1024


def _cparams(*sem):
    return pltpu.CompilerParams(dimension_semantics=sem, vmem_limit_bytes=VMEM_LIMIT)


def _sigmoid(x):
    return 1.0 / (1.0 + jnp.exp(-x))


def _split(x):
    hi = x.astype(BF16)
    lo = (x - hi.astype(F32)).astype(BF16)
    return hi, lo


def _dg(a, b, dims):
    return lax.dot_general(a, b, (dims, ((), ())), preferred_element_type=F32)


NN = ((1,), (0,))
NT = ((1,), (1,))
TN = ((0,), (0,))


def _dot1(a, b, dims=NN):
    return _dg(a.astype(BF16), b.astype(BF16), dims)


def _dot3(a, b, dims=NN):
    ah, al = _split(a)
    bh, bl = _split(b)
    return _dg(ah, bh, dims) + _dg(al, bh, dims) + _dg(ah, bl, dims)


def _split3(x):
    hi = x.astype(BF16)
    r1 = x - hi.astype(F32)
    mid = r1.astype(BF16)
    lo = (r1 - mid.astype(F32)).astype(BF16)
    return hi, mid, lo


def _dot_exact_rhs(a, b_bf16):
    return sum(_dg(t, b_bf16, NN) for t in _split3(a))


def _dot_exact_lhs(l_bf16, x):
    return sum(_dg(l_bf16, t, NN) for t in _split3(x))


def _seg_ones(width, seg, scale):
    i = jnp.arange(width)[:, None] // seg
    j = jnp.arange(width)[None, :] // seg
    return jnp.where(i == j, scale, 0.0).astype(BF16)


def _in_proj_kernel(x_ref, g1_ref, w_ref, mult_ref, bd_ref, pa_ref, ps_ref, pg_ref):
    x = x_ref[...]
    xn = x * lax.rsqrt(jnp.mean(x * x, axis=-1, keepdims=True) + RMS_EPS) * g1_ref[...]
    xb = xn.astype(BF16)
    bd = bd_ref[...]
    for c in range(A_COLS // COL_CHUNK):
        acc = _dg(xb, w_ref[:, c * COL_CHUNK:(c + 1) * COL_CHUNK], NN)
        if c % 3 != 2:
            ms = _dot_exact_rhs(acc * acc, bd)
            acc = acc * lax.rsqrt(ms + RMS_EPS) * mult_ref[c:c + 1, :]
        pa_ref[:, c * COL_CHUNK:(c + 1) * COL_CHUNK] = acc
    for c in range(SHIFT_WIDTH // COL_CHUNK):
        lo = A_COLS + c * COL_CHUNK
        ps_ref[:, c * COL_CHUNK:(c + 1) * COL_CHUNK] = _dg(xb, w_ref[:, lo:lo + COL_CHUNK], NN)
    for c in range(2 * D_MODEL // COL_CHUNK):
        lo = A_COLS + SHIFT_WIDTH + c * COL_CHUNK
        pg_ref[:, c * COL_CHUNK:(c + 1) * COL_CHUNK] = _sigmoid(_dg(xb, w_ref[:, lo:lo + COL_CHUNK], NN))


def _in_proj(x2, norm1_g, w_in_b, mult, bd, tm):
    n = x2.shape[0]
    const = lambda i: (0, 0)
    return pl.pallas_call(
        _in_proj_kernel,
        grid=(n // tm,),
        in_specs=[
            pl.BlockSpec((tm, D_MODEL), lambda i: (i, 0)),
            pl.BlockSpec((1, D_MODEL), const),
            pl.BlockSpec((D_MODEL, D_IN), const, pipeline_mode=pl.Buffered(1)),
            pl.BlockSpec(mult.shape, const),
            pl.BlockSpec(bd.shape, const),
        ],
        out_specs=[
            pl.BlockSpec((tm, A_COLS), lambda i: (i, 0)),
            pl.BlockSpec((tm, SHIFT_WIDTH), lambda i: (i, 0)),
            pl.BlockSpec((tm, 2 * D_MODEL), lambda i: (i, 0)),
        ],
        out_shape=[
            jax.ShapeDtypeStruct((n, A_COLS), F32),
            jax.ShapeDtypeStruct((n, SHIFT_WIDTH), F32),
            jax.ShapeDtypeStruct((n, 2 * D_MODEL), F32),
        ],
        compiler_params=_cparams("parallel"),
        name="in_proj",
    )(x2, norm1_g.reshape(1, D_MODEL), w_in_b, mult, bd)


def _attn_kernel(*refs, dil):
    halves = A_WIDTH // LANE_TILE
    q_refs, kp_refs, kc_refs, vp_refs, vc_refs = (refs[i * halves:(i + 1) * halves] for i in range(5))
    out_refs = refs[5 * halves:5 * halves + 3]
    qs, ks, vs = refs[5 * halves + 3:5 * halves + 6]
    stat_s = refs[5 * halves + 6:5 * halves + 9]
    stage = refs[5 * halves + 9:]
    n = pl.program_id(1)
    qi = lax.broadcasted_iota(jnp.int32, (N_BACK, 2 * N_BACK), 0)
    kj = lax.broadcasted_iota(jnp.int32, (N_BACK, 2 * N_BACK), 1)
    first_prev = jnp.where(n > 0, 0, N_BACK)
    lo = jnp.where(kj < N_BACK, qi + first_prev, N_BACK)
    hi = jnp.where(kj < N_BACK, N_BACK - 1, qi + N_BACK)
    mask = jnp.logical_and(kj >= lo, kj <= hi)

    def stream(r, carry):
        rows = pl.ds(r, N_BACK, stride=dil) if dil > 1 else slice(None)
        for c in range(halves):
            lanes = slice(c * LANE_TILE, (c + 1) * LANE_TILE)
            qs[:, lanes] = q_refs[c][rows, :]
            ks[0:N_BACK, lanes] = kp_refs[c][rows, :]
            ks[N_BACK:, lanes] = kc_refs[c][rows, :]
            vs[0:N_BACK, lanes] = vp_refs[c][rows, :]
            vs[N_BACK:, lanes] = vc_refs[c][rows, :]
        sls = [slice(h * HEAD_DIM, (h + 1) * HEAD_DIM) for h in range(HEADS_PER_GROUP)]
        s = [jnp.where(mask, _dot1(qs[:, sl], ks[:, sl], NT), -jnp.inf) for sl in sls]
        m = [jnp.max(t, axis=-1, keepdims=True) for t in s]
        p = [jnp.exp(t - mm) for t, mm in zip(s, m)]
        den = [jnp.sum(t, axis=-1, keepdims=True) for t in p]
        num = [_dot1(t, vs[:, sl]) for t, sl in zip(p, sls)]
        for h, sl in enumerate(sls):
            stat_s[0][:, sl] = num[h]
            stat_s[1][:, sl] = jnp.broadcast_to(den[h], (N_BACK, HEAD_DIM))
            stat_s[2][:, sl] = jnp.broadcast_to(m[h], (N_BACK, HEAD_DIM))
        for i in range(3):
            if dil > 1:
                for c in range(halves):
                    stage[i * halves + c][rows, :] = stat_s[i][:, c * LANE_TILE:(c + 1) * LANE_TILE]
            else:
                out_refs[i][...] = stat_s[i][...]
        return carry

    if dil > 1:
        lax.fori_loop(0, dil, stream, 0)
        for i in range(3):
            for c in range(halves):
                out_refs[i][:, c * LANE_TILE:(c + 1) * LANE_TILE] = stage[i * halves + c][...]
    else:
        stream(0, 0)


def _attn_prompt(pa3, g):
    bsz, seq, _ = pa3.shape
    dil = GROUP_DILATIONS[g]
    rows = N_BACK * dil
    nb = seq // rows
    halves = A_WIDTH // LANE_TILE

    def specs(slab, prev):
        def one(c):
            col = (3 * g + slab) * halves + c
            if prev:
                return pl.BlockSpec((None, rows, LANE_TILE), lambda b, n: (b, jnp.maximum(n - 1, 0), col))
            return pl.BlockSpec((None, rows, LANE_TILE), lambda b, n: (b, n, col))
        return [one(c) for c in range(halves)]

    in_specs = specs(0, False) + specs(1, True) + specs(1, False) + specs(2, True) + specs(2, False)
    out_spec = pl.BlockSpec((None, rows, A_WIDTH), lambda b, n: (b, n, 0))
    out_sds = jax.ShapeDtypeStruct((bsz, seq, A_WIDTH), F32)
    scratch = [pltpu.VMEM((N_BACK, A_WIDTH), F32), pltpu.VMEM((2 * N_BACK, A_WIDTH), F32),
               pltpu.VMEM((2 * N_BACK, A_WIDTH), F32)] + [pltpu.VMEM((N_BACK, A_WIDTH), F32)] * 3
    if dil > 1:
        scratch += [pltpu.VMEM((rows, LANE_TILE), F32)] * (3 * halves)
    outs = pl.pallas_call(
        functools.partial(_attn_kernel, dil=dil),
        grid=(bsz, nb),
        in_specs=in_specs,
        out_specs=[out_spec] * 3,
        out_shape=[out_sds] * 3,
        scratch_shapes=scratch,
        compiler_params=_cparams("parallel", "arbitrary"),
        name=f"attn_prompt_g{g}",
    )(*([pa3] * len(in_specs)))
    return [o.reshape(bsz * seq, A_WIDTH) for o in outs]


def _sample_cache_kernel(pat_ref, c0_ref, c1_ref, c2_ref, o0_ref, o1_ref, o2_ref, oa_ref):
    b = pl.program_id(0)
    n_b = pat_ref.shape[1]

    @pl.when(b == 0)
    def _():
        oa_ref[...] = jnp.zeros_like(oa_ref)

    lane_b = lax.broadcasted_iota(jnp.int32, pat_ref.shape, 1) == b
    col = jnp.sum(jnp.where(lane_b, pat_ref[...], 0.0), axis=1, keepdims=True)
    c_refs = (c0_ref, c1_ref, c2_ref)
    o_refs = (o0_ref, o1_ref, o2_ref)
    stats = []
    for g in range(N_GROUPS):
        dil = GROUP_DILATIONS[g]
        win = GROUP_WINDOWS[g]
        base = g * 3 * A_WIDTH
        lane = lax.broadcasted_iota(jnp.int32, (HEAD_DIM, win), 1)
        last = lane == win - 1
        live = lax.broadcasted_iota(jnp.int32, (1, win), 1) % dil == 0
        for h in range(HEADS_PER_GROUP):
            o = h * HEAD_DIM
            q = col[base + o:base + o + HEAD_DIM]
            kn = col[base + A_WIDTH + o:base + A_WIDTH + o + HEAD_DIM]
            vn = col[base + 2 * A_WIDTH + o:base + 2 * A_WIDTH + o + HEAD_DIM]
            kt = c_refs[g][0, 0, h]
            vt = c_refs[g][0, 1, h]
            o_refs[g][0, 0, h] = jnp.where(last, kn, pltpu.roll(kt, win - 1, axis=1))
            o_refs[g][0, 1, h] = jnp.where(last, vn, pltpu.roll(vt, win - 1, axis=1))
            s_c = jnp.where(live, jnp.sum(kt * q, axis=0, keepdims=True), -jnp.inf)
            s_n = jnp.sum(kn * q, axis=0, keepdims=True)
            m = jnp.maximum(jnp.max(s_c, axis=1, keepdims=True), s_n)
            p_c = jnp.exp(s_c - m)
            p_n = jnp.exp(s_n - m)
            den = jnp.sum(p_c, axis=1, keepdims=True) + p_n
            num = jnp.sum(vt * p_c, axis=1, keepdims=True) + p_n * vn
            stats.append((num, den, m))
    outs = []
    for h in range(HEADS_PER_GROUP):
        per_g = [stats[g * HEADS_PER_GROUP + h] for g in range(N_GROUPS)]
        mx = functools.reduce(jnp.maximum, [m for _, _, m in per_g])
        num = sum(n_ * jnp.exp(m - mx) for n_, _, m in per_g)
        den = sum(d_ * jnp.exp(m - mx) for _, d_, m in per_g)
        outs.append(num / den)
    o_col = jnp.concatenate(outs, axis=0)
    lane_o = lax.broadcasted_iota(jnp.int32, (A_WIDTH, n_b), 1) == b
    oa_ref[...] = jnp.where(lane_o, o_col, oa_ref[...])


def _sample_cache(pa_t, caches_t):
    bd = pa_t.shape[1]
    cspecs = [pl.BlockSpec((1, 2, HEADS_PER_GROUP, HEAD_DIM, w), lambda b: (b, 0, 0, 0, 0)) for w in GROUP_WINDOWS]
    return pl.pallas_call(
        _sample_cache_kernel,
        grid=(bd,),
        in_specs=[pl.BlockSpec(pa_t.shape, lambda b: (0, 0))] + cspecs,
        out_specs=cspecs + [pl.BlockSpec((A_WIDTH, bd), lambda b: (0, 0))],
        out_shape=[jax.ShapeDtypeStruct(c.shape, c.dtype) for c in caches_t]
                  + [jax.ShapeDtypeStruct((A_WIDTH, bd), F32)],
        compiler_params=_cparams("arbitrary"),
        name="sample_cache",
    )(pa_t, *caches_t)


def _rwkv_prep_math(pf, shifted, mu_ref, w0_ref, w2_ref, a0_ref, a2_ref, g2_ref, kk_ref, ka_ref, bd_ref):
    z = pf + mu_ref[...] * (shifted - pf)
    c1, c2, c3 = B_WIDTH, 2 * B_WIDTH, 3 * B_WIDTH
    r, k, v = z[:, :c1], z[:, c1:c2], z[:, c2:c3]
    wl = z[:, c3:c3 + DECAY_LORA]
    al = z[:, c3 + DECAY_LORA:c3 + DECAY_LORA + AAA_LORA]
    gl = z[:, c3 + DECAY_LORA + AAA_LORA:]
    xw = w0_ref[...] + _dot3(jnp.tanh(wl), w2_ref[...])
    w_log = -(jnp.maximum(-xw, 0.0) + jnp.log(1.0 + jnp.exp(-jnp.abs(xw)))) - 0.5
    lw = -jnp.exp(w_log)
    a = _sigmoid(a0_ref[...] + _dot3(al, a2_ref[...]))
    g = _dot3(_sigmoid(gl), g2_ref[...])
    kk = k * kk_ref[...]
    bd = bd_ref[...]
    kks = []
    for c in range(B_WIDTH // 128):
        kc = kk[:, c * 128:(c + 1) * 128]
        nrm = jnp.sqrt(_dot_exact_rhs(kc * kc, bd))
        kks.append(kc / jnp.maximum(nrm, 1e-12))
    kkn = jnp.concatenate(kks, axis=1)
    return r, lw, k * (1.0 + (a - 1.0) * ka_ref[...]), v, -kkn, kkn * a, g


def _rwkv_prep_kernel(ps_ref, pv_ref, p0_ref, *refs):
    param_refs, out_refs = refs[:9], refs[9:]
    i = pl.program_id(1)
    pf = ps_ref[...]
    prev_row = jnp.where(i == 0, p0_ref[...], pv_ref[7:8, :])
    row = lax.broadcasted_iota(jnp.int32, pf.shape, 0)
    shifted = jnp.where(row == 0, prev_row, pltpu.roll(pf, 1, axis=0))
    for o_ref, val in zip(out_refs, _rwkv_prep_math(pf, shifted, *param_refs)):
        o_ref[...] = val


def _rwkv_prep_t_kernel(ps_ref, prev_ref, *refs):
    param_refs, out_refs = refs[:9], refs[9:]
    for o_ref, val in zip(out_refs, _rwkv_prep_math(ps_ref[...], prev_ref[...], *param_refs)):
        o_ref[...] = val.T


def _prep_params(p):
    row = lambda a: a.reshape(1, -1)
    return [row(p["mu_shift"]), row(p["w0"]), p["w2"], row(p["a0"]), p["a2"], p["g2"],
            row(p["k_k"]), row(p["k_a"]), _seg_ones(128, RWKV_HEAD_DIM, 1.0)]


def _rwkv_prep(ps3, prev0, p, tm):
    bsz, t, _ = ps3.shape
    const = lambda b, i: (0, 0)
    params = _prep_params(p)
    out_spec = pl.BlockSpec((None, tm, B_WIDTH), lambda b, i: (b, i, 0))
    out_sds = jax.ShapeDtypeStruct((bsz, t, B_WIDTH), F32)
    return pl.pallas_call(
        _rwkv_prep_kernel,
        grid=(bsz, t // tm),
        in_specs=[
            pl.BlockSpec((None, tm, SHIFT_WIDTH), lambda b, i: (b, i, 0)),
            pl.BlockSpec((None, 8, SHIFT_WIDTH), lambda b, i: (b, jnp.maximum(i * (tm // 8) - 1, 0), 0)),
            pl.BlockSpec((None, 1, SHIFT_WIDTH), lambda b, i: (b, 0, 0)),
        ] + [pl.BlockSpec(a.shape, const) for a in params],
        out_specs=[out_spec] * 7,
        out_shape=[out_sds] * 7,
        compiler_params=_cparams("parallel", "arbitrary"),
        name="rwkv_prep",
    )(ps3, ps3, prev0, *params)


def _rwkv_prep_t(ps2, prev2, p):
    bd = ps2.shape[0]
    params = _prep_params(p)
    full = lambda a: pl.BlockSpec(a.shape, lambda i: (0, 0))
    out_sds = jax.ShapeDtypeStruct((B_WIDTH, bd), F32)
    return pl.pallas_call(
        _rwkv_prep_t_kernel,
        grid=(1,),
        in_specs=[full(ps2), full(prev2)] + [full(a) for a in params],
        out_specs=[pl.BlockSpec((B_WIDTH, bd), lambda i: (0, 0))] * 7,
        out_shape=[out_sds] * 7,
        compiler_params=_cparams("arbitrary"),
        name="rwkv_prep_t",
    )(ps2, prev2, *params)


def _rwkv_scan_kernel(r_ref, lw_ref, k_ref, v_ref, a_ref, b_ref, g_ref, rk_ref, lnw_ref, lnb_ref,
                      tril_ref, o_ref, s_ref):
    chunk = r_ref.shape[0]

    @pl.when(pl.program_id(1) == 0)
    def _():
        s_ref[...] = jnp.zeros_like(s_ref)

    lw = lw_ref[...]
    r, k, v = r_ref[...], k_ref[...], v_ref[...]
    cum = _dot_exact_lhs(tril_ref[...], lw)
    e_in = jnp.exp(cum)
    e_inv = jnp.exp(-cum)
    at = a_ref[...] * jnp.exp(cum - lw)
    rt = r * e_in
    bt = b_ref[...] * e_inv
    kt = k * e_inv
    g_last = e_in[chunk - 1:chunk, :]

    ti = lax.broadcasted_iota(jnp.int32, (chunk, 2 * chunk), 0)
    si = lax.broadcasted_iota(jnp.int32, (chunk, 2 * chunk), 1)
    si = jnp.where(si >= chunk, si - chunk, si)
    strict = si < ti
    incl = si <= ti
    eye = jnp.where(lax.broadcasted_iota(jnp.int32, (chunk, chunk), 0)
                    == lax.broadcasted_iota(jnp.int32, (chunk, chunk), 1), 1.0, 0.0).astype(F32)
    n_sq = max(chunk.bit_length() - 2, 0)

    heads = range(RWKV_HEADS)
    sls = [slice(h * RWKV_HEAD_DIM, (h + 1) * RWKV_HEAD_DIM) for h in heads]
    ar = [jnp.concatenate([at[:, sl], rt[:, sl]], axis=0) for sl in sls]
    bk = [jnp.concatenate([bt[:, sl], kt[:, sl]], axis=0) for sl in sls]
    s_old = [s_ref[h] for h in heads]
    m_all = [_dot1(ar[h], bk[h], NT) for h in heads]
    gs = [_dot1(ar[h], s_old[h], NT) for h in heads]
    a_m = [jnp.where(strict, m_all[h][:chunk], 0.0) for h in heads]
    r_m = [jnp.where(incl, m_all[h][chunk:], 0.0) for h in heads]
    pw = [a_m[h][:, :chunk] for h in heads]
    x = [eye + pw[h] for h in heads]
    for _ in range(n_sq):
        pw = [_dot1(pw[h], pw[h]) for h in heads]
        x = [x[h] + _dot1(x[h], pw[h]) for h in heads]
    v_h = [v[:, sl] for sl in sls]
    av = [_dot1(a_m[h][:, chunk:], v_h[h]) for h in heads]
    u = [_dot1(x[h], gs[h][:chunk] + av[h]) for h in heads]
    uv = [jnp.concatenate([u[h], v_h[h]], axis=0) for h in heads]
    y = [gs[h][chunk:] + _dot1(r_m[h], uv[h]) for h in heads]
    for h in heads:
        s_ref[h] = (s_old[h] + _dot3(uv[h], bk[h], TN)) * g_last[:, sls[h]]
    for h, sl in enumerate(sls):
        mu = jnp.mean(y[h], axis=-1, keepdims=True)
        var = jnp.mean(jnp.square(y[h] - mu), axis=-1, keepdims=True)
        yn = (y[h] - mu) * lax.rsqrt(var + GN_EPS) * lnw_ref[:, sl] + lnb_ref[:, sl]
        bonus = jnp.sum(r[:, sl] * k[:, sl] * rk_ref[:, sl], axis=-1, keepdims=True) * v_h[h]
        o_ref[:, sl] = (yn + bonus) * g_ref[:, sl]


def _rwkv_scan(streams, p, chunk):
    bsz, t, _ = streams[0].shape
    row = lambda a: a.reshape(1, B_WIDTH)
    tril = (jnp.arange(chunk)[None, :] <= jnp.arange(chunk)[:, None]).astype(BF16)
    const = lambda b, c: (0, 0)
    blk = pl.BlockSpec((None, chunk, B_WIDTH), lambda b, c: (b, c, 0))
    st_shape = (bsz, RWKV_HEADS, RWKV_HEAD_DIM, RWKV_HEAD_DIM)
    st = pl.BlockSpec((None,) + st_shape[1:], lambda b, c: (b, 0, 0, 0))
    vec = pl.BlockSpec((1, B_WIDTH), const)
    return pl.pallas_call(
        _rwkv_scan_kernel,
        grid=(bsz, t // chunk),
        in_specs=[blk] * 7 + [vec, vec, vec, pl.BlockSpec((chunk, chunk), const)],
        out_specs=[blk, st],
        out_shape=[jax.ShapeDtypeStruct((bsz, t, B_WIDTH), F32), jax.ShapeDtypeStruct(st_shape, F32)],
        compiler_params=_cparams("parallel", "arbitrary"),
        name="rwkv_scan",
    )(*streams, row(p["r_k"]), row(p["ln_x_w"]), row(p["ln_x_b"]), tril)


def _rwkv_step_kernel(r_ref, lw_ref, k_ref, v_ref, a_ref, b_ref, g_ref, rk_ref, lnw_ref, lnb_ref, s_ref,
                      o_ref, so_ref, y_ref):
    w = jnp.exp(lw_ref[...])
    a, b, k, r = a_ref[...], b_ref[...], k_ref[...], r_ref[...]

    def value_row(i, carry):
        s = s_ref[i]
        sa = jnp.sum(s * a, axis=0, keepdims=True)
        s_new = s * w + sa * b + v_ref[pl.ds(i, 1), :] * k
        so_ref[i] = s_new
        y_ref[pl.ds(i, 1), :] = jnp.sum(s_new * r, axis=0, keepdims=True)
        return carry

    lax.fori_loop(0, RWKV_HEAD_DIM, value_row, 0, unroll=4)
    y = y_ref[...]
    mu = jnp.mean(y, axis=0, keepdims=True)
    var = jnp.mean(jnp.square(y - mu), axis=0, keepdims=True)
    yn = (y - mu) * lax.rsqrt(var + GN_EPS) * lnw_ref[...] + lnb_ref[...]
    bonus = jnp.sum(r_ref[...] * k_ref[...] * rk_ref[...], axis=0, keepdims=True) * v_ref[...]
    o_ref[...] = (yn + bonus) * g_ref[...]


def _rwkv_step(streams_t, s_t, p):
    bd = s_t.shape[-1]
    n = RWKV_HEAD_DIM
    col = lambda a: a.reshape(B_WIDTH, 1)
    vec = pl.BlockSpec((n, bd), lambda h: (h, 0))
    par = pl.BlockSpec((n, 1), lambda h: (h, 0))
    st = pl.BlockSpec((None, n, n, bd), lambda h: (h, 0, 0, 0))
    return pl.pallas_call(
        _rwkv_step_kernel,
        grid=(RWKV_HEADS,),
        in_specs=[vec] * 7 + [par] * 3 + [st],
        out_specs=[vec, st],
        out_shape=[jax.ShapeDtypeStruct((B_WIDTH, bd), F32), jax.ShapeDtypeStruct(s_t.shape, F32)],
        scratch_shapes=[pltpu.VMEM((n, bd), F32)],
        compiler_params=_cparams("parallel"),
        name="rwkv_step",
    )(*streams_t, col(p["r_k"]), col(p["ln_x_w"]), col(p["ln_x_b"]), s_t)


def _finish_kernel(x_ref, *refs, n_stats):
    stat_refs = refs[:n_stats]
    ob_ref, g_ref, wpa_ref, wpb_ref, wo_ref, g2_ref, rw_ref, rb_ref, h_ref, hn_ref, lg_ref = refs[n_stats:]
    if n_stats == 1:
        o_a = stat_refs[0][...]
    else:
        nums, dens, maxes = stat_refs[0::3], stat_refs[1::3], stat_refs[2::3]
        mx = functools.reduce(jnp.maximum, [m[...] for m in maxes])
        es = [jnp.exp(m[...] - mx) for m in maxes]
        num = sum(n_[...] * e for n_, e in zip(nums, es))
        den = sum(d_[...] * e for d_, e in zip(dens, es))
        o_a = num / den
    br_a = _dot1(o_a, wpa_ref[...])
    br_b = _dot1(ob_ref[...], wpb_ref[...])
    mix = g_ref[:, :D_MODEL] * br_a + g_ref[:, D_MODEL:] * br_b
    h = x_ref[...] + _dot1(mix, wo_ref[...])
    h_ref[...] = h
    hn = h * lax.rsqrt(jnp.mean(h * h, axis=-1, keepdims=True) + RMS_EPS) * g2_ref[...]
    hn_ref[...] = hn
    lg_ref[...] = _dot3(hn, rw_ref[...]) + rb_ref[...]


def _finish(x2, stats, o_b, gates, wb, tm):
    n = x2.shape[0]
    const = lambda i: (0, 0)
    rows = lambda w: pl.BlockSpec((tm, w), lambda i: (i, 0))
    params = [wb["w_pa"], wb["w_pb"], wb["w_o"], wb["norm2_g"], wb["router_w"], wb["router_b"]]
    return pl.pallas_call(
        functools.partial(_finish_kernel, n_stats=len(stats)),
        grid=(n // tm,),
        in_specs=[rows(D_MODEL)] + [rows(A_WIDTH)] * len(stats) + [rows(B_WIDTH), rows(2 * D_MODEL)]
                 + [pl.BlockSpec(a.shape, const) for a in params],
        out_specs=[rows(D_MODEL), rows(D_MODEL), rows(LOGIT_PAD)],
        out_shape=[jax.ShapeDtypeStruct((n, D_MODEL), F32), jax.ShapeDtypeStruct((n, D_MODEL), F32),
                   jax.ShapeDtypeStruct((n, LOGIT_PAD), F32)],
        compiler_params=_cparams("parallel"),
        name="finish",
    )(x2, *stats, o_b, gates, *params)


def _row(ref, i):
    return ref.at[pl.ds(i, 1)]


def _moe_kernel(blk_e_ref, nused_ref, tok_ref, tok_next_ref, hn_hbm, wgu_ref, bgu_ref, wd_ref, bdn_ref,
                ys_ref, xbuf_a, xbuf_b, sem):
    i = pl.program_id(0)
    nused = nused_ref[0]
    n_chunks = D_FF // COL_CHUNK
    rows_per_chunk = MOE_BLOCK // n_chunks
    bufs = (xbuf_a, xbuf_b)

    def issue(idx_ref, s, lo, hi):
        for r in range(lo, hi):
            pltpu.make_async_copy(_row(hn_hbm, idx_ref[0, 0, r]), _row(bufs[s], r), sem.at[s]).start(priority=r % 2)

    def drain(s):
        for _ in range(MOE_BLOCK):
            pltpu.make_async_copy(_row(hn_hbm, 0), _row(bufs[s], 0), sem.at[s]).wait()

    @pl.when(jnp.logical_and(i == 0, nused > 0))
    def _():
        issue(tok_ref, 0, 0, MOE_BLOCK)

    def block(s):
        drain(s)
        x = bufs[s][...].astype(BF16)

        def gate_up(c):
            issue(tok_next_ref, 1 - s, c * rows_per_chunk, (c + 1) * rows_per_chunk)
            lo = c * COL_CHUNK
            gate = _dg(x, wgu_ref[:, lo:lo + COL_CHUNK], NN) + bgu_ref[:, lo:lo + COL_CHUNK]
            up = _dg(x, wgu_ref[:, D_FF + lo:D_FF + lo + COL_CHUNK], NN) + bgu_ref[:, D_FF + lo:D_FF + lo + COL_CHUNK]
            return gate, up

        acc = jnp.zeros((MOE_BLOCK, D_MODEL), F32)
        nxt = gate_up(0)
        for c in range(n_chunks):
            gate, up = nxt
            if c + 1 < n_chunks:
                nxt = gate_up(c + 1)
            gate = jnp.minimum(gate, SWIGLU_LIMIT)
            up = jnp.clip(up, -SWIGLU_LIMIT, SWIGLU_LIMIT)
            hid = (up + 1.0) * gate * _sigmoid(gate * SWIGLU_ALPHA)
            acc = acc + _dg(hid.astype(BF16), wd_ref[c * COL_CHUNK:(c + 1) * COL_CHUNK, :], NN)
        ys_ref[...] = acc + bdn_ref[...]

        @pl.when(i + 1 >= nused)
        def _():
            drain(1 - s)

    for s in range(2):
        pl.when(jnp.logical_and(i < nused, i % 2 == s))(functools.partial(block, s))

    @pl.when(i >= nused)
    def _():
        ys_ref[...] = jnp.zeros_like(ys_ref)


def _moe_blocks(hn, row_tok, blk_e, nused, wb):
    n_blocks = row_tok.shape[0]
    last = n_blocks - 1
    grid_spec = pltpu.PrefetchScalarGridSpec(
        num_scalar_prefetch=2,
        grid=(n_blocks,),
        in_specs=[
            pl.BlockSpec((1, 1, MOE_BLOCK), lambda i, be, nu: (i, 0, 0), memory_space=pltpu.SMEM),
            pl.BlockSpec((1, 1, MOE_BLOCK), lambda i, be, nu: (jnp.minimum(i + 1, last), 0, 0),
                         memory_space=pltpu.SMEM),
            pl.BlockSpec(memory_space=pl.ANY),
            pl.BlockSpec((None, D_MODEL, 2 * D_FF), lambda i, be, nu: (be[i], 0, 0)),
            pl.BlockSpec((None, 1, 2 * D_FF), lambda i, be, nu: (be[i], 0, 0)),
            pl.BlockSpec((None, D_FF, D_MODEL), lambda i, be, nu: (be[i], 0, 0)),
            pl.BlockSpec((None, 1, D_MODEL), lambda i, be, nu: (be[i], 0, 0)),
        ],
        out_specs=pl.BlockSpec((MOE_BLOCK, D_MODEL), lambda i, be, nu: (i, 0)),
        scratch_shapes=[pltpu.VMEM((MOE_BLOCK, D_MODEL), F32), pltpu.VMEM((MOE_BLOCK, D_MODEL), F32),
                        pltpu.SemaphoreType.DMA((2,))],
    )
    return pl.pallas_call(
        _moe_kernel,
        grid_spec=grid_spec,
        out_shape=jax.ShapeDtypeStruct((n_blocks * MOE_BLOCK, D_MODEL), F32),
        compiler_params=_cparams("arbitrary"),
        name="moe_blocks",
    )(blk_e, nused, row_tok, row_tok, hn, wb["w_gu"], wb["b_gu"], wb["w_down"], wb["b_down"])


def _combine_kernel(pos_ref, pos_next_ref, h_ref, gate_ref, ys_hbm, y_ref, gbuf, sem):
    tm = h_ref.shape[0]
    rows = TOP_K * tm
    i = pl.program_id(0)
    n = pl.num_programs(0)
    slot = i % 2

    def issue(idx_ref, s):
        for kk in range(rows):
            pltpu.make_async_copy(_row(ys_hbm, idx_ref[0, 0, kk]), gbuf.at[s, pl.ds(kk, 1)], sem.at[s]).start(
                priority=kk % 2)

    @pl.when(i == 0)
    def _():
        issue(pos_ref, 0)

    @pl.when(i + 1 < n)
    def _():
        issue(pos_next_ref, 1 - slot)

    for kk in range(rows):
        pltpu.make_async_copy(_row(ys_hbm, 0), gbuf.at[slot, pl.ds(0, 1)], sem.at[slot]).wait()
    gv = gate_ref[...]
    ff = gv[:, 0:1] * gbuf[slot, 0:tm, :]
    for j in range(1, TOP_K):
        ff = ff + gv[:, j:j + 1] * gbuf[slot, j * tm:(j + 1) * tm, :]
    y_ref[...] = h_ref[...] + ff


def _combine(h, gates_pad, pos, ys):
    n = h.shape[0]
    tm = min(COMBINE_TOKENS, n)
    steps = n // tm
    rows = TOP_K * tm
    pos3 = pos.reshape(steps, tm, TOP_K).transpose(0, 2, 1).reshape(steps, 1, rows)
    return pl.pallas_call(
        _combine_kernel,
        grid=(steps,),
        in_specs=[
            pl.BlockSpec((1, 1, rows), lambda i: (i, 0, 0), memory_space=pltpu.SMEM),
            pl.BlockSpec((1, 1, rows), lambda i: (jnp.minimum(i + 1, steps - 1), 0, 0), memory_space=pltpu.SMEM),
            pl.BlockSpec((tm, D_MODEL), lambda i: (i, 0)),
            pl.BlockSpec((tm, LOGIT_PAD), lambda i: (i, 0)),
            pl.BlockSpec(memory_space=pl.ANY),
        ],
        out_specs=pl.BlockSpec((tm, D_MODEL), lambda i: (i, 0)),
        out_shape=jax.ShapeDtypeStruct((n, D_MODEL), F32),
        scratch_shapes=[pltpu.VMEM((2, rows, D_MODEL), F32), pltpu.SemaphoreType.DMA((2,))],
        compiler_params=_cparams("arbitrary"),
        name="moe_combine",
    )(pos3, pos3, h, gates_pad, ys)


def _route(logits):
    n = logits.shape[0]
    nk = n * TOP_K
    top_v, top_e = lax.top_k(logits, TOP_K)
    gates = jax.nn.softmax(top_v, axis=-1)
    flat_e = top_e.reshape(nk)
    onehot = (flat_e[:, None] == jnp.arange(N_EXPERTS, dtype=top_e.dtype)[None, :]).astype(jnp.int32)
    csum = jnp.cumsum(onehot, axis=0)
    counts = csum[-1]
    starts = jnp.cumsum(counts) - counts
    pcounts = (counts + MOE_BLOCK - 1) // MOE_BLOCK * MOE_BLOCK
    pends = jnp.cumsum(pcounts)
    pstarts = pends - pcounts
    pos = jnp.sum(onehot * (csum - 1 + pstarts[None, :]), axis=1).astype(jnp.int32)
    n_blocks = -(-nk // MOE_BLOCK) + N_EXPERTS
    blk_start = jnp.arange(n_blocks, dtype=jnp.int32) * MOE_BLOCK
    blk_e = jnp.minimum(jnp.sum(blk_start[:, None] >= pends[None, :], axis=1), N_EXPERTS - 1).astype(jnp.int32)
    nused = (pends[-1:] // MOE_BLOCK).astype(jnp.int32)
    order = jnp.argsort(flat_e)
    e_row = jnp.repeat(blk_e, MOE_BLOCK)
    rank = jnp.arange(n_blocks * MOE_BLOCK, dtype=jnp.int32) - pstarts[e_row]
    src = jnp.minimum(starts[e_row] + rank, nk - 1)
    row_tok = jnp.where(rank < counts[e_row], order[src] // TOP_K, 0).astype(jnp.int32)
    gates_pad = jnp.pad(gates, ((0, 0), (0, LOGIT_PAD - TOP_K)))
    return pos, row_tok.reshape(n_blocks, 1, MOE_BLOCK), blk_e, nused, gates_pad


def _moe(h, hn, logits_pad, wb):
    pos, row_tok, blk_e, nused, gates_pad = _route(logits_pad[:, :N_EXPERTS])
    ys = _moe_blocks(hn, row_tok, blk_e, nused, wb)
    return _combine(h, gates_pad, pos, ys)


def kernel(x_prompt, x_sample, cache_kv_w128, cache_kv_w512, cache_kv_w2048, state_wkv, state_shift,
           norm1_g, w_in, q_norm_g, k_norm_g, mu_shift, w0, w2, a0, a2, g2, k_k, k_a, r_k,
           ln_x_w, ln_x_b, w_pa, w_pb, w_o, norm2_g, router_w, router_b, w_gu, b_gu, w_down, b_down):
    bp, seq, _ = x_prompt.shape
    bd, t_s, _ = x_sample.shape
    assert t_s == 1
    rw = dict(mu_shift=mu_shift, w0=w0, w2=w2, a0=a0, a2=a2, g2=g2, k_k=k_k, k_a=k_a, r_k=r_k,
              ln_x_w=ln_x_w, ln_x_b=ln_x_b)
    wb = dict(
        w_pa=w_pa.astype(BF16), w_pb=w_pb.astype(BF16), w_o=w_o.astype(BF16),
        norm2_g=norm2_g.reshape(1, D_MODEL),
        router_w=jnp.pad(router_w, ((0, 0), (0, LOGIT_PAD - N_EXPERTS))),
        router_b=jnp.pad(router_b, (0, LOGIT_PAD - N_EXPERTS)).reshape(1, LOGIT_PAD),
        w_gu=w_gu.astype(BF16), b_gu=b_gu.reshape(N_EXPERTS, 1, 2 * D_FF),
        w_down=w_down.astype(BF16), b_down=b_down.reshape(N_EXPERTS, 1, D_MODEL),
    )
    w_in_b = w_in.astype(BF16)
    mult = jnp.stack([jnp.tile(q_norm_g[c // 3] * (HEAD_DIM ** -0.5) if c % 3 == 0 else
                               (k_norm_g[c // 3] if c % 3 == 1 else jnp.ones((HEAD_DIM,), F32)),
                               HEADS_PER_GROUP) for c in range(A_COLS // COL_CHUNK)])
    bd_head = _seg_ones(COL_CHUNK, HEAD_DIM, 1.0 / HEAD_DIM)

    xp2 = x_prompt.reshape(bp * seq, D_MODEL)
    pa, ps, pg = _in_proj(xp2, norm1_g, w_in_b, mult, bd_head, tm=256)
    pa3 = pa.reshape(bp, seq, A_COLS)
    stats = []
    kv_p = []
    for g in range(N_GROUPS):
        stats += _attn_prompt(pa3, g)
        keep = min(GROUP_WINDOWS[g], seq)
        lo = g * 3 * A_WIDTH + A_WIDTH
        kv_p.append(pa3[:, seq - keep:, lo:lo + 2 * A_WIDTH].reshape(bp, keep, 2, HEADS_PER_GROUP, HEAD_DIM))
    ps3 = ps.reshape(bp, seq, SHIFT_WIDTH)
    streams = _rwkv_prep(ps3, jnp.zeros((bp, 1, SHIFT_WIDTH), F32), rw, tm=512)
    o_b, wkv_p = _rwkv_scan(streams, rw, chunk=SCAN_CHUNK)
    h, hn, lg = _finish(xp2, stats, o_b.reshape(bp * seq, B_WIDTH), pg, wb, tm=256)
    y_prompt = _moe(h, hn, lg, wb).reshape(bp, seq, D_MODEL)
    shift_p = ps3[:, -1:]

    caches = (cache_kv_w128, cache_kv_w512, cache_kv_w2048)
    xs2 = x_sample.reshape(bd, D_MODEL)
    pa_s, ps_s, pg_s = _in_proj(xs2, norm1_g, w_in_b, mult, bd_head, tm=bd)
    caches_t = [jnp.transpose(c.astype(F32), (0, 2, 3, 4, 1)) for c in caches]
    *kv_t, oa_t = _sample_cache(pa_s.T, caches_t)
    kv_s = [jnp.transpose(t, (0, 4, 1, 2, 3)).astype(c.dtype) for t, c in zip(kv_t, caches)]
    streams_t = _rwkv_prep_t(ps_s, state_shift.reshape(bd, SHIFT_WIDTH).astype(F32), rw)
    ob_t, s_t = _rwkv_step(streams_t, jnp.transpose(state_wkv.astype(F32), (1, 2, 3, 0)), rw)
    wkv_s = jnp.transpose(s_t, (3, 0, 1, 2))
    h_s, hn_s, lg_s = _finish(xs2, [oa_t.T], ob_t.T, pg_s, wb, tm=bd)
    y_sample = _moe(h_s, hn_s, lg_s, wb).reshape(bd, 1, D_MODEL)
    shift_s = ps_s.reshape(bd, 1, SHIFT_WIDTH)

    return (y_prompt, y_sample, kv_p[0], kv_p[1], kv_p[2], wkv_p.astype(state_wkv.dtype), shift_p,
            kv_s[0], kv_s[1], kv_s[2], wkv_s.astype(state_wkv.dtype), shift_s.astype(state_shift.dtype))
```

```python
import functools

import jax
import jax.numpy as jnp
from jax import lax
from jax.experimental import pallas as pl
from jax.experimental.pallas import tpu as pltpu

F32 = jnp.float32
BF16 = jnp.bfloat16

D_MODEL = 1024
GROUP_WINDOWS = (128, 512, 2048)
GROUP_DILATIONS = (1, 4, 16)
N_GROUPS = 3
HEADS_PER_GROUP = 4
HEAD_DIM = 64
A_WIDTH = HEADS_PER_GROUP * HEAD_DIM
A_COLS = N_GROUPS * 3 * A_WIDTH
RWKV_HEADS = 8
RWKV_HEAD_DIM = 64
B_WIDTH = RWKV_HEADS * RWKV_HEAD_DIM
DECAY_LORA = 64
AAA_LORA = 64
GATE_LORA = 128
SHIFT_WIDTH = 3 * B_WIDTH + DECAY_LORA + AAA_LORA + GATE_LORA
D_IN = A_COLS + SHIFT_WIDTH + 2 * D_MODEL
N_EXPERTS = 32
TOP_K = 4
D_FF = D_MODEL
SWIGLU_LIMIT = 7.0
SWIGLU_ALPHA = 1.702
MOE_BLOCK = 256
RMS_EPS = 1e-6
GN_EPS = 64e-5

N_BACK = 128
COL_CHUNK = 256
SCAN_CHUNK = 64
LANE_TILE = 128
LOGIT_PAD = LANE_TILE
COMBINE_TOKENS = 128
VMEM_LIMIT = 48 * 1024 * 1024


def _cparams(*sem):
    return pltpu.CompilerParams(dimension_semantics=sem, vmem_limit_bytes=VMEM_LIMIT)


def _sigmoid(x):
    return 1.0 / (1.0 + jnp.exp(-x))


def _split(x):
    hi = x.astype(BF16)
    lo = (x - hi.astype(F32)).astype(BF16)
    return hi, lo


def _dg(a, b, dims):
    return lax.dot_general(a, b, (dims, ((), ())), preferred_element_type=F32)


NN = ((1,), (0,))
NT = ((1,), (1,))
TN = ((0,), (0,))


def _dot1(a, b, dims=NN):
    return _dg(a.astype(BF16), b.astype(BF16), dims)


def _dot3(a, b, dims=NN):
    ah, al = _split(a)
    bh, bl = _split(b)
    return _dg(ah, bh, dims) + _dg(al, bh, dims) + _dg(ah, bl, dims)


def _split3(x):
    hi = x.astype(BF16)
    r1 = x - hi.astype(F32)
    mid = r1.astype(BF16)
    lo = (r1 - mid.astype(F32)).astype(BF16)
    return hi, mid, lo


def _dot_exact_rhs(a, b_bf16):
    return sum(_dg(t, b_bf16, NN) for t in _split3(a))


def _dot_exact_lhs(l_bf16, x):
    return sum(_dg(l_bf16, t, NN) for t in _split3(x))


def _seg_ones(width, seg, scale):
    i = jnp.arange(width)[:, None] // seg
    j = jnp.arange(width)[None, :] // seg
    return jnp.where(i == j, scale, 0.0).astype(BF16)


def _in_proj_kernel(x_ref, g1_ref, w_ref, mult_ref, bd_ref, pa_ref, ps_ref, pg_ref):
    x = x_ref[...]
    xn = x * lax.rsqrt(jnp.mean(x * x, axis=-1, keepdims=True) + RMS_EPS) * g1_ref[...]
    xb = xn.astype(BF16)
    bd = bd_ref[...]
    for c in range(A_COLS // COL_CHUNK):
        acc = _dg(xb, w_ref[:, c * COL_CHUNK:(c + 1) * COL_CHUNK], NN)
        if c % 3 != 2:
            ms = _dot_exact_rhs(acc * acc, bd)
            acc = acc * lax.rsqrt(ms + RMS_EPS) * mult_ref[c:c + 1, :]
        pa_ref[:, c * COL_CHUNK:(c + 1) * COL_CHUNK] = acc
    for c in range(SHIFT_WIDTH // COL_CHUNK):
        lo = A_COLS + c * COL_CHUNK
        ps_ref[:, c * COL_CHUNK:(c + 1) * COL_CHUNK] = _dg(xb, w_ref[:, lo:lo + COL_CHUNK], NN)
    for c in range(2 * D_MODEL // COL_CHUNK):
        lo = A_COLS + SHIFT_WIDTH + c * COL_CHUNK
        pg_ref[:, c * COL_CHUNK:(c + 1) * COL_CHUNK] = _sigmoid(_dg(xb, w_ref[:, lo:lo + COL_CHUNK], NN)).astype(BF16)


def _in_proj(x2, norm1_g, w_in_b, mult, bd, tm):
    n = x2.shape[0]
    const = lambda i: (0, 0)
    return pl.pallas_call(
        _in_proj_kernel,
        grid=(n // tm,),
        in_specs=[
            pl.BlockSpec((tm, D_MODEL), lambda i: (i, 0)),
            pl.BlockSpec((1, D_MODEL), const),
            pl.BlockSpec((D_MODEL, D_IN), const, pipeline_mode=pl.Buffered(1)),
            pl.BlockSpec(mult.shape, const),
            pl.BlockSpec(bd.shape, const),
        ],
        out_specs=[
            pl.BlockSpec((tm, A_COLS), lambda i: (i, 0)),
            pl.BlockSpec((tm, SHIFT_WIDTH), lambda i: (i, 0)),
            pl.BlockSpec((tm, 2 * D_MODEL), lambda i: (i, 0)),
        ],
        out_shape=[
            jax.ShapeDtypeStruct((n, A_COLS), F32),
            jax.ShapeDtypeStruct((n, SHIFT_WIDTH), F32),
            jax.ShapeDtypeStruct((n, 2 * D_MODEL), BF16),
        ],
        compiler_params=_cparams("parallel"),
        name="in_proj",
    )(x2, norm1_g.reshape(1, D_MODEL), w_in_b, mult, bd)


def _attn_kernel(*refs, dil):
    halves = A_WIDTH // LANE_TILE
    q_refs, kp_refs, kc_refs, vp_refs, vc_refs = (refs[i * halves:(i + 1) * halves] for i in range(5))
    out_refs = refs[5 * halves:5 * halves + 3]
    qs, ks, vs = refs[5 * halves + 3:5 * halves + 6]
    stat_s = refs[5 * halves + 6:5 * halves + 9]
    stage = refs[5 * halves + 9:]
    n = pl.program_id(1)
    qi = lax.broadcasted_iota(jnp.int32, (N_BACK, 2 * N_BACK), 0)
    kj = lax.broadcasted_iota(jnp.int32, (N_BACK, 2 * N_BACK), 1)
    first_prev = jnp.where(n > 0, 0, N_BACK)
    lo = jnp.where(kj < N_BACK, qi + first_prev, N_BACK)
    hi = jnp.where(kj < N_BACK, N_BACK - 1, qi + N_BACK)
    mask = jnp.logical_and(kj >= lo, kj <= hi)

    def stream(r, carry):
        rows = pl.ds(r, N_BACK, stride=dil) if dil > 1 else slice(None)
        for c in range(halves):
            lanes = slice(c * LANE_TILE, (c + 1) * LANE_TILE)
            qs[:, lanes] = q_refs[c][rows, :]
            ks[0:N_BACK, lanes] = kp_refs[c][rows, :]
            ks[N_BACK:, lanes] = kc_refs[c][rows, :]
            vs[0:N_BACK, lanes] = vp_refs[c][rows, :]
            vs[N_BACK:, lanes] = vc_refs[c][rows, :]
        sls = [slice(h * HEAD_DIM, (h + 1) * HEAD_DIM) for h in range(HEADS_PER_GROUP)]
        s = [jnp.where(mask, _dot1(qs[:, sl], ks[:, sl], NT), -jnp.inf) for sl in sls]
        m = [jnp.max(t, axis=-1, keepdims=True) for t in s]
        p = [jnp.exp(t - mm) for t, mm in zip(s, m)]
        den = [jnp.sum(t, axis=-1, keepdims=True) for t in p]
        num = [_dot1(t, vs[:, sl]) for t, sl in zip(p, sls)]
        for h, sl in enumerate(sls):
            stat_s[0][:, sl] = num[h]
            stat_s[1][:, sl] = jnp.broadcast_to(den[h], (N_BACK, HEAD_DIM))
            stat_s[2][:, sl] = jnp.broadcast_to(m[h], (N_BACK, HEAD_DIM))
        for i in range(3):
            if dil > 1:
                for c in range(halves):
                    stage[i * halves + c][rows, :] = stat_s[i][:, c * LANE_TILE:(c + 1) * LANE_TILE]
            else:
                out_refs[i][...] = stat_s[i][...]
        return carry

    if dil > 1:
        lax.fori_loop(0, dil, stream, 0, unroll=2)
        for i in range(3):
            for c in range(halves):
                out_refs[i][:, c * LANE_TILE:(c + 1) * LANE_TILE] = stage[i * halves + c][...]
    else:
        stream(0, 0)


def _attn_prompt(pa3, g):
    bsz, seq, _ = pa3.shape
    dil = GROUP_DILATIONS[g]
    rows = N_BACK * dil
    nb = seq // rows
    halves = A_WIDTH // LANE_TILE

    def specs(slab, prev):
        def one(c):
            col = (3 * g + slab) * halves + c
            if prev:
                return pl.BlockSpec((None, rows, LANE_TILE), lambda b, n: (b, jnp.maximum(n - 1, 0), col))
            return pl.BlockSpec((None, rows, LANE_TILE), lambda b, n: (b, n, col))
        return [one(c) for c in range(halves)]

    in_specs = specs(0, False) + specs(1, True) + specs(1, False) + specs(2, True) + specs(2, False)
    out_spec = pl.BlockSpec((None, rows, A_WIDTH), lambda b, n: (b, n, 0))
    out_sds = jax.ShapeDtypeStruct((bsz, seq, A_WIDTH), F32)
    scratch = [pltpu.VMEM((N_BACK, A_WIDTH), F32), pltpu.VMEM((2 * N_BACK, A_WIDTH), F32),
               pltpu.VMEM((2 * N_BACK, A_WIDTH), F32)] + [pltpu.VMEM((N_BACK, A_WIDTH), F32)] * 3
    if dil > 1:
        scratch += [pltpu.VMEM((rows, LANE_TILE), F32)] * (3 * halves)
    outs = pl.pallas_call(
        functools.partial(_attn_kernel, dil=dil),
        grid=(bsz, nb),
        in_specs=in_specs,
        out_specs=[out_spec] * 3,
        out_shape=[out_sds] * 3,
        scratch_shapes=scratch,
        compiler_params=_cparams("parallel", "arbitrary"),
        name=f"attn_prompt_g{g}",
    )(*([pa3] * len(in_specs)))
    return [o.reshape(bsz * seq, A_WIDTH) for o in outs]


def _sample_cache_kernel(pat_ref, c0_ref, c1_ref, c2_ref, o0_ref, o1_ref, o2_ref, oa_ref):
    b = pl.program_id(0)
    n_b = pat_ref.shape[1]

    @pl.when(b == 0)
    def _():
        oa_ref[...] = jnp.zeros_like(oa_ref)

    lane_b = lax.broadcasted_iota(jnp.int32, pat_ref.shape, 1) == b
    col = jnp.sum(jnp.where(lane_b, pat_ref[...], 0.0), axis=1, keepdims=True)
    c_refs = (c0_ref, c1_ref, c2_ref)
    o_refs = (o0_ref, o1_ref, o2_ref)
    stats = []
    for g in range(N_GROUPS):
        dil = GROUP_DILATIONS[g]
        win = GROUP_WINDOWS[g]
        base = g * 3 * A_WIDTH
        lane = lax.broadcasted_iota(jnp.int32, (HEAD_DIM, win), 1)
        last = lane == win - 1
        live = lax.broadcasted_iota(jnp.int32, (1, win), 1) % dil == 0
        for h in range(HEADS_PER_GROUP):
            o = h * HEAD_DIM
            q = col[base + o:base + o + HEAD_DIM]
            kn = col[base + A_WIDTH + o:base + A_WIDTH + o + HEAD_DIM]
            vn = col[base + 2 * A_WIDTH + o:base + 2 * A_WIDTH + o + HEAD_DIM]
            kt = c_refs[g][0, 0, h]
            vt = c_refs[g][0, 1, h]
            o_refs[g][0, 0, h] = jnp.where(last, kn, pltpu.roll(kt, win - 1, axis=1))
            o_refs[g][0, 1, h] = jnp.where(last, vn, pltpu.roll(vt, win - 1, axis=1))
            s_c = jnp.where(live, jnp.sum(kt * q, axis=0, keepdims=True), -jnp.inf)
            s_n = jnp.sum(kn * q, axis=0, keepdims=True)
            m = jnp.maximum(jnp.max(s_c, axis=1, keepdims=True), s_n)
            p_c = jnp.exp(s_c - m)
            p_n = jnp.exp(s_n - m)
            den = jnp.sum(p_c, axis=1, keepdims=True) + p_n
            num = jnp.sum(vt * p_c, axis=1, keepdims=True) + p_n * vn
            stats.append((num, den, m))
    outs = []
    for h in range(HEADS_PER_GROUP):
        per_g = [stats[g * HEADS_PER_GROUP + h] for g in range(N_GROUPS)]
        mx = functools.reduce(jnp.maximum, [m for _, _, m in per_g])
        num = sum(n_ * jnp.exp(m - mx) for n_, _, m in per_g)
        den = sum(d_ * jnp.exp(m - mx) for _, d_, m in per_g)
        outs.append(num / den)
    o_col = jnp.concatenate(outs, axis=0)
    lane_o = lax.broadcasted_iota(jnp.int32, (A_WIDTH, n_b), 1) == b
    oa_ref[...] = jnp.where(lane_o, o_col, oa_ref[...])


def _sample_cache(pa_t, caches_t):
    bd = pa_t.shape[1]
    cspecs = [pl.BlockSpec((1, 2, HEADS_PER_GROUP, HEAD_DIM, w), lambda b: (b, 0, 0, 0, 0)) for w in GROUP_WINDOWS]
    return pl.pallas_call(
        _sample_cache_kernel,
        grid=(bd,),
        in_specs=[pl.BlockSpec(pa_t.shape, lambda b: (0, 0))] + cspecs,
        out_specs=cspecs + [pl.BlockSpec((A_WIDTH, bd), lambda b: (0, 0))],
        out_shape=[jax.ShapeDtypeStruct(c.shape, c.dtype) for c in caches_t]
                  + [jax.ShapeDtypeStruct((A_WIDTH, bd), F32)],
        compiler_params=_cparams("arbitrary"),
        name="sample_cache",
    )(pa_t, *caches_t)


def _rwkv_prep_math(pf, shifted, mu_ref, w0_ref, w2_ref, a0_ref, a2_ref, g2_ref, kk_ref, ka_ref, bd_ref):
    z = pf + mu_ref[...] * (shifted - pf)
    c1, c2, c3 = B_WIDTH, 2 * B_WIDTH, 3 * B_WIDTH
    r, k, v = z[:, :c1], z[:, c1:c2], z[:, c2:c3]
    wl = z[:, c3:c3 + DECAY_LORA]
    al = z[:, c3 + DECAY_LORA:c3 + DECAY_LORA + AAA_LORA]
    gl = z[:, c3 + DECAY_LORA + AAA_LORA:]
    xw = w0_ref[...] + _dot3(jnp.tanh(wl), w2_ref[...])
    w_log = -(jnp.maximum(-xw, 0.0) + jnp.log(1.0 + jnp.exp(-jnp.abs(xw)))) - 0.5
    lw = -jnp.exp(w_log)
    a = _sigmoid(a0_ref[...] + _dot3(al, a2_ref[...]))
    g = _dot3(_sigmoid(gl), g2_ref[...])
    kk = k * kk_ref[...]
    bd = bd_ref[...]
    kks = []
    for c in range(B_WIDTH // 128):
        kc = kk[:, c * 128:(c + 1) * 128]
        nrm = jnp.sqrt(_dot_exact_rhs(kc * kc, bd))
        kks.append(kc / jnp.maximum(nrm, 1e-12))
    kkn = jnp.concatenate(kks, axis=1)
    return r, lw, k * (1.0 + (a - 1.0) * ka_ref[...]), v, -kkn, kkn * a, g


def _rwkv_prep_kernel(ps_ref, pv_ref, p0_ref, *refs):
    param_refs, out_refs = refs[:9], refs[9:]
    i = pl.program_id(1)
    pf = ps_ref[...]
    prev_row = jnp.where(i == 0, p0_ref[...], pv_ref[7:8, :])
    row = lax.broadcasted_iota(jnp.int32, pf.shape, 0)
    shifted = jnp.where(row == 0, prev_row, pltpu.roll(pf, 1, axis=0))
    for o_ref, val in zip(out_refs, _rwkv_prep_math(pf, shifted, *param_refs)):
        o_ref[...] = val


def _rwkv_prep_t_kernel(ps_ref, prev_ref, *refs):
    param_refs, out_refs = refs[:9], refs[9:]
    for o_ref, val in zip(out_refs, _rwkv_prep_math(ps_ref[...], prev_ref[...], *param_refs)):
        o_ref[...] = val.T


def _prep_params(p):
    row = lambda a: a.reshape(1, -1)
    return [row(p["mu_shift"]), row(p["w0"]), p["w2"], row(p["a0"]), p["a2"], p["g2"],
            row(p["k_k"]), row(p["k_a"]), _seg_ones(128, RWKV_HEAD_DIM, 1.0)]


def _rwkv_prep(ps3, prev0, p, tm):
    bsz, t, _ = ps3.shape
    const = lambda b, i: (0, 0)
    params = _prep_params(p)
    out_spec = pl.BlockSpec((None, tm, B_WIDTH), lambda b, i: (b, i, 0))
    out_sds = jax.ShapeDtypeStruct((bsz, t, B_WIDTH), F32)
    return pl.pallas_call(
        _rwkv_prep_kernel,
        grid=(bsz, t // tm),
        in_specs=[
            pl.BlockSpec((None, tm, SHIFT_WIDTH), lambda b, i: (b, i, 0)),
            pl.BlockSpec((None, 8, SHIFT_WIDTH), lambda b, i: (b, jnp.maximum(i * (tm // 8) - 1, 0), 0)),
            pl.BlockSpec((None, 1, SHIFT_WIDTH), lambda b, i: (b, 0, 0)),
        ] + [pl.BlockSpec(a.shape, const) for a in params],
        out_specs=[out_spec] * 7,
        out_shape=[out_sds] * 7,
        compiler_params=_cparams("parallel", "arbitrary"),
        name="rwkv_prep",
    )(ps3, ps3, prev0, *params)


def _rwkv_prep_t(ps2, prev2, p):
    bd = ps2.shape[0]
    params = _prep_params(p)
    full = lambda a: pl.BlockSpec(a.shape, lambda i: (0, 0))
    out_sds = jax.ShapeDtypeStruct((B_WIDTH, bd), F32)
    return pl.pallas_call(
        _rwkv_prep_t_kernel,
        grid=(1,),
        in_specs=[full(ps2), full(prev2)] + [full(a) for a in params],
        out_specs=[pl.BlockSpec((B_WIDTH, bd), lambda i: (0, 0))] * 7,
        out_shape=[out_sds] * 7,
        compiler_params=_cparams("arbitrary"),
        name="rwkv_prep_t",
    )(ps2, prev2, *params)


def _rwkv_scan_kernel(r_ref, lw_ref, k_ref, v_ref, a_ref, b_ref, g_ref, rk_ref, lnw_ref, lnb_ref,
                      tril_ref, o_ref, s_ref):
    chunk = r_ref.shape[0]

    @pl.when(pl.program_id(1) == 0)
    def _():
        s_ref[...] = jnp.zeros_like(s_ref)

    lw = lw_ref[...]
    r, k, v = r_ref[...], k_ref[...], v_ref[...]
    cum = _dot_exact_lhs(tril_ref[...], lw)
    e_in = jnp.exp(cum)
    e_inv = jnp.exp(-cum)
    at = a_ref[...] * jnp.exp(cum - lw)
    rt = r * e_in
    bt = b_ref[...] * e_inv
    kt = k * e_inv
    g_last = e_in[chunk - 1:chunk, :]

    ti = lax.broadcasted_iota(jnp.int32, (chunk, 2 * chunk), 0)
    si = lax.broadcasted_iota(jnp.int32, (chunk, 2 * chunk), 1)
    si = jnp.where(si >= chunk, si - chunk, si)
    strict = si < ti
    incl = si <= ti
    eye = jnp.where(lax.broadcasted_iota(jnp.int32, (chunk, chunk), 0)
                    == lax.broadcasted_iota(jnp.int32, (chunk, chunk), 1), 1.0, 0.0).astype(F32)
    n_sq = max(chunk.bit_length() - 2, 0)

    heads = range(RWKV_HEADS)
    sls = [slice(h * RWKV_HEAD_DIM, (h + 1) * RWKV_HEAD_DIM) for h in heads]
    ar = [jnp.concatenate([at[:, sl], rt[:, sl]], axis=0) for sl in sls]
    bk = [jnp.concatenate([bt[:, sl], kt[:, sl]], axis=0) for sl in sls]
    s_old = [s_ref[h] for h in heads]
    m_all = [_dot1(ar[h], bk[h], NT) for h in heads]
    gs = [_dot1(ar[h], s_old[h], NT) for h in heads]
    a_m = [jnp.where(strict, m_all[h][:chunk], 0.0) for h in heads]
    r_m = [jnp.where(incl, m_all[h][chunk:], 0.0) for h in heads]
    pw = [a_m[h][:, :chunk] for h in heads]
    x = [eye + pw[h] for h in heads]
    for _ in range(n_sq):
        pw = [_dot1(pw[h], pw[h]) for h in heads]
        x = [x[h] + _dot1(x[h], pw[h]) for h in heads]
    v_h = [v[:, sl] for sl in sls]
    av = [_dot1(a_m[h][:, chunk:], v_h[h]) for h in heads]
    u = [_dot1(x[h], gs[h][:chunk] + av[h]) for h in heads]
    uv = [jnp.concatenate([u[h], v_h[h]], axis=0) for h in heads]
    y = [gs[h][chunk:] + _dot1(r_m[h], uv[h]) for h in heads]
    for h in heads:
        s_ref[h] = (s_old[h] + _dot3(uv[h], bk[h], TN)) * g_last[:, sls[h]]
    for h, sl in enumerate(sls):
        mu = jnp.mean(y[h], axis=-1, keepdims=True)
        var = jnp.mean(jnp.square(y[h] - mu), axis=-1, keepdims=True)
        yn = (y[h] - mu) * lax.rsqrt(var + GN_EPS) * lnw_ref[:, sl] + lnb_ref[:, sl]
        bonus = jnp.sum(r[:, sl] * k[:, sl] * rk_ref[:, sl], axis=-1, keepdims=True) * v_h[h]
        o_ref[:, sl] = (yn + bonus) * g_ref[:, sl]


def _rwkv_scan(streams, p, chunk):
    bsz, t, _ = streams[0].shape
    row = lambda a: a.reshape(1, B_WIDTH)
    tril = (jnp.arange(chunk)[None, :] <= jnp.arange(chunk)[:, None]).astype(BF16)
    const = lambda b, c: (0, 0)
    blk = pl.BlockSpec((None, chunk, B_WIDTH), lambda b, c: (b, c, 0))
    st_shape = (bsz, RWKV_HEADS, RWKV_HEAD_DIM, RWKV_HEAD_DIM)
    st = pl.BlockSpec((None,) + st_shape[1:], lambda b, c: (b, 0, 0, 0))
    vec = pl.BlockSpec((1, B_WIDTH), const)
    return pl.pallas_call(
        _rwkv_scan_kernel,
        grid=(bsz, t // chunk),
        in_specs=[blk] * 7 + [vec, vec, vec, pl.BlockSpec((chunk, chunk), const)],
        out_specs=[blk, st],
        out_shape=[jax.ShapeDtypeStruct((bsz, t, B_WIDTH), F32), jax.ShapeDtypeStruct(st_shape, F32)],
        compiler_params=_cparams("parallel", "arbitrary"),
        name="rwkv_scan",
    )(*streams, row(p["r_k"]), row(p["ln_x_w"]), row(p["ln_x_b"]), tril)


def _rwkv_step_kernel(r_ref, lw_ref, k_ref, v_ref, a_ref, b_ref, g_ref, rk_ref, lnw_ref, lnb_ref, s_ref,
                      o_ref, so_ref, y_ref):
    w = jnp.exp(lw_ref[...])
    a, b, k, r = a_ref[...], b_ref[...], k_ref[...], r_ref[...]

    def value_row(i, carry):
        s = s_ref[i]
        sa = jnp.sum(s * a, axis=0, keepdims=True)
        s_new = s * w + sa * b + v_ref[pl.ds(i, 1), :] * k
        so_ref[i] = s_new
        y_ref[pl.ds(i, 1), :] = jnp.sum(s_new * r, axis=0, keepdims=True)
        return carry

    lax.fori_loop(0, RWKV_HEAD_DIM, value_row, 0, unroll=4)
    y = y_ref[...]
    mu = jnp.mean(y, axis=0, keepdims=True)
    var = jnp.mean(jnp.square(y - mu), axis=0, keepdims=True)
    yn = (y - mu) * lax.rsqrt(var + GN_EPS) * lnw_ref[...] + lnb_ref[...]
    bonus = jnp.sum(r_ref[...] * k_ref[...] * rk_ref[...], axis=0, keepdims=True) * v_ref[...]
    o_ref[...] = (yn + bonus) * g_ref[...]


def _rwkv_step(streams_t, s_t, p):
    bd = s_t.shape[-1]
    n = RWKV_HEAD_DIM
    col = lambda a: a.reshape(B_WIDTH, 1)
    vec = pl.BlockSpec((n, bd), lambda h: (h, 0))
    par = pl.BlockSpec((n, 1), lambda h: (h, 0))
    st = pl.BlockSpec((None, n, n, bd), lambda h: (h, 0, 0, 0))
    return pl.pallas_call(
        _rwkv_step_kernel,
        grid=(RWKV_HEADS,),
        in_specs=[vec] * 7 + [par] * 3 + [st],
        out_specs=[vec, st],
        out_shape=[jax.ShapeDtypeStruct((B_WIDTH, bd), F32), jax.ShapeDtypeStruct(s_t.shape, F32)],
        scratch_shapes=[pltpu.VMEM((n, bd), F32)],
        compiler_params=_cparams("parallel"),
        name="rwkv_step",
    )(*streams_t, col(p["r_k"]), col(p["ln_x_w"]), col(p["ln_x_b"]), s_t)


def _finish_kernel(x_ref, *refs, n_stats):
    stat_refs = refs[:n_stats]
    ob_ref, g_ref, wpa_ref, wpb_ref, wo_ref, g2_ref, rw_ref, rb_ref, h_ref, hn_ref, lg_ref = refs[n_stats:]
    if n_stats == 1:
        o_a = stat_refs[0][...]
    else:
        nums, dens, maxes = stat_refs[0::3], stat_refs[1::3], stat_refs[2::3]
        mx = functools.reduce(jnp.maximum, [m[...] for m in maxes])
        es = [jnp.exp(m[...] - mx) for m in maxes]
        num = sum(n_[...] * e for n_, e in zip(nums, es))
        den = sum(d_[...] * e for d_, e in zip(dens, es))
        o_a = num / den
    br_a = _dot1(o_a, wpa_ref[...])
    br_b = _dot1(ob_ref[...], wpb_ref[...])
    mix = g_ref[:, :D_MODEL] * br_a + g_ref[:, D_MODEL:] * br_b
    h = x_ref[...] + _dot1(mix, wo_ref[...])
    h_ref[...] = h
    hn = h * lax.rsqrt(jnp.mean(h * h, axis=-1, keepdims=True) + RMS_EPS) * g2_ref[...]
    hn_ref[...] = hn
    lg_ref[...] = _dot3(hn, rw_ref[...]) + rb_ref[...]


def _finish(x2, stats, o_b, gates, wb, tm):
    n = x2.shape[0]
    const = lambda i: (0, 0)
    rows = lambda w: pl.BlockSpec((tm, w), lambda i: (i, 0))
    params = [wb["w_pa"], wb["w_pb"], wb["w_o"], wb["norm2_g"], wb["router_w"], wb["router_b"]]
    return pl.pallas_call(
        functools.partial(_finish_kernel, n_stats=len(stats)),
        grid=(n // tm,),
        in_specs=[rows(D_MODEL)] + [rows(A_WIDTH)] * len(stats) + [rows(B_WIDTH), rows(2 * D_MODEL)]
                 + [pl.BlockSpec(a.shape, const) for a in params],
        out_specs=[rows(D_MODEL), rows(D_MODEL), rows(LOGIT_PAD)],
        out_shape=[jax.ShapeDtypeStruct((n, D_MODEL), F32), jax.ShapeDtypeStruct((n, D_MODEL), F32),
                   jax.ShapeDtypeStruct((n, LOGIT_PAD), F32)],
        compiler_params=_cparams("parallel"),
        name="finish",
    )(x2, *stats, o_b, gates, *params)


def _row(ref, i):
    return ref.at[pl.ds(i, 1)]


def _moe_kernel(blk_e_ref, nused_ref, tok_ref, tok_next_ref, hn_hbm, wgu_ref, bgu_ref, wd_ref, bdn_ref,
                ys_ref, xbuf_a, xbuf_b, sem):
    i = pl.program_id(0)
    nused = nused_ref[0]
    n_chunks = D_FF // COL_CHUNK
    issue_chunks = n_chunks // 2
    rows_per_chunk = MOE_BLOCK // issue_chunks
    bufs = (xbuf_a, xbuf_b)

    def issue(idx_ref, s, lo, hi):
        for r in range(lo, hi):
            pltpu.make_async_copy(_row(hn_hbm, idx_ref[0, 0, r]), _row(bufs[s], r), sem.at[s]).start(priority=r % 2)

    def drain(s):
        for _ in range(MOE_BLOCK):
            pltpu.make_async_copy(_row(hn_hbm, 0), _row(bufs[s], 0), sem.at[s]).wait()

    @pl.when(jnp.logical_and(i == 0, nused > 0))
    def _():
        issue(tok_ref, 0, 0, MOE_BLOCK)

    def block(s):
        drain(s)
        x = bufs[s][...].astype(BF16)

        def gate_up(c):
            if c < issue_chunks:
                issue(tok_next_ref, 1 - s, c * rows_per_chunk, (c + 1) * rows_per_chunk)
            lo = c * COL_CHUNK
            gate = _dg(x, wgu_ref[:, lo:lo + COL_CHUNK], NN) + bgu_ref[:, lo:lo + COL_CHUNK]
            up = _dg(x, wgu_ref[:, D_FF + lo:D_FF + lo + COL_CHUNK], NN) + bgu_ref[:, D_FF + lo:D_FF + lo + COL_CHUNK]
            return gate, up

        acc = jnp.zeros((MOE_BLOCK, D_MODEL), F32)
        nxt = gate_up(0)
        for c in range(n_chunks):
            gate, up = nxt
            if c + 1 < n_chunks:
                nxt = gate_up(c + 1)
            gate = jnp.minimum(gate, SWIGLU_LIMIT)
            up = jnp.clip(up, -SWIGLU_LIMIT, SWIGLU_LIMIT)
            hid = (up + 1.0) * gate * _sigmoid(gate * SWIGLU_ALPHA)
            acc = acc + _dg(hid.astype(BF16), wd_ref[c * COL_CHUNK:(c + 1) * COL_CHUNK, :], NN)
        ys_ref[...] = acc + bdn_ref[...]

        @pl.when(i + 1 >= nused)
        def _():
            drain(1 - s)

    for s in range(2):
        pl.when(jnp.logical_and(i < nused, i % 2 == s))(functools.partial(block, s))

    @pl.when(i >= nused)
    def _():
        ys_ref[...] = jnp.zeros_like(ys_ref)


def _moe_blocks(hn, row_tok, blk_e, nused, wb):
    n_blocks = row_tok.shape[0]
    last = n_blocks - 1
    grid_spec = pltpu.PrefetchScalarGridSpec(
        num_scalar_prefetch=2,
        grid=(n_blocks,),
        in_specs=[
            pl.BlockSpec((1, 1, MOE_BLOCK), lambda i, be, nu: (i, 0, 0), memory_space=pltpu.SMEM),
            pl.BlockSpec((1, 1, MOE_BLOCK), lambda i, be, nu: (jnp.minimum(i + 1, last), 0, 0),
                         memory_space=pltpu.SMEM),
            pl.BlockSpec(memory_space=pl.ANY),
            pl.BlockSpec((None, D_MODEL, 2 * D_FF), lambda i, be, nu: (be[i], 0, 0)),
            pl.BlockSpec((None, 1, 2 * D_FF), lambda i, be, nu: (be[i], 0, 0)),
            pl.BlockSpec((None, D_FF, D_MODEL), lambda i, be, nu: (be[i], 0, 0)),
            pl.BlockSpec((None, 1, D_MODEL), lambda i, be, nu: (be[i], 0, 0)),
        ],
        out_specs=pl.BlockSpec((MOE_BLOCK, D_MODEL), lambda i, be, nu: (i, 0)),
        scratch_shapes=[pltpu.VMEM((MOE_BLOCK, D_MODEL), F32), pltpu.VMEM((MOE_BLOCK, D_MODEL), F32),
                        pltpu.SemaphoreType.DMA((2,))],
    )
    return pl.pallas_call(
        _moe_kernel,
        grid_spec=grid_spec,
        out_shape=jax.ShapeDtypeStruct((n_blocks * MOE_BLOCK, D_MODEL), F32),
        compiler_params=_cparams("arbitrary"),
        name="moe_blocks",
    )(blk_e, nused, row_tok, row_tok, hn, wb["w_gu"], wb["b_gu"], wb["w_down"], wb["b_down"])


def _combine_kernel(pos_ref, pos_next_ref, h_ref, gate_ref, ys_hbm, y_ref, gbuf, sem):
    tm = h_ref.shape[0]
    rows = TOP_K * tm
    i = pl.program_id(0)
    n = pl.num_programs(0)
    slot = i % 2

    def issue(idx_ref, s):
        for kk in range(rows):
            pltpu.make_async_copy(_row(ys_hbm, idx_ref[0, 0, kk]), gbuf.at[s, pl.ds(kk, 1)], sem.at[s]).start(
                priority=kk % 2)

    @pl.when(i == 0)
    def _():
        issue(pos_ref, 0)

    @pl.when(i + 1 < n)
    def _():
        issue(pos_next_ref, 1 - slot)

    for kk in range(rows):
        pltpu.make_async_copy(_row(ys_hbm, 0), gbuf.at[slot, pl.ds(0, 1)], sem.at[slot]).wait()
    gv = gate_ref[...]
    ff = gv[:, 0:1] * gbuf[slot, 0:tm, :]
    for j in range(1, TOP_K):
        ff = ff + gv[:, j:j + 1] * gbuf[slot, j * tm:(j + 1) * tm, :]
    y_ref[...] = h_ref[...] + ff


def _combine(h, gates_pad, pos, ys):
    n = h.shape[0]
    tm = min(COMBINE_TOKENS, n)
    steps = n // tm
    rows = TOP_K * tm
    pos3 = pos.reshape(steps, tm, TOP_K).transpose(0, 2, 1).reshape(steps, 1, rows)
    return pl.pallas_call(
        _combine_kernel,
        grid=(steps,),
        in_specs=[
            pl.BlockSpec((1, 1, rows), lambda i: (i, 0, 0), memory_space=pltpu.SMEM),
            pl.BlockSpec((1, 1, rows), lambda i: (jnp.minimum(i + 1, steps - 1), 0, 0), memory_space=pltpu.SMEM),
            pl.BlockSpec((tm, D_MODEL), lambda i: (i, 0)),
            pl.BlockSpec((tm, LOGIT_PAD), lambda i: (i, 0)),
            pl.BlockSpec(memory_space=pl.ANY),
        ],
        out_specs=pl.BlockSpec((tm, D_MODEL), lambda i: (i, 0)),
        out_shape=jax.ShapeDtypeStruct((n, D_MODEL), F32),
        scratch_shapes=[pltpu.VMEM((2, rows, D_MODEL), F32), pltpu.SemaphoreType.DMA((2,))],
        compiler_params=_cparams("arbitrary"),
        name="moe_combine",
    )(pos3, pos3, h, gates_pad, ys)


def _route(logits):
    n = logits.shape[0]
    nk = n * TOP_K
    top_v, top_e = lax.top_k(logits, TOP_K)
    gates = jax.nn.softmax(top_v, axis=-1)
    flat_e = top_e.reshape(nk)
    onehot = (flat_e[:, None] == jnp.arange(N_EXPERTS, dtype=top_e.dtype)[None, :]).astype(jnp.int32)
    csum = jnp.cumsum(onehot, axis=0)
    counts = csum[-1]
    starts = jnp.cumsum(counts) - counts
    pcounts = (counts + MOE_BLOCK - 1) // MOE_BLOCK * MOE_BLOCK
    pends = jnp.cumsum(pcounts)
    pstarts = pends - pcounts
    pos = jnp.sum(onehot * (csum - 1 + pstarts[None, :]), axis=1).astype(jnp.int32)
    n_blocks = -(-nk // MOE_BLOCK) + N_EXPERTS
    blk_start = jnp.arange(n_blocks, dtype=jnp.int32) * MOE_BLOCK
    blk_e = jnp.minimum(jnp.sum(blk_start[:, None] >= pends[None, :], axis=1), N_EXPERTS - 1).astype(jnp.int32)
    nused = (pends[-1:] // MOE_BLOCK).astype(jnp.int32)
    order = jnp.argsort(flat_e)
    e_row = jnp.repeat(blk_e, MOE_BLOCK)
    rank = jnp.arange(n_blocks * MOE_BLOCK, dtype=jnp.int32) - pstarts[e_row]
    src = jnp.minimum(starts[e_row] + rank, nk - 1)
    row_tok = jnp.where(rank < counts[e_row], order[src] // TOP_K, 0).astype(jnp.int32)
    gates_pad = jnp.pad(gates, ((0, 0), (0, LOGIT_PAD - TOP_K)))
    return pos, row_tok.reshape(n_blocks, 1, MOE_BLOCK), blk_e, nused, gates_pad


def _moe(h, hn, logits_pad, wb):
    pos, row_tok, blk_e, nused, gates_pad = _route(logits_pad[:, :N_EXPERTS])
    ys = _moe_blocks(hn, row_tok, blk_e, nused, wb)
    return _combine(h, gates_pad, pos, ys)


def kernel(x_prompt, x_sample, cache_kv_w128, cache_kv_w512, cache_kv_w2048, state_wkv, state_shift,
           norm1_g, w_in, q_norm_g, k_norm_g, mu_shift, w0, w2, a0, a2, g2, k_k, k_a, r_k,
           ln_x_w, ln_x_b, w_pa, w_pb, w_o, norm2_g, router_w, router_b, w_gu, b_gu, w_down, b_down):
    bp, seq, _ = x_prompt.shape
    bd, t_s, _ = x_sample.shape
    assert t_s == 1
    rw = dict(mu_shift=mu_shift, w0=w0, w2=w2, a0=a0, a2=a2, g2=g2, k_k=k_k, k_a=k_a, r_k=r_k,
              ln_x_w=ln_x_w, ln_x_b=ln_x_b)
    wb = dict(
        w_pa=w_pa.astype(BF16), w_pb=w_pb.astype(BF16), w_o=w_o.astype(BF16),
        norm2_g=norm2_g.reshape(1, D_MODEL),
        router_w=jnp.pad(router_w, ((0, 0), (0, LOGIT_PAD - N_EXPERTS))),
        router_b=jnp.pad(router_b, (0, LOGIT_PAD - N_EXPERTS)).reshape(1, LOGIT_PAD),
        w_gu=w_gu.astype(BF16), b_gu=b_gu.reshape(N_EXPERTS, 1, 2 * D_FF),
        w_down=w_down.astype(BF16), b_down=b_down.reshape(N_EXPERTS, 1, D_MODEL),
    )
    w_in_b = w_in.astype(BF16)
    mult = jnp.stack([jnp.tile(q_norm_g[c // 3] * (HEAD_DIM ** -0.5) if c % 3 == 0 else
                               (k_norm_g[c // 3] if c % 3 == 1 else jnp.ones((HEAD_DIM,), F32)),
                               HEADS_PER_GROUP) for c in range(A_COLS // COL_CHUNK)])
    bd_head = _seg_ones(COL_CHUNK, HEAD_DIM, 1.0 / HEAD_DIM)

    xp2 = x_prompt.reshape(bp * seq, D_MODEL)
    pa, ps, pg = _in_proj(xp2, norm1_g, w_in_b, mult, bd_head, tm=256)
    pa3 = pa.reshape(bp, seq, A_COLS)
    stats = []
    kv_p = []
    for g in range(N_GROUPS):
        stats += _attn_prompt(pa3, g)
        keep = min(GROUP_WINDOWS[g], seq)
        lo = g * 3 * A_WIDTH + A_WIDTH
        kv_p.append(pa3[:, seq - keep:, lo:lo + 2 * A_WIDTH].reshape(bp, keep, 2, HEADS_PER_GROUP, HEAD_DIM))
    ps3 = ps.reshape(bp, seq, SHIFT_WIDTH)
    streams = _rwkv_prep(ps3, jnp.zeros((bp, 1, SHIFT_WIDTH), F32), rw, tm=512)
    o_b, wkv_p = _rwkv_scan(streams, rw, chunk=SCAN_CHUNK)
    h, hn, lg = _finish(xp2, stats, o_b.reshape(bp * seq, B_WIDTH), pg, wb, tm=256)
    y_prompt = _moe(h, hn, lg, wb).reshape(bp, seq, D_MODEL)
    shift_p = ps3[:, -1:]

    caches = (cache_kv_w128, cache_kv_w512, cache_kv_w2048)
    xs2 = x_sample.reshape(bd, D_MODEL)
    pa_s, ps_s, pg_s = _in_proj(xs2, norm1_g, w_in_b, mult, bd_head, tm=bd)
    caches_t = [jnp.transpose(c.astype(F32), (0, 2, 3, 4, 1)) for c in caches]
    *kv_t, oa_t = _sample_cache(pa_s.T, caches_t)
    kv_s = [jnp.transpose(t, (0, 4, 1, 2, 3)).astype(c.dtype) for t, c in zip(kv_t, caches)]
    streams_t = _rwkv_prep_t(ps_s, state_shift.reshape(bd, SHIFT_WIDTH).astype(F32), rw)
    ob_t, s_t = _rwkv_step(streams_t, jnp.transpose(state_wkv.astype(F32), (1, 2, 3, 0)), rw)
    wkv_s = jnp.transpose(s_t, (3, 0, 1, 2))
    h_s, hn_s, lg_s = _finish(xs2, [oa_t.T], ob_t.T, pg_s, wb, tm=bd)
    y_sample = _moe(h_s, hn_s, lg_s, wb).reshape(bd, 1, D_MODEL)
    shift_s = ps_s.reshape(bd, 1, SHIFT_WIDTH)

    return (y_prompt, y_sample, kv_p[0], kv_p[1], kv_p[2], wkv_p.astype(state_wkv.dtype), shift_p,
            kv_s[0], kv_s[1], kv_s[2], wkv_s.astype(state_wkv.dtype), shift_s.astype(state_shift.dtype))
```

```python
import functools

import jax
import jax.numpy as jnp
from jax import lax
from jax.experimental import pallas as pl
from jax.experimental.pallas import tpu as pltpu

F32 = jnp.float32
BF16 = jnp.bfloat16

D_MODEL = 1024
GROUP_WINDOWS = (128, 512, 2048)
GROUP_DILATIONS = (1, 4, 16)
N_GROUPS = 3
HEADS_PER_GROUP = 4
HEAD_DIM = 64
A_WIDTH = HEADS_PER_GROUP * HEAD_DIM
A_COLS = N_GROUPS * 3 * A_WIDTH
RWKV_HEADS = 8
RWKV_HEAD_DIM = 64
B_WIDTH = RWKV_HEADS * RWKV_HEAD_DIM
DECAY_LORA = 64
AAA_LORA = 64
GATE_LORA = 128
SHIFT_WIDTH = 3 * B_WIDTH + DECAY_LORA + AAA_LORA + GATE_LORA
D_IN = A_COLS + SHIFT_WIDTH + 2 * D_MODEL
N_EXPERTS = 32
TOP_K = 4
D_FF = D_MODEL
SWIGLU_LIMIT = 7.0
SWIGLU_ALPHA = 1.702
MOE_BLOCK = 256
RMS_EPS = 1e-6
GN_EPS = 64e-5

N_BACK = 128
COL_CHUNK = 256
SCAN_CHUNK = 64
LANE_TILE = 128
LOGIT_PAD = LANE_TILE
COMBINE_TOKENS = 128
VMEM_LIMIT = 48 * 1024 * 1024


def _cparams(*sem):
    return pltpu.CompilerParams(dimension_semantics=sem, vmem_limit_bytes=VMEM_LIMIT)


def _sigmoid(x):
    return 1.0 / (1.0 + jnp.exp(-x))


def _split(x):
    hi = x.astype(BF16)
    lo = (x - hi.astype(F32)).astype(BF16)
    return hi, lo


def _dg(a, b, dims):
    return lax.dot_general(a, b, (dims, ((), ())), preferred_element_type=F32)


NN = ((1,), (0,))
NT = ((1,), (1,))
TN = ((0,), (0,))


def _dot1(a, b, dims=NN):
    return _dg(a.astype(BF16), b.astype(BF16), dims)


def _dot3(a, b, dims=NN):
    ah, al = _split(a)
    bh, bl = _split(b)
    return _dg(ah, bh, dims) + _dg(al, bh, dims) + _dg(ah, bl, dims)


def _split3(x):
    hi = x.astype(BF16)
    r1 = x - hi.astype(F32)
    mid = r1.astype(BF16)
    lo = (r1 - mid.astype(F32)).astype(BF16)
    return hi, mid, lo


def _dot_exact_rhs(a, b_bf16):
    return sum(_dg(t, b_bf16, NN) for t in _split3(a))


def _dot_exact_lhs(l_bf16, x):
    return sum(_dg(l_bf16, t, NN) for t in _split3(x))


def _seg_ones(width, seg, scale):
    i = jnp.arange(width)[:, None] // seg
    j = jnp.arange(width)[None, :] // seg
    return jnp.where(i == j, scale, 0.0).astype(BF16)


def _in_proj_kernel(x_ref, g1_ref, w_ref, mult_ref, bd_ref, pa_ref, ps_ref, pg_ref):
    x = x_ref[...]
    xn = x * lax.rsqrt(jnp.mean(x * x, axis=-1, keepdims=True) + RMS_EPS) * g1_ref[...]
    xb = xn.astype(BF16)
    bd = bd_ref[...]
    for c in range(A_COLS // COL_CHUNK):
        acc = _dg(xb, w_ref[:, c * COL_CHUNK:(c + 1) * COL_CHUNK], NN)
        if c % 3 != 2:
            ms = _dot_exact_rhs(acc * acc, bd)
            acc = acc * lax.rsqrt(ms + RMS_EPS) * mult_ref[c:c + 1, :]
        pa_ref[:, c * COL_CHUNK:(c + 1) * COL_CHUNK] = acc
    for c in range(SHIFT_WIDTH // COL_CHUNK):
        lo = A_COLS + c * COL_CHUNK
        ps_ref[:, c * COL_CHUNK:(c + 1) * COL_CHUNK] = _dg(xb, w_ref[:, lo:lo + COL_CHUNK], NN)
    for c in range(2 * D_MODEL // COL_CHUNK):
        lo = A_COLS + SHIFT_WIDTH + c * COL_CHUNK
        pg_ref[:, c * COL_CHUNK:(c + 1) * COL_CHUNK] = _sigmoid(_dg(xb, w_ref[:, lo:lo + COL_CHUNK], NN)).astype(BF16)


def _in_proj(x2, norm1_g, w_in_b, mult, bd, tm):
    n = x2.shape[0]
    const = lambda i: (0, 0)
    return pl.pallas_call(
        _in_proj_kernel,
        grid=(n // tm,),
        in_specs=[
            pl.BlockSpec((tm, D_MODEL), lambda i: (i, 0)),
            pl.BlockSpec((1, D_MODEL), const),
            pl.BlockSpec((D_MODEL, D_IN), const, pipeline_mode=pl.Buffered(1)),
            pl.BlockSpec(mult.shape, const),
            pl.BlockSpec(bd.shape, const),
        ],
        out_specs=[
            pl.BlockSpec((tm, A_COLS), lambda i: (i, 0)),
            pl.BlockSpec((tm, SHIFT_WIDTH), lambda i: (i, 0)),
            pl.BlockSpec((tm, 2 * D_MODEL), lambda i: (i, 0)),
        ],
        out_shape=[
            jax.ShapeDtypeStruct((n, A_COLS), F32),
            jax.ShapeDtypeStruct((n, SHIFT_WIDTH), F32),
            jax.ShapeDtypeStruct((n, 2 * D_MODEL), BF16),
        ],
        compiler_params=_cparams("parallel"),
        name="in_proj",
    )(x2, norm1_g.reshape(1, D_MODEL), w_in_b, mult, bd)


def _attn_kernel(*refs, dil):
    halves = A_WIDTH // LANE_TILE
    q_refs, kp_refs, kc_refs, vp_refs, vc_refs = (refs[i * halves:(i + 1) * halves] for i in range(5))
    out_refs = refs[5 * halves:5 * halves + 3]
    qs, ks, vs = refs[5 * halves + 3:5 * halves + 6]
    stat_s = refs[5 * halves + 6:5 * halves + 9]
    stage = refs[5 * halves + 9:]
    n = pl.program_id(1)
    qi = lax.broadcasted_iota(jnp.int32, (N_BACK, 2 * N_BACK), 0)
    kj = lax.broadcasted_iota(jnp.int32, (N_BACK, 2 * N_BACK), 1)
    first_prev = jnp.where(n > 0, 0, N_BACK)
    lo = jnp.where(kj < N_BACK, qi + first_prev, N_BACK)
    hi = jnp.where(kj < N_BACK, N_BACK - 1, qi + N_BACK)
    mask = jnp.logical_and(kj >= lo, kj <= hi)

    def stream(r, carry):
        rows = pl.ds(r, N_BACK, stride=dil) if dil > 1 else slice(None)
        for c in range(halves):
            lanes = slice(c * LANE_TILE, (c + 1) * LANE_TILE)
            qs[:, lanes] = q_refs[c][rows, :]
            ks[0:N_BACK, lanes] = kp_refs[c][rows, :]
            ks[N_BACK:, lanes] = kc_refs[c][rows, :]
            vs[0:N_BACK, lanes] = vp_refs[c][rows, :]
            vs[N_BACK:, lanes] = vc_refs[c][rows, :]
        sls = [slice(h * HEAD_DIM, (h + 1) * HEAD_DIM) for h in range(HEADS_PER_GROUP)]
        s = [jnp.where(mask, _dot1(qs[:, sl], ks[:, sl], NT), -jnp.inf) for sl in sls]
        m = [jnp.max(t, axis=-1, keepdims=True) for t in s]
        p = [jnp.exp(t - mm) for t, mm in zip(s, m)]
        den = [jnp.sum(t, axis=-1, keepdims=True) for t in p]
        num = [_dot1(t, vs[:, sl]) for t, sl in zip(p, sls)]
        for h, sl in enumerate(sls):
            stat_s[0][:, sl] = num[h]
            stat_s[1][:, sl] = jnp.broadcast_to(den[h], (N_BACK, HEAD_DIM))
            stat_s[2][:, sl] = jnp.broadcast_to(m[h], (N_BACK, HEAD_DIM))
        for i in range(3):
            if dil > 1:
                for c in range(halves):
                    stage[i * halves + c][rows, :] = stat_s[i][:, c * LANE_TILE:(c + 1) * LANE_TILE]
            else:
                out_refs[i][...] = stat_s[i][...]
        return carry

    if dil > 1:
        lax.fori_loop(0, dil, stream, 0)
        for i in range(3):
            for c in range(halves):
                out_refs[i][:, c * LANE_TILE:(c + 1) * LANE_TILE] = stage[i * halves + c][...]
    else:
        stream(0, 0)


def _attn_prompt(pa3, g):
    bsz, seq, _ = pa3.shape
    dil = GROUP_DILATIONS[g]
    rows = N_BACK * dil
    nb = seq // rows
    halves = A_WIDTH // LANE_TILE

    def specs(slab, prev):
        def one(c):
            col = (3 * g + slab) * halves + c
            if prev:
                return pl.BlockSpec((None, rows, LANE_TILE), lambda b, n: (b, jnp.maximum(n - 1, 0), col))
            return pl.BlockSpec((None, rows, LANE_TILE), lambda b, n: (b, n, col))
        return [one(c) for c in range(halves)]

    in_specs = specs(0, False) + specs(1, True) + specs(1, False) + specs(2, True) + specs(2, False)
    out_spec = pl.BlockSpec((None, rows, A_WIDTH), lambda b, n: (b, n, 0))
    out_sds = jax.ShapeDtypeStruct((bsz, seq, A_WIDTH), F32)
    scratch = [pltpu.VMEM((N_BACK, A_WIDTH), F32), pltpu.VMEM((2 * N_BACK, A_WIDTH), F32),
               pltpu.VMEM((2 * N_BACK, A_WIDTH), F32)] + [pltpu.VMEM((N_BACK, A_WIDTH), F32)] * 3
    if dil > 1:
        scratch += [pltpu.VMEM((rows, LANE_TILE), F32)] * (3 * halves)
    outs = pl.pallas_call(
        functools.partial(_attn_kernel, dil=dil),
        grid=(bsz, nb),
        in_specs=in_specs,
        out_specs=[out_spec] * 3,
        out_shape=[out_sds] * 3,
        scratch_shapes=scratch,
        compiler_params=_cparams("parallel", "arbitrary"),
        name=f"attn_prompt_g{g}",
    )(*([pa3] * len(in_specs)))
    return [o.reshape(bsz * seq, A_WIDTH) for o in outs]


def _sample_cache_kernel(pat_ref, c0_ref, c1_ref, c2_ref, o0_ref, o1_ref, o2_ref, oa_ref):
    b = pl.program_id(0)
    n_b = pat_ref.shape[1]

    @pl.when(b == 0)
    def _():
        oa_ref[...] = jnp.zeros_like(oa_ref)

    lane_b = lax.broadcasted_iota(jnp.int32, pat_ref.shape, 1) == b
    col = jnp.sum(jnp.where(lane_b, pat_ref[...], 0.0), axis=1, keepdims=True)
    c_refs = (c0_ref, c1_ref, c2_ref)
    o_refs = (o0_ref, o1_ref, o2_ref)
    stats = []
    for g in range(N_GROUPS):
        dil = GROUP_DILATIONS[g]
        win = GROUP_WINDOWS[g]
        base = g * 3 * A_WIDTH
        lane = lax.broadcasted_iota(jnp.int32, (HEAD_DIM, win), 1)
        last = lane == win - 1
        live = lax.broadcasted_iota(jnp.int32, (1, win), 1) % dil == 0
        for h in range(HEADS_PER_GROUP):
            o = h * HEAD_DIM
            q = col[base + o:base + o + HEAD_DIM]
            kn = col[base + A_WIDTH + o:base + A_WIDTH + o + HEAD_DIM]
            vn = col[base + 2 * A_WIDTH + o:base + 2 * A_WIDTH + o + HEAD_DIM]
            kt = c_refs[g][0, 0, h]
            vt = c_refs[g][0, 1, h]
            o_refs[g][0, 0, h] = jnp.where(last, kn, pltpu.roll(kt, win - 1, axis=1))
            o_refs[g][0, 1, h] = jnp.where(last, vn, pltpu.roll(vt, win - 1, axis=1))
            s_c = jnp.where(live, jnp.sum(kt * q, axis=0, keepdims=True), -jnp.inf)
            s_n = jnp.sum(kn * q, axis=0, keepdims=True)
            m = jnp.maximum(jnp.max(s_c, axis=1, keepdims=True), s_n)
            p_c = jnp.exp(s_c - m)
            p_n = jnp.exp(s_n - m)
            den = jnp.sum(p_c, axis=1, keepdims=True) + p_n
            num = jnp.sum(vt * p_c, axis=1, keepdims=True) + p_n * vn
            stats.append((num, den, m))
    outs = []
    for h in range(HEADS_PER_GROUP):
        per_g = [stats[g * HEADS_PER_GROUP + h] for g in range(N_GROUPS)]
        mx = functools.reduce(jnp.maximum, [m for _, _, m in per_g])
        num = sum(n_ * jnp.exp(m - mx) for n_, _, m in per_g)
        den = sum(d_ * jnp.exp(m - mx) for _, d_, m in per_g)
        outs.append(num / den)
    o_col = jnp.concatenate(outs, axis=0)
    lane_o = lax.broadcasted_iota(jnp.int32, (A_WIDTH, n_b), 1) == b
    oa_ref[...] = jnp.where(lane_o, o_col, oa_ref[...])


def _sample_cache(pa_t, caches_t):
    bd = pa_t.shape[1]
    cspecs = [pl.BlockSpec((1, 2, HEADS_PER_GROUP, HEAD_DIM, w), lambda b: (b, 0, 0, 0, 0)) for w in GROUP_WINDOWS]
    return pl.pallas_call(
        _sample_cache_kernel,
        grid=(bd,),
        in_specs=[pl.BlockSpec(pa_t.shape, lambda b: (0, 0))] + cspecs,
        out_specs=cspecs + [pl.BlockSpec((A_WIDTH, bd), lambda b: (0, 0))],
        out_shape=[jax.ShapeDtypeStruct(c.shape, c.dtype) for c in caches_t]
                  + [jax.ShapeDtypeStruct((A_WIDTH, bd), F32)],
        compiler_params=_cparams("arbitrary"),
        name="sample_cache",
    )(pa_t, *caches_t)


def _rwkv_prep_math(pf, shifted, mu_ref, w0_ref, w2_ref, a0_ref, a2_ref, g2_ref, kk_ref, ka_ref, bd_ref):
    z = pf + mu_ref[...] * (shifted - pf)
    c1, c2, c3 = B_WIDTH, 2 * B_WIDTH, 3 * B_WIDTH
    r, k, v = z[:, :c1], z[:, c1:c2], z[:, c2:c3]
    wl = z[:, c3:c3 + DECAY_LORA]
    al = z[:, c3 + DECAY_LORA:c3 + DECAY_LORA + AAA_LORA]
    gl = z[:, c3 + DECAY_LORA + AAA_LORA:]
    xw = w0_ref[...] + _dot3(jnp.tanh(wl), w2_ref[...])
    w_log = -(jnp.maximum(-xw, 0.0) + jnp.log(1.0 + jnp.exp(-jnp.abs(xw)))) - 0.5
    lw = -jnp.exp(w_log)
    a = _sigmoid(a0_ref[...] + _dot3(al, a2_ref[...]))
    g = _dot3(_sigmoid(gl), g2_ref[...])
    kk = k * kk_ref[...]
    bd = bd_ref[...]
    kks = []
    for c in range(B_WIDTH // 128):
        kc = kk[:, c * 128:(c + 1) * 128]
        nrm = jnp.sqrt(_dot_exact_rhs(kc * kc, bd))
        kks.append(kc / jnp.maximum(nrm, 1e-12))
    kkn = jnp.concatenate(kks, axis=1)
    return r, lw, k * (1.0 + (a - 1.0) * ka_ref[...]), v, -kkn, kkn * a, g


def _rwkv_prep_kernel(ps_ref, pv_ref, p0_ref, *refs):
    param_refs, out_refs = refs[:9], refs[9:]
    i = pl.program_id(1)
    pf = ps_ref[...]
    prev_row = jnp.where(i == 0, p0_ref[...], pv_ref[7:8, :])
    row = lax.broadcasted_iota(jnp.int32, pf.shape, 0)
    shifted = jnp.where(row == 0, prev_row, pltpu.roll(pf, 1, axis=0))
    for o_ref, val in zip(out_refs, _rwkv_prep_math(pf, shifted, *param_refs)):
        o_ref[...] = val


def _rwkv_prep_t_kernel(ps_ref, prev_ref, *refs):
    param_refs, out_refs = refs[:9], refs[9:]
    for o_ref, val in zip(out_refs, _rwkv_prep_math(ps_ref[...], prev_ref[...], *param_refs)):
        o_ref[...] = val.T


def _prep_params(p):
    row = lambda a: a.reshape(1, -1)
    return [row(p["mu_shift"]), row(p["w0"]), p["w2"], row(p["a0"]), p["a2"], p["g2"],
            row(p["k_k"]), row(p["k_a"]), _seg_ones(128, RWKV_HEAD_DIM, 1.0)]


def _rwkv_prep(ps3, prev0, p, tm):
    bsz, t, _ = ps3.shape
    const = lambda b, i: (0, 0)
    params = _prep_params(p)
    out_spec = pl.BlockSpec((None, tm, B_WIDTH), lambda b, i: (b, i, 0))
    out_sds = jax.ShapeDtypeStruct((bsz, t, B_WIDTH), F32)
    return pl.pallas_call(
        _rwkv_prep_kernel,
        grid=(bsz, t // tm),
        in_specs=[
            pl.BlockSpec((None, tm, SHIFT_WIDTH), lambda b, i: (b, i, 0)),
            pl.BlockSpec((None, 8, SHIFT_WIDTH), lambda b, i: (b, jnp.maximum(i * (tm // 8) - 1, 0), 0)),
            pl.BlockSpec((None, 1, SHIFT_WIDTH), lambda b, i: (b, 0, 0)),
        ] + [pl.BlockSpec(a.shape, const) for a in params],
        out_specs=[out_spec] * 7,
        out_shape=[out_sds] * 7,
        compiler_params=_cparams("parallel", "arbitrary"),
        name="rwkv_prep",
    )(ps3, ps3, prev0, *params)


def _rwkv_prep_t(ps2, prev2, p):
    bd = ps2.shape[0]
    params = _prep_params(p)
    full = lambda a: pl.BlockSpec(a.shape, lambda i: (0, 0))
    out_sds = jax.ShapeDtypeStruct((B_WIDTH, bd), F32)
    return pl.pallas_call(
        _rwkv_prep_t_kernel,
        grid=(1,),
        in_specs=[full(ps2), full(prev2)] + [full(a) for a in params],
        out_specs=[pl.BlockSpec((B_WIDTH, bd), lambda i: (0, 0))] * 7,
        out_shape=[out_sds] * 7,
        compiler_params=_cparams("arbitrary"),
        name="rwkv_prep_t",
    )(ps2, prev2, *params)


def _rwkv_scan_kernel(r_ref, lw_ref, k_ref, v_ref, a_ref, b_ref, g_ref, rk_ref, lnw_ref, lnb_ref,
                      tril_ref, o_ref, s_ref):
    chunk = r_ref.shape[0]

    @pl.when(pl.program_id(1) == 0)
    def _():
        s_ref[...] = jnp.zeros_like(s_ref)

    lw = lw_ref[...]
    r, k, v = r_ref[...], k_ref[...], v_ref[...]
    cum = _dot_exact_lhs(tril_ref[...], lw)
    e_in = jnp.exp(cum)
    e_inv = jnp.exp(-cum)
    at = a_ref[...] * jnp.exp(cum - lw)
    rt = r * e_in
    bt = b_ref[...] * e_inv
    kt = k * e_inv
    g_last = e_in[chunk - 1:chunk, :]

    ti = lax.broadcasted_iota(jnp.int32, (chunk, 2 * chunk), 0)
    si = lax.broadcasted_iota(jnp.int32, (chunk, 2 * chunk), 1)
    si = jnp.where(si >= chunk, si - chunk, si)
    strict = si < ti
    incl = si <= ti
    eye = jnp.where(lax.broadcasted_iota(jnp.int32, (chunk, chunk), 0)
                    == lax.broadcasted_iota(jnp.int32, (chunk, chunk), 1), 1.0, 0.0).astype(F32)
    n_sq = max(chunk.bit_length() - 2, 0)

    heads = range(RWKV_HEADS)
    sls = [slice(h * RWKV_HEAD_DIM, (h + 1) * RWKV_HEAD_DIM) for h in heads]
    ar = [jnp.concatenate([at[:, sl], rt[:, sl]], axis=0) for sl in sls]
    bk = [jnp.concatenate([bt[:, sl], kt[:, sl]], axis=0) for sl in sls]
    s_old = [s_ref[h] for h in heads]
    m_all = [_dot1(ar[h], bk[h], NT) for h in heads]
    gs = [_dot1(ar[h], s_old[h], NT) for h in heads]
    a_m = [jnp.where(strict, m_all[h][:chunk], 0.0) for h in heads]
    r_m = [jnp.where(incl, m_all[h][chunk:], 0.0) for h in heads]
    pw = [a_m[h][:, :chunk] for h in heads]
    x = [eye + pw[h] for h in heads]
    for _ in range(n_sq):
        pw = [_dot1(pw[h], pw[h]) for h in heads]
        x = [x[h] + _dot1(x[h], pw[h]) for h in heads]
    v_h = [v[:, sl] for sl in sls]
    av = [_dot1(a_m[h][:, chunk:], v_h[h]) for h in heads]
    u = [_dot1(x[h], gs[h][:chunk] + av[h]) for h in heads]
    uv = [jnp.concatenate([u[h], v_h[h]], axis=0) for h in heads]
    y = [gs[h][chunk:] + _dot1(r_m[h], uv[h]) for h in heads]
    for h in heads:
        s_ref[h] = (s_old[h] + _dot3(uv[h], bk[h], TN)) * g_last[:, sls[h]]
    for h, sl in enumerate(sls):
        mu = jnp.mean(y[h], axis=-1, keepdims=True)
        var = jnp.mean(jnp.square(y[h] - mu), axis=-1, keepdims=True)
        yn = (y[h] - mu) * lax.rsqrt(var + GN_EPS) * lnw_ref[:, sl] + lnb_ref[:, sl]
        bonus = jnp.sum(r[:, sl] * k[:, sl] * rk_ref[:, sl], axis=-1, keepdims=True) * v_h[h]
        o_ref[:, sl] = (yn + bonus) * g_ref[:, sl]


def _rwkv_scan(streams, p, chunk):
    bsz, t, _ = streams[0].shape
    row = lambda a: a.reshape(1, B_WIDTH)
    tril = (jnp.arange(chunk)[None, :] <= jnp.arange(chunk)[:, None]).astype(BF16)
    const = lambda b, c: (0, 0)
    blk = pl.BlockSpec((None, chunk, B_WIDTH), lambda b, c: (b, c, 0))
    st_shape = (bsz, RWKV_HEADS, RWKV_HEAD_DIM, RWKV_HEAD_DIM)
    st = pl.BlockSpec((None,) + st_shape[1:], lambda b, c: (b, 0, 0, 0))
    vec = pl.BlockSpec((1, B_WIDTH), const)
    return pl.pallas_call(
        _rwkv_scan_kernel,
        grid=(bsz, t // chunk),
        in_specs=[blk] * 7 + [vec, vec, vec, pl.BlockSpec((chunk, chunk), const)],
        out_specs=[blk, st],
        out_shape=[jax.ShapeDtypeStruct((bsz, t, B_WIDTH), F32), jax.ShapeDtypeStruct(st_shape, F32)],
        compiler_params=_cparams("parallel", "arbitrary"),
        name="rwkv_scan",
    )(*streams, row(p["r_k"]), row(p["ln_x_w"]), row(p["ln_x_b"]), tril)


def _rwkv_step_kernel(r_ref, lw_ref, k_ref, v_ref, a_ref, b_ref, g_ref, rk_ref, lnw_ref, lnb_ref, s_ref,
                      o_ref, so_ref, y_ref):
    w = jnp.exp(lw_ref[...])
    a, b, k, r = a_ref[...], b_ref[...], k_ref[...], r_ref[...]

    def value_row(i, carry):
        s = s_ref[i]
        sa = jnp.sum(s * a, axis=0, keepdims=True)
        s_new = s * w + sa * b + v_ref[pl.ds(i, 1), :] * k
        so_ref[i] = s_new
        y_ref[pl.ds(i, 1), :] = jnp.sum(s_new * r, axis=0, keepdims=True)
        return carry

    lax.fori_loop(0, RWKV_HEAD_DIM, value_row, 0, unroll=4)
    y = y_ref[...]
    mu = jnp.mean(y, axis=0, keepdims=True)
    var = jnp.mean(jnp.square(y - mu), axis=0, keepdims=True)
    yn = (y - mu) * lax.rsqrt(var + GN_EPS) * lnw_ref[...] + lnb_ref[...]
    bonus = jnp.sum(r_ref[...] * k_ref[...] * rk_ref[...], axis=0, keepdims=True) * v_ref[...]
    o_ref[...] = (yn + bonus) * g_ref[...]


def _rwkv_step(streams_t, s_t, p):
    bd = s_t.shape[-1]
    n = RWKV_HEAD_DIM
    col = lambda a: a.reshape(B_WIDTH, 1)
    vec = pl.BlockSpec((n, bd), lambda h: (h, 0))
    par = pl.BlockSpec((n, 1), lambda h: (h, 0))
    st = pl.BlockSpec((None, n, n, bd), lambda h: (h, 0, 0, 0))
    return pl.pallas_call(
        _rwkv_step_kernel,
        grid=(RWKV_HEADS,),
        in_specs=[vec] * 7 + [par] * 3 + [st],
        out_specs=[vec, st],
        out_shape=[jax.ShapeDtypeStruct((B_WIDTH, bd), F32), jax.ShapeDtypeStruct(s_t.shape, F32)],
        scratch_shapes=[pltpu.VMEM((n, bd), F32)],
        compiler_params=_cparams("parallel"),
        name="rwkv_step",
    )(*streams_t, col(p["r_k"]), col(p["ln_x_w"]), col(p["ln_x_b"]), s_t)


def _finish_kernel(x_ref, *refs, n_stats):
    stat_refs = refs[:n_stats]
    ob_ref, g_ref, wpa_ref, wpb_ref, wo_ref, g2_ref, rw_ref, rb_ref, h_ref, hn_ref, lg_ref = refs[n_stats:]
    if n_stats == 1:
        o_a = stat_refs[0][...]
    else:
        nums, dens, maxes = stat_refs[0::3], stat_refs[1::3], stat_refs[2::3]
        mx = functools.reduce(jnp.maximum, [m[...] for m in maxes])
        es = [jnp.exp(m[...] - mx) for m in maxes]
        num = sum(n_[...] * e for n_, e in zip(nums, es))
        den = sum(d_[...] * e for d_, e in zip(dens, es))
        o_a = num / den
    br_a = _dot1(o_a, wpa_ref[...])
    br_b = _dot1(ob_ref[...], wpb_ref[...])
    mix = g_ref[:, :D_MODEL] * br_a + g_ref[:, D_MODEL:] * br_b
    h = x_ref[...] + _dot1(mix, wo_ref[...])
    h_ref[...] = h
    hn = h * lax.rsqrt(jnp.mean(h * h, axis=-1, keepdims=True) + RMS_EPS) * g2_ref[...]
    hn_ref[...] = hn
    lg_ref[...] = _dot3(hn, rw_ref[...]) + rb_ref[...]


def _finish(x2, stats, o_b, gates, wb, tm):
    n = x2.shape[0]
    const = lambda i: (0, 0)
    rows = lambda w: pl.BlockSpec((tm, w), lambda i: (i, 0))
    params = [wb["w_pa"], wb["w_pb"], wb["w_o"], wb["norm2_g"], wb["router_w"], wb["router_b"]]
    return pl.pallas_call(
        functools.partial(_finish_kernel, n_stats=len(stats)),
        grid=(n // tm,),
        in_specs=[rows(D_MODEL)] + [rows(A_WIDTH)] * len(stats) + [rows(B_WIDTH), rows(2 * D_MODEL)]
                 + [pl.BlockSpec(a.shape, const) for a in params],
        out_specs=[rows(D_MODEL), rows(D_MODEL), rows(LOGIT_PAD)],
        out_shape=[jax.ShapeDtypeStruct((n, D_MODEL), F32), jax.ShapeDtypeStruct((n, D_MODEL), F32),
                   jax.ShapeDtypeStruct((n, LOGIT_PAD), F32)],
        compiler_params=_cparams("parallel"),
        name="finish",
    )(x2, *stats, o_b, gates, *params)


def _row(ref, i):
    return ref.at[pl.ds(i, 1)]


def _moe_kernel(blk_e_ref, nused_ref, tok_ref, tok_next_ref, hn_hbm, wgu_ref, bgu_ref, wd_ref, bdn_ref,
                ys_ref, xbuf_a, xbuf_b, fence_buf, sem, fence_sem):
    i = pl.program_id(0)
    nused = nused_ref[0]
    n_chunks = D_FF // COL_CHUNK
    bufs = (xbuf_a, xbuf_b)

    def issue(idx_ref, s):
        for r in range(MOE_BLOCK):
            pltpu.make_async_copy(_row(hn_hbm, idx_ref[0, 0, r]), _row(bufs[s], r), sem.at[s]).start(priority=r % 2)

    def drain(s):
        for _ in range(MOE_BLOCK):
            pltpu.make_async_copy(_row(hn_hbm, 0), _row(bufs[s], 0), sem.at[s]).wait()

    @pl.when(jnp.logical_and(i == 0, nused > 0))
    def _():
        issue(tok_ref, 0)

    def block(s):
        drain(s)
        x = bufs[s][...].astype(BF16)
        fence_buf[0] = jnp.zeros(fence_buf.shape[1:], F32)
        fence = pltpu.make_async_copy(fence_buf.at[0], fence_buf.at[1], fence_sem)
        fence.start()
        issue(tok_next_ref, 1 - s)
        acc = jnp.zeros((MOE_BLOCK, D_MODEL), F32)
        for c in range(n_chunks):
            if c == n_chunks // 2:
                fence.wait()
            lo = c * COL_CHUNK
            gate = _dg(x, wgu_ref[:, lo:lo + COL_CHUNK], NN) + bgu_ref[:, lo:lo + COL_CHUNK]
            up = _dg(x, wgu_ref[:, D_FF + lo:D_FF + lo + COL_CHUNK], NN) + bgu_ref[:, D_FF + lo:D_FF + lo + COL_CHUNK]
            gate = jnp.minimum(gate, SWIGLU_LIMIT)
            up = jnp.clip(up, -SWIGLU_LIMIT, SWIGLU_LIMIT)
            hid = (up + 1.0) * gate * _sigmoid(gate * SWIGLU_ALPHA)
            acc = acc + _dg(hid.astype(BF16), wd_ref[lo:lo + COL_CHUNK, :], NN)
        ys_ref[...] = acc + bdn_ref[...]

        @pl.when(i + 1 >= nused)
        def _():
            drain(1 - s)

    for s in range(2):
        pl.when(jnp.logical_and(i < nused, i % 2 == s))(functools.partial(block, s))

    @pl.when(i >= nused)
    def _():
        ys_ref[...] = jnp.zeros_like(ys_ref)


def _moe_blocks(hn, row_tok, blk_e, nused, wb):
    n_blocks = row_tok.shape[0]
    last = n_blocks - 1
    grid_spec = pltpu.PrefetchScalarGridSpec(
        num_scalar_prefetch=2,
        grid=(n_blocks,),
        in_specs=[
            pl.BlockSpec((1, 1, MOE_BLOCK), lambda i, be, nu: (i, 0, 0), memory_space=pltpu.SMEM),
            pl.BlockSpec((1, 1, MOE_BLOCK), lambda i, be, nu: (jnp.minimum(i + 1, last), 0, 0),
                         memory_space=pltpu.SMEM),
            pl.BlockSpec(memory_space=pl.ANY),
            pl.BlockSpec((None, D_MODEL, 2 * D_FF), lambda i, be, nu: (be[i], 0, 0)),
            pl.BlockSpec((None, 1, 2 * D_FF), lambda i, be, nu: (be[i], 0, 0)),
            pl.BlockSpec((None, D_FF, D_MODEL), lambda i, be, nu: (be[i], 0, 0)),
            pl.BlockSpec((None, 1, D_MODEL), lambda i, be, nu: (be[i], 0, 0)),
        ],
        out_specs=pl.BlockSpec((MOE_BLOCK, D_MODEL), lambda i, be, nu: (i, 0)),
        scratch_shapes=[pltpu.VMEM((MOE_BLOCK, D_MODEL), F32), pltpu.VMEM((MOE_BLOCK, D_MODEL), F32),
                        pltpu.VMEM((2, 8, LANE_TILE), F32), pltpu.SemaphoreType.DMA((2,)),
                        pltpu.SemaphoreType.DMA(())],
    )
    return pl.pallas_call(
        _moe_kernel,
        grid_spec=grid_spec,
        out_shape=jax.ShapeDtypeStruct((n_blocks * MOE_BLOCK, D_MODEL), F32),
        compiler_params=_cparams("arbitrary"),
        name="moe_blocks",
    )(blk_e, nused, row_tok, row_tok, hn, wb["w_gu"], wb["b_gu"], wb["w_down"], wb["b_down"])


def _combine_kernel(pos_ref, pos_next_ref, h_ref, gate_ref, ys_hbm, y_ref, gbuf, sem):
    tm = h_ref.shape[0]
    rows = TOP_K * tm
    i = pl.program_id(0)
    n = pl.num_programs(0)
    slot = i % 2

    def issue(idx_ref, s):
        for kk in range(rows):
            pltpu.make_async_copy(_row(ys_hbm, idx_ref[0, 0, kk]), gbuf.at[s, pl.ds(kk, 1)], sem.at[s]).start(
                priority=kk % 2)

    @pl.when(i == 0)
    def _():
        issue(pos_ref, 0)

    @pl.when(i + 1 < n)
    def _():
        issue(pos_next_ref, 1 - slot)

    for kk in range(rows):
        pltpu.make_async_copy(_row(ys_hbm, 0), gbuf.at[slot, pl.ds(0, 1)], sem.at[slot]).wait()
    gv = gate_ref[...]
    ff = gv[:, 0:1] * gbuf[slot, 0:tm, :]
    for j in range(1, TOP_K):
        ff = ff + gv[:, j:j + 1] * gbuf[slot, j * tm:(j + 1) * tm, :]
    y_ref[...] = h_ref[...] + ff


def _combine(h, gates_pad, pos, ys):
    n = h.shape[0]
    tm = min(COMBINE_TOKENS, n)
    steps = n // tm
    rows = TOP_K * tm
    pos3 = pos.reshape(steps, tm, TOP_K).transpose(0, 2, 1).reshape(steps, 1, rows)
    return pl.pallas_call(
        _combine_kernel,
        grid=(steps,),
        in_specs=[
            pl.BlockSpec((1, 1, rows), lambda i: (i, 0, 0), memory_space=pltpu.SMEM),
            pl.BlockSpec((1, 1, rows), lambda i: (jnp.minimum(i + 1, steps - 1), 0, 0), memory_space=pltpu.SMEM),
            pl.BlockSpec((tm, D_MODEL), lambda i: (i, 0)),
            pl.BlockSpec((tm, LOGIT_PAD), lambda i: (i, 0)),
            pl.BlockSpec(memory_space=pl.ANY),
        ],
        out_specs=pl.BlockSpec((tm, D_MODEL), lambda i: (i, 0)),
        out_shape=jax.ShapeDtypeStruct((n, D_MODEL), F32),
        scratch_shapes=[pltpu.VMEM((2, rows, D_MODEL), F32), pltpu.SemaphoreType.DMA((2,))],
        compiler_params=_cparams("arbitrary"),
        name="moe_combine",
    )(pos3, pos3, h, gates_pad, ys)


def _route(logits):
    n = logits.shape[0]
    nk = n * TOP_K
    top_v, top_e = lax.top_k(logits, TOP_K)
    gates = jax.nn.softmax(top_v, axis=-1)
    flat_e = top_e.reshape(nk)
    onehot = (flat_e[:, None] == jnp.arange(N_EXPERTS, dtype=top_e.dtype)[None, :]).astype(jnp.int32)
    csum = jnp.cumsum(onehot, axis=0)
    counts = csum[-1]
    starts = jnp.cumsum(counts) - counts
    pcounts = (counts + MOE_BLOCK - 1) // MOE_BLOCK * MOE_BLOCK
    pends = jnp.cumsum(pcounts)
    pstarts = pends - pcounts
    pos = jnp.sum(onehot * (csum - 1 + pstarts[None, :]), axis=1).astype(jnp.int32)
    n_blocks = -(-nk // MOE_BLOCK) + N_EXPERTS
    blk_start = jnp.arange(n_blocks, dtype=jnp.int32) * MOE_BLOCK
    blk_e = jnp.minimum(jnp.sum(blk_start[:, None] >= pends[None, :], axis=1), N_EXPERTS - 1).astype(jnp.int32)
    nused = (pends[-1:] // MOE_BLOCK).astype(jnp.int32)
    order = jnp.argsort(flat_e)
    e_row = jnp.repeat(blk_e, MOE_BLOCK)
    rank = jnp.arange(n_blocks * MOE_BLOCK, dtype=jnp.int32) - pstarts[e_row]
    src = jnp.minimum(starts[e_row] + rank, nk - 1)
    row_tok = jnp.where(rank < counts[e_row], order[src] // TOP_K, 0).astype(jnp.int32)
    gates_pad = jnp.pad(gates, ((0, 0), (0, LOGIT_PAD - TOP_K)))
    return pos, row_tok.reshape(n_blocks, 1, MOE_BLOCK), blk_e, nused, gates_pad


def _moe(h, hn, logits_pad, wb):
    pos, row_tok, blk_e, nused, gates_pad = _route(logits_pad[:, :N_EXPERTS])
    ys = _moe_blocks(hn, row_tok, blk_e, nused, wb)
    return _combine(h, gates_pad, pos, ys)


def kernel(x_prompt, x_sample, cache_kv_w128, cache_kv_w512, cache_kv_w2048, state_wkv, state_shift,
           norm1_g, w_in, q_norm_g, k_norm_g, mu_shift, w0, w2, a0, a2, g2, k_k, k_a, r_k,
           ln_x_w, ln_x_b, w_pa, w_pb, w_o, norm2_g, router_w, router_b, w_gu, b_gu, w_down, b_down):
    bp, seq, _ = x_prompt.shape
    bd, t_s, _ = x_sample.shape
    assert t_s == 1
    rw = dict(mu_shift=mu_shift, w0=w0, w2=w2, a0=a0, a2=a2, g2=g2, k_k=k_k, k_a=k_a, r_k=r_k,
              ln_x_w=ln_x_w, ln_x_b=ln_x_b)
    wb = dict(
        w_pa=w_pa.astype(BF16), w_pb=w_pb.astype(BF16), w_o=w_o.astype(BF16),
        norm2_g=norm2_g.reshape(1, D_MODEL),
        router_w=jnp.pad(router_w, ((0, 0), (0, LOGIT_PAD - N_EXPERTS))),
        router_b=jnp.pad(router_b, (0, LOGIT_PAD - N_EXPERTS)).reshape(1, LOGIT_PAD),
        w_gu=w_gu.astype(BF16), b_gu=b_gu.reshape(N_EXPERTS, 1, 2 * D_FF),
        w_down=w_down.astype(BF16), b_down=b_down.reshape(N_EXPERTS, 1, D_MODEL),
    )
    w_in_b = w_in.astype(BF16)
    mult = jnp.stack([jnp.tile(q_norm_g[c // 3] * (HEAD_DIM ** -0.5) if c % 3 == 0 else
                               (k_norm_g[c // 3] if c % 3 == 1 else jnp.ones((HEAD_DIM,), F32)),
                               HEADS_PER_GROUP) for c in range(A_COLS // COL_CHUNK)])
    bd_head = _seg_ones(COL_CHUNK, HEAD_DIM, 1.0 / HEAD_DIM)

    xp2 = x_prompt.reshape(bp * seq, D_MODEL)
    pa, ps, pg = _in_proj(xp2, norm1_g, w_in_b, mult, bd_head, tm=256)
    pa3 = pa.reshape(bp, seq, A_COLS)
    stats = []
    kv_p = []
    for g in range(N_GROUPS):
        stats += _attn_prompt(pa3, g)
        keep = min(GROUP_WINDOWS[g], seq)
        lo = g * 3 * A_WIDTH + A_WIDTH
        kv_p.append(pa3[:, seq - keep:, lo:lo + 2 * A_WIDTH].reshape(bp, keep, 2, HEADS_PER_GROUP, HEAD_DIM))
    ps3 = ps.reshape(bp, seq, SHIFT_WIDTH)
    streams = _rwkv_prep(ps3, jnp.zeros((bp, 1, SHIFT_WIDTH), F32), rw, tm=512)
    o_b, wkv_p = _rwkv_scan(streams, rw, chunk=SCAN_CHUNK)
    h, hn, lg = _finish(xp2, stats, o_b.reshape(bp * seq, B_WIDTH), pg, wb, tm=256)
    y_prompt = _moe(h, hn, lg, wb).reshape(bp, seq, D_MODEL)
    shift_p = ps3[:, -1:]

    caches = (cache_kv_w128, cache_kv_w512, cache_kv_w2048)
    xs2 = x_sample.reshape(bd, D_MODEL)
    pa_s, ps_s, pg_s = _in_proj(xs2, norm1_g, w_in_b, mult, bd_head, tm=bd)
    caches_t = [jnp.transpose(c.astype(F32), (0, 2, 3, 4, 1)) for c in caches]
    *kv_t, oa_t = _sample_cache(pa_s.T, caches_t)
    kv_s = [jnp.transpose(t, (0, 4, 1, 2, 3)).astype(c.dtype) for t, c in zip(kv_t, caches)]
    streams_t = _rwkv_prep_t(ps_s, state_shift.reshape(bd, SHIFT_WIDTH).astype(F32), rw)
    ob_t, s_t = _rwkv_step(streams_t, jnp.transpose(state_wkv.astype(F32), (1, 2, 3, 0)), rw)
    wkv_s = jnp.transpose(s_t, (3, 0, 1, 2))
    h_s, hn_s, lg_s = _finish(xs2, [oa_t.T], ob_t.T, pg_s, wb, tm=bd)
    y_sample = _moe(h_s, hn_s, lg_s, wb).reshape(bd, 1, D_MODEL)
    shift_s = ps_s.reshape(bd, 1, SHIFT_WIDTH)

    return (y_prompt, y_sample, kv_p[0], kv_p[1], kv_p[2], wkv_p.astype(state_wkv.dtype), shift_p,
            kv_s[0], kv_s[1], kv_s[2], wkv_s.astype(state_wkv.dtype), shift_s.astype(state_shift.dtype))
```

```python
import functools

import jax
import jax.numpy as jnp
from jax import lax
from jax.experimental import pallas as pl
from jax.experimental.pallas import tpu as pltpu

F32 = jnp.float32
BF16 = jnp.bfloat16

D_MODEL = 1024
GROUP_WINDOWS = (128, 512, 2048)
GROUP_DILATIONS = (1, 4, 16)
N_GROUPS = 3
HEADS_PER_GROUP = 4
HEAD_DIM = 64
A_WIDTH = HEADS_PER_GROUP * HEAD_DIM
A_COLS = N_GROUPS * 3 * A_WIDTH
RWKV_HEADS = 8
RWKV_HEAD_DIM = 64
B_WIDTH = RWKV_HEADS * RWKV_HEAD_DIM
DECAY_LORA = 64
AAA_LORA = 64
GATE_LORA = 128
SHIFT_WIDTH = 3 * B_WIDTH + DECAY_LORA + AAA_LORA + GATE_LORA
D_IN = A_COLS + SHIFT_WIDTH + 2 * D_MODEL
N_EXPERTS = 32
TOP_K = 4
D_FF = D_MODEL
SWIGLU_LIMIT = 7.0
SWIGLU_ALPHA = 1.702
MOE_BLOCK = 256
RMS_EPS = 1e-6
GN_EPS = 64e-5

N_BACK = 128
COL_CHUNK = 256
SCAN_CHUNK = 64
SCAN_BATCH = 2
LANE_TILE = 128
LOGIT_PAD = LANE_TILE
COMBINE_TOKENS = 256
VMEM_LIMIT = 48 * 1024 * 1024


def _cparams(*sem):
    return pltpu.CompilerParams(dimension_semantics=sem, vmem_limit_bytes=VMEM_LIMIT)


def _sigmoid(x):
    return 1.0 / (1.0 + jnp.exp(-x))


def _split(x):
    hi = x.astype(BF16)
    lo = (x - hi.astype(F32)).astype(BF16)
    return hi, lo


def _dg(a, b, dims):
    return lax.dot_general(a, b, (dims, ((), ())), preferred_element_type=F32)


NN = ((1,), (0,))
NT = ((1,), (1,))
TN = ((0,), (0,))


def _dot1(a, b, dims=NN):
    return _dg(a.astype(BF16), b.astype(BF16), dims)


def _dot3(a, b, dims=NN):
    ah, al = _split(a)
    bh, bl = _split(b)
    return _dg(ah, bh, dims) + _dg(al, bh, dims) + _dg(ah, bl, dims)


def _split3(x):
    hi = x.astype(BF16)
    r1 = x - hi.astype(F32)
    mid = r1.astype(BF16)
    lo = (r1 - mid.astype(F32)).astype(BF16)
    return hi, mid, lo


def _dot_exact_rhs(a, b_bf16):
    return sum(_dg(t, b_bf16, NN) for t in _split3(a))


def _dot_exact_lhs(l_bf16, x):
    return sum(_dg(l_bf16, t, NN) for t in _split3(x))


def _seg_ones(width, seg, scale):
    i = jnp.arange(width)[:, None] // seg
    j = jnp.arange(width)[None, :] // seg
    return jnp.where(i == j, scale, 0.0).astype(BF16)


def _in_proj_kernel(x_ref, g1_ref, w_ref, mult_ref, bd_ref, pa_ref, ps_ref, pg_ref):
    x = x_ref[...]
    xn = x * lax.rsqrt(jnp.mean(x * x, axis=-1, keepdims=True) + RMS_EPS) * g1_ref[...]
    xb = xn.astype(BF16)
    bd = bd_ref[...]
    for c in range(A_COLS // COL_CHUNK):
        acc = _dg(xb, w_ref[:, c * COL_CHUNK:(c + 1) * COL_CHUNK], NN)
        if c % 3 != 2:
            ms = _dot_exact_rhs(acc * acc, bd)
            acc = acc * lax.rsqrt(ms + RMS_EPS) * mult_ref[c:c + 1, :]
        pa_ref[:, c * COL_CHUNK:(c + 1) * COL_CHUNK] = acc
    for c in range(SHIFT_WIDTH // COL_CHUNK):
        lo = A_COLS + c * COL_CHUNK
        ps_ref[:, c * COL_CHUNK:(c + 1) * COL_CHUNK] = _dg(xb, w_ref[:, lo:lo + COL_CHUNK], NN)
    for c in range(2 * D_MODEL // COL_CHUNK):
        lo = A_COLS + SHIFT_WIDTH + c * COL_CHUNK
        pg_ref[:, c * COL_CHUNK:(c + 1) * COL_CHUNK] = _sigmoid(_dg(xb, w_ref[:, lo:lo + COL_CHUNK], NN)).astype(BF16)


def _in_proj(x2, norm1_g, w_in_b, mult, bd, tm):
    n = x2.shape[0]
    const = lambda i: (0, 0)
    return pl.pallas_call(
        _in_proj_kernel,
        grid=(n // tm,),
        in_specs=[
            pl.BlockSpec((tm, D_MODEL), lambda i: (i, 0)),
            pl.BlockSpec((1, D_MODEL), const),
            pl.BlockSpec((D_MODEL, D_IN), const, pipeline_mode=pl.Buffered(1)),
            pl.BlockSpec(mult.shape, const),
            pl.BlockSpec(bd.shape, const),
        ],
        out_specs=[
            pl.BlockSpec((tm, A_COLS), lambda i: (i, 0)),
            pl.BlockSpec((tm, SHIFT_WIDTH), lambda i: (i, 0)),
            pl.BlockSpec((tm, 2 * D_MODEL), lambda i: (i, 0)),
        ],
        out_shape=[
            jax.ShapeDtypeStruct((n, A_COLS), F32),
            jax.ShapeDtypeStruct((n, SHIFT_WIDTH), F32),
            jax.ShapeDtypeStruct((n, 2 * D_MODEL), BF16),
        ],
        compiler_params=_cparams("parallel"),
        name="in_proj",
    )(x2, norm1_g.reshape(1, D_MODEL), w_in_b, mult, bd)


def _attn_kernel(*refs, dil):
    halves = A_WIDTH // LANE_TILE
    q_refs, kp_refs, kc_refs, vp_refs, vc_refs = (refs[i * halves:(i + 1) * halves] for i in range(5))
    out_refs = refs[5 * halves:5 * halves + 3]
    qs, ks, vs = refs[5 * halves + 3:5 * halves + 6]
    stat_s = refs[5 * halves + 6:5 * halves + 9]
    stage = refs[5 * halves + 9:]
    n = pl.program_id(1)
    qi = lax.broadcasted_iota(jnp.int32, (N_BACK, 2 * N_BACK), 0)
    kj = lax.broadcasted_iota(jnp.int32, (N_BACK, 2 * N_BACK), 1)
    first_prev = jnp.where(n > 0, 0, N_BACK)
    lo = jnp.where(kj < N_BACK, qi + first_prev, N_BACK)
    hi = jnp.where(kj < N_BACK, N_BACK - 1, qi + N_BACK)
    mask = jnp.logical_and(kj >= lo, kj <= hi)

    def stream(r, carry):
        rows = pl.ds(r, N_BACK, stride=dil) if dil > 1 else slice(None)
        for c in range(halves):
            lanes = slice(c * LANE_TILE, (c + 1) * LANE_TILE)
            qs[:, lanes] = q_refs[c][rows, :]
            ks[0:N_BACK, lanes] = kp_refs[c][rows, :]
            ks[N_BACK:, lanes] = kc_refs[c][rows, :]
            vs[0:N_BACK, lanes] = vp_refs[c][rows, :]
            vs[N_BACK:, lanes] = vc_refs[c][rows, :]
        sls = [slice(h * HEAD_DIM, (h + 1) * HEAD_DIM) for h in range(HEADS_PER_GROUP)]
        s = [jnp.where(mask, _dot1(qs[:, sl], ks[:, sl], NT), -jnp.inf) for sl in sls]
        m = [jnp.max(t, axis=-1, keepdims=True) for t in s]
        p = [jnp.exp(t - mm) for t, mm in zip(s, m)]
        den = [jnp.sum(t, axis=-1, keepdims=True) for t in p]
        num = [_dot1(t, vs[:, sl]) for t, sl in zip(p, sls)]
        for h, sl in enumerate(sls):
            stat_s[0][:, sl] = num[h]
            stat_s[1][:, sl] = jnp.broadcast_to(den[h], (N_BACK, HEAD_DIM))
            stat_s[2][:, sl] = jnp.broadcast_to(m[h], (N_BACK, HEAD_DIM))
        for i in range(3):
            if dil > 1:
                for c in range(halves):
                    stage[i * halves + c][rows, :] = stat_s[i][:, c * LANE_TILE:(c + 1) * LANE_TILE]
            else:
                out_refs[i][...] = stat_s[i][...]
        return carry

    if dil > 1:
        lax.fori_loop(0, dil, stream, 0)
        for i in range(3):
            for c in range(halves):
                out_refs[i][:, c * LANE_TILE:(c + 1) * LANE_TILE] = stage[i * halves + c][...]
    else:
        stream(0, 0)


def _attn_prompt(pa3, g):
    bsz, seq, _ = pa3.shape
    dil = GROUP_DILATIONS[g]
    rows = N_BACK * dil
    nb = seq // rows
    halves = A_WIDTH // LANE_TILE

    def specs(slab, prev):
        def one(c):
            col = (3 * g + slab) * halves + c
            if prev:
                return pl.BlockSpec((None, rows, LANE_TILE), lambda b, n: (b, jnp.maximum(n - 1, 0), col))
            return pl.BlockSpec((None, rows, LANE_TILE), lambda b, n: (b, n, col))
        return [one(c) for c in range(halves)]

    in_specs = specs(0, False) + specs(1, True) + specs(1, False) + specs(2, True) + specs(2, False)
    out_spec = pl.BlockSpec((None, rows, A_WIDTH), lambda b, n: (b, n, 0))
    out_sds = jax.ShapeDtypeStruct((bsz, seq, A_WIDTH), F32)
    scratch = [pltpu.VMEM((N_BACK, A_WIDTH), F32), pltpu.VMEM((2 * N_BACK, A_WIDTH), F32),
               pltpu.VMEM((2 * N_BACK, A_WIDTH), F32)] + [pltpu.VMEM((N_BACK, A_WIDTH), F32)] * 3
    if dil > 1:
        scratch += [pltpu.VMEM((rows, LANE_TILE), F32)] * (3 * halves)
    outs = pl.pallas_call(
        functools.partial(_attn_kernel, dil=dil),
        grid=(bsz, nb),
        in_specs=in_specs,
        out_specs=[out_spec] * 3,
        out_shape=[out_sds] * 3,
        scratch_shapes=scratch,
        compiler_params=_cparams("parallel", "arbitrary"),
        name=f"attn_prompt_g{g}",
    )(*([pa3] * len(in_specs)))
    return [o.reshape(bsz * seq, A_WIDTH) for o in outs]


def _sample_cache_kernel(pat_ref, c0_ref, c1_ref, c2_ref, o0_ref, o1_ref, o2_ref, oa_ref):
    b = pl.program_id(0)
    n_b = pat_ref.shape[1]

    @pl.when(b == 0)
    def _():
        oa_ref[...] = jnp.zeros_like(oa_ref)

    lane_b = lax.broadcasted_iota(jnp.int32, pat_ref.shape, 1) == b
    col = jnp.sum(jnp.where(lane_b, pat_ref[...], 0.0), axis=1, keepdims=True)
    c_refs = (c0_ref, c1_ref, c2_ref)
    o_refs = (o0_ref, o1_ref, o2_ref)
    stats = []
    for g in range(N_GROUPS):
        dil = GROUP_DILATIONS[g]
        win = GROUP_WINDOWS[g]
        base = g * 3 * A_WIDTH
        lane = lax.broadcasted_iota(jnp.int32, (HEAD_DIM, win), 1)
        last = lane == win - 1
        live = lax.broadcasted_iota(jnp.int32, (1, win), 1) % dil == 0
        for h in range(HEADS_PER_GROUP):
            o = h * HEAD_DIM
            q = col[base + o:base + o + HEAD_DIM]
            kn = col[base + A_WIDTH + o:base + A_WIDTH + o + HEAD_DIM]
            vn = col[base + 2 * A_WIDTH + o:base + 2 * A_WIDTH + o + HEAD_DIM]
            kt = c_refs[g][0, 0, h]
            vt = c_refs[g][0, 1, h]
            o_refs[g][0, 0, h] = jnp.where(last, kn, pltpu.roll(kt, win - 1, axis=1))
            o_refs[g][0, 1, h] = jnp.where(last, vn, pltpu.roll(vt, win - 1, axis=1))
            s_c = jnp.where(live, jnp.sum(kt * q, axis=0, keepdims=True), -jnp.inf)
            s_n = jnp.sum(kn * q, axis=0, keepdims=True)
            m = jnp.maximum(jnp.max(s_c, axis=1, keepdims=True), s_n)
            p_c = jnp.exp(s_c - m)
            p_n = jnp.exp(s_n - m)
            den = jnp.sum(p_c, axis=1, keepdims=True) + p_n
            num = jnp.sum(vt * p_c, axis=1, keepdims=True) + p_n * vn
            stats.append((num, den, m))
    outs = []
    for h in range(HEADS_PER_GROUP):
        per_g = [stats[g * HEADS_PER_GROUP + h] for g in range(N_GROUPS)]
        mx = functools.reduce(jnp.maximum, [m for _, _, m in per_g])
        num = sum(n_ * jnp.exp(m - mx) for n_, _, m in per_g)
        den = sum(d_ * jnp.exp(m - mx) for _, d_, m in per_g)
        outs.append(num / den)
    o_col = jnp.concatenate(outs, axis=0)
    lane_o = lax.broadcasted_iota(jnp.int32, (A_WIDTH, n_b), 1) == b
    oa_ref[...] = jnp.where(lane_o, o_col, oa_ref[...])


def _sample_cache(pa_t, caches_t):
    bd = pa_t.shape[1]
    cspecs = [pl.BlockSpec((1, 2, HEADS_PER_GROUP, HEAD_DIM, w), lambda b: (b, 0, 0, 0, 0)) for w in GROUP_WINDOWS]
    return pl.pallas_call(
        _sample_cache_kernel,
        grid=(bd,),
        in_specs=[pl.BlockSpec(pa_t.shape, lambda b: (0, 0))] + cspecs,
        out_specs=cspecs + [pl.BlockSpec((A_WIDTH, bd), lambda b: (0, 0))],
        out_shape=[jax.ShapeDtypeStruct(c.shape, c.dtype) for c in caches_t]
                  + [jax.ShapeDtypeStruct((A_WIDTH, bd), F32)],
        compiler_params=_cparams("arbitrary"),
        name="sample_cache",
    )(pa_t, *caches_t)


def _rwkv_prep_math(pf, shifted, mu_ref, w0_ref, w2_ref, a0_ref, a2_ref, g2_ref, kk_ref, ka_ref, bd_ref):
    z = pf + mu_ref[...] * (shifted - pf)
    c1, c2, c3 = B_WIDTH, 2 * B_WIDTH, 3 * B_WIDTH
    r, k, v = z[:, :c1], z[:, c1:c2], z[:, c2:c3]
    wl = z[:, c3:c3 + DECAY_LORA]
    al = z[:, c3 + DECAY_LORA:c3 + DECAY_LORA + AAA_LORA]
    gl = z[:, c3 + DECAY_LORA + AAA_LORA:]
    xw = w0_ref[...] + _dot1(jnp.tanh(wl), w2_ref[...])
    w_log = -(jnp.maximum(-xw, 0.0) + jnp.log(1.0 + jnp.exp(-jnp.abs(xw)))) - 0.5
    lw = -jnp.exp(w_log)
    a = _sigmoid(a0_ref[...] + _dot1(al, a2_ref[...]))
    g = _dot1(_sigmoid(gl), g2_ref[...])
    kk = k * kk_ref[...]
    bd = bd_ref[...]
    kks = []
    for c in range(B_WIDTH // 128):
        kc = kk[:, c * 128:(c + 1) * 128]
        nrm = jnp.sqrt(_dot_exact_rhs(kc * kc, bd))
        kks.append(kc / jnp.maximum(nrm, 1e-12))
    kkn = jnp.concatenate(kks, axis=1)
    return r, lw, k * (1.0 + (a - 1.0) * ka_ref[...]), v, -kkn, kkn * a, g


def _rwkv_prep_kernel(ps_ref, pv_ref, p0_ref, *refs):
    param_refs, out_refs = refs[:9], refs[9:]
    i = pl.program_id(1)
    pf = ps_ref[...]
    prev_row = jnp.where(i == 0, p0_ref[...], pv_ref[7:8, :])
    row = lax.broadcasted_iota(jnp.int32, pf.shape, 0)
    shifted = jnp.where(row == 0, prev_row, pltpu.roll(pf, 1, axis=0))
    for o_ref, val in zip(out_refs, _rwkv_prep_math(pf, shifted, *param_refs)):
        o_ref[...] = val


def _rwkv_prep_t_kernel(ps_ref, prev_ref, *refs):
    param_refs, out_refs = refs[:9], refs[9:]
    for o_ref, val in zip(out_refs, _rwkv_prep_math(ps_ref[...], prev_ref[...], *param_refs)):
        o_ref[...] = val.T


def _prep_params(p):
    row = lambda a: a.reshape(1, -1)
    return [row(p["mu_shift"]), row(p["w0"]), p["w2"], row(p["a0"]), p["a2"], p["g2"],
            row(p["k_k"]), row(p["k_a"]), _seg_ones(128, RWKV_HEAD_DIM, 1.0)]


def _rwkv_prep(ps3, prev0, p, tm):
    bsz, t, _ = ps3.shape
    const = lambda b, i: (0, 0)
    params = _prep_params(p)
    out_spec = pl.BlockSpec((None, tm, B_WIDTH), lambda b, i: (b, i, 0))
    out_sds = jax.ShapeDtypeStruct((bsz, t, B_WIDTH), F32)
    return pl.pallas_call(
        _rwkv_prep_kernel,
        grid=(bsz, t // tm),
        in_specs=[
            pl.BlockSpec((None, tm, SHIFT_WIDTH), lambda b, i: (b, i, 0)),
            pl.BlockSpec((None, 8, SHIFT_WIDTH), lambda b, i: (b, jnp.maximum(i * (tm // 8) - 1, 0), 0)),
            pl.BlockSpec((None, 1, SHIFT_WIDTH), lambda b, i: (b, 0, 0)),
        ] + [pl.BlockSpec(a.shape, const) for a in params],
        out_specs=[out_spec] * 7,
        out_shape=[out_sds] * 7,
        compiler_params=_cparams("parallel", "arbitrary"),
        name="rwkv_prep",
    )(ps3, ps3, prev0, *params)


def _rwkv_prep_t(ps2, prev2, p):
    bd = ps2.shape[0]
    params = _prep_params(p)
    full = lambda a: pl.BlockSpec(a.shape, lambda i: (0, 0))
    out_sds = jax.ShapeDtypeStruct((B_WIDTH, bd), F32)
    return pl.pallas_call(
        _rwkv_prep_t_kernel,
        grid=(1,),
        in_specs=[full(ps2), full(prev2)] + [full(a) for a in params],
        out_specs=[pl.BlockSpec((B_WIDTH, bd), lambda i: (0, 0))] * 7,
        out_shape=[out_sds] * 7,
        compiler_params=_cparams("arbitrary"),
        name="rwkv_prep_t",
    )(ps2, prev2, *params)


def _rwkv_scan_kernel(r_ref, lw_ref, k_ref, v_ref, a_ref, b_ref, g_ref, rk_ref, lnw_ref, lnb_ref,
                      tril_ref, o_ref, s_ref):
    n_b, chunk, _ = r_ref.shape

    @pl.when(pl.program_id(1) == 0)
    def _():
        s_ref[...] = jnp.zeros_like(s_ref)

    ti = lax.broadcasted_iota(jnp.int32, (chunk, 2 * chunk), 0)
    si = lax.broadcasted_iota(jnp.int32, (chunk, 2 * chunk), 1)
    si = jnp.where(si >= chunk, si - chunk, si)
    strict = si < ti
    incl = si <= ti
    eye = jnp.where(lax.broadcasted_iota(jnp.int32, (chunk, chunk), 0)
                    == lax.broadcasted_iota(jnp.int32, (chunk, chunk), 1), 1.0, 0.0).astype(F32)
    n_sq = max(chunk.bit_length() - 2, 0)

    sls = [slice(h * RWKV_HEAD_DIM, (h + 1) * RWKV_HEAD_DIM) for h in range(RWKV_HEADS)]

    chains, ar, bk, v_h, g_last = [], [], [], [], []
    for bi in range(n_b):
        lw = lw_ref[bi]
        cum = _dot_exact_lhs(tril_ref[...], lw)
        e_in = jnp.exp(cum)
        e_inv = jnp.exp(-cum)
        at = a_ref[bi] * jnp.exp(cum - lw)
        rt = r_ref[bi] * e_in
        bt = b_ref[bi] * e_inv
        kt = k_ref[bi] * e_inv
        v = v_ref[bi]
        for h, sl in enumerate(sls):
            chains.append((bi, h))
            ar.append(jnp.concatenate([at[:, sl], rt[:, sl]], axis=0))
            bk.append(jnp.concatenate([bt[:, sl], kt[:, sl]], axis=0))
            v_h.append(v[:, sl])
            g_last.append(e_in[chunk - 1:chunk, sl])
    n = range(len(chains))
    s_old = [s_ref[bi, h] for bi, h in chains]
    m_all = [_dot1(ar[c], bk[c], NT) for c in n]
    gs = [_dot1(ar[c], s_old[c], NT) for c in n]
    a_m = [jnp.where(strict, m_all[c][:chunk], 0.0) for c in n]
    r_m = [jnp.where(incl, m_all[c][chunk:], 0.0) for c in n]
    pw = [a_m[c][:, :chunk] for c in n]
    x = [eye + pw[c] for c in n]
    for _ in range(n_sq):
        pw = [_dot1(pw[c], pw[c]) for c in n]
        x = [x[c] + _dot1(x[c], pw[c]) for c in n]
    av = [_dot1(a_m[c][:, chunk:], v_h[c]) for c in n]
    u = [_dot1(x[c], gs[c][:chunk] + av[c]) for c in n]
    uv = [jnp.concatenate([u[c], v_h[c]], axis=0) for c in n]
    y = [gs[c][chunk:] + _dot1(r_m[c], uv[c]) for c in n]
    for c, (bi, h) in enumerate(chains):
        s_ref[bi, h] = (s_old[c] + _dot3(uv[c], bk[c], TN)) * g_last[c]
    for c, (bi, h) in enumerate(chains):
        sl = sls[h]
        mu = jnp.mean(y[c], axis=-1, keepdims=True)
        var = jnp.mean(jnp.square(y[c] - mu), axis=-1, keepdims=True)
        yn = (y[c] - mu) * lax.rsqrt(var + GN_EPS) * lnw_ref[:, sl] + lnb_ref[:, sl]
        bonus = jnp.sum(r_ref[bi, :, sl] * k_ref[bi, :, sl] * rk_ref[:, sl], axis=-1, keepdims=True) * v_h[c]
        o_ref[bi, :, sl] = (yn + bonus) * g_ref[bi, :, sl]


def _rwkv_scan(streams, p, chunk):
    bsz, t, _ = streams[0].shape
    n_b = SCAN_BATCH if bsz % SCAN_BATCH == 0 else 1
    row = lambda a: a.reshape(1, B_WIDTH)
    tril = (jnp.arange(chunk)[None, :] <= jnp.arange(chunk)[:, None]).astype(BF16)
    const = lambda b, c: (0, 0)
    blk = pl.BlockSpec((n_b, chunk, B_WIDTH), lambda b, c: (b, c, 0))
    st_shape = (bsz, RWKV_HEADS, RWKV_HEAD_DIM, RWKV_HEAD_DIM)
    st = pl.BlockSpec((n_b,) + st_shape[1:], lambda b, c: (b, 0, 0, 0))
    vec = pl.BlockSpec((1, B_WIDTH), const)
    return pl.pallas_call(
        _rwkv_scan_kernel,
        grid=(bsz // n_b, t // chunk),
        in_specs=[blk] * 7 + [vec, vec, vec, pl.BlockSpec((chunk, chunk), const)],
        out_specs=[blk, st],
        out_shape=[jax.ShapeDtypeStruct((bsz, t, B_WIDTH), F32), jax.ShapeDtypeStruct(st_shape, F32)],
        compiler_params=_cparams("parallel", "arbitrary"),
        name="rwkv_scan",
    )(*streams, row(p["r_k"]), row(p["ln_x_w"]), row(p["ln_x_b"]), tril)


def _rwkv_step_kernel(r_ref, lw_ref, k_ref, v_ref, a_ref, b_ref, g_ref, rk_ref, lnw_ref, lnb_ref, s_ref,
                      o_ref, so_ref, y_ref):
    w = jnp.exp(lw_ref[...])
    a, b, k, r = a_ref[...], b_ref[...], k_ref[...], r_ref[...]

    def value_row(i, carry):
        s = s_ref[i]
        sa = jnp.sum(s * a, axis=0, keepdims=True)
        s_new = s * w + sa * b + v_ref[pl.ds(i, 1), :] * k
        so_ref[i] = s_new
        y_ref[pl.ds(i, 1), :] = jnp.sum(s_new * r, axis=0, keepdims=True)
        return carry

    lax.fori_loop(0, RWKV_HEAD_DIM, value_row, 0, unroll=4)
    y = y_ref[...]
    mu = jnp.mean(y, axis=0, keepdims=True)
    var = jnp.mean(jnp.square(y - mu), axis=0, keepdims=True)
    yn = (y - mu) * lax.rsqrt(var + GN_EPS) * lnw_ref[...] + lnb_ref[...]
    bonus = jnp.sum(r_ref[...] * k_ref[...] * rk_ref[...], axis=0, keepdims=True) * v_ref[...]
    o_ref[...] = (yn + bonus) * g_ref[...]


def _rwkv_step(streams_t, s_t, p):
    bd = s_t.shape[-1]
    n = RWKV_HEAD_DIM
    col = lambda a: a.reshape(B_WIDTH, 1)
    vec = pl.BlockSpec((n, bd), lambda h: (h, 0))
    par = pl.BlockSpec((n, 1), lambda h: (h, 0))
    st = pl.BlockSpec((None, n, n, bd), lambda h: (h, 0, 0, 0))
    return pl.pallas_call(
        _rwkv_step_kernel,
        grid=(RWKV_HEADS,),
        in_specs=[vec] * 7 + [par] * 3 + [st],
        out_specs=[vec, st],
        out_shape=[jax.ShapeDtypeStruct((B_WIDTH, bd), F32), jax.ShapeDtypeStruct(s_t.shape, F32)],
        scratch_shapes=[pltpu.VMEM((n, bd), F32)],
        compiler_params=_cparams("parallel"),
        name="rwkv_step",
    )(*streams_t, col(p["r_k"]), col(p["ln_x_w"]), col(p["ln_x_b"]), s_t)


def _finish_kernel(x_ref, *refs, n_stats):
    stat_refs = refs[:n_stats]
    ob_ref, g_ref, wpa_ref, wpb_ref, wo_ref, g2_ref, rw_ref, rb_ref, h_ref, hn_ref, lg_ref = refs[n_stats:]
    if n_stats == 1:
        o_a = stat_refs[0][...]
    else:
        nums, dens, maxes = stat_refs[0::3], stat_refs[1::3], stat_refs[2::3]
        mx = functools.reduce(jnp.maximum, [m[...] for m in maxes])
        es = [jnp.exp(m[...] - mx) for m in maxes]
        num = sum(n_[...] * e for n_, e in zip(nums, es))
        den = sum(d_[...] * e for d_, e in zip(dens, es))
        o_a = num / den
    br_a = _dot1(o_a, wpa_ref[...])
    br_b = _dot1(ob_ref[...], wpb_ref[...])
    mix = g_ref[:, :D_MODEL] * br_a + g_ref[:, D_MODEL:] * br_b
    h = x_ref[...] + _dot1(mix, wo_ref[...])
    h_ref[...] = h
    hn = h * lax.rsqrt(jnp.mean(h * h, axis=-1, keepdims=True) + RMS_EPS) * g2_ref[...]
    hn_ref[...] = hn
    lg_ref[...] = _dot3(hn, rw_ref[...]) + rb_ref[...]


def _finish(x2, stats, o_b, gates, wb, tm):
    n = x2.shape[0]
    const = lambda i: (0, 0)
    rows = lambda w: pl.BlockSpec((tm, w), lambda i: (i, 0))
    params = [wb["w_pa"], wb["w_pb"], wb["w_o"], wb["norm2_g"], wb["router_w"], wb["router_b"]]
    return pl.pallas_call(
        functools.partial(_finish_kernel, n_stats=len(stats)),
        grid=(n // tm,),
        in_specs=[rows(D_MODEL)] + [rows(A_WIDTH)] * len(stats) + [rows(B_WIDTH), rows(2 * D_MODEL)]
                 + [pl.BlockSpec(a.shape, const) for a in params],
        out_specs=[rows(D_MODEL), rows(D_MODEL), rows(LOGIT_PAD)],
        out_shape=[jax.ShapeDtypeStruct((n, D_MODEL), F32), jax.ShapeDtypeStruct((n, D_MODEL), F32),
                   jax.ShapeDtypeStruct((n, LOGIT_PAD), F32)],
        compiler_params=_cparams("parallel"),
        name="finish",
    )(x2, *stats, o_b, gates, *params)


def _row(ref, i):
    return ref.at[pl.ds(i, 1)]


def _moe_kernel(blk_e_ref, nused_ref, tok_ref, tok_next_ref, hn_hbm, wgu32_ref, bgu_ref, wd32_ref, bdn_ref,
                ys_ref, xbuf_a, xbuf_b, wgu_ref, wd_ref, fence_buf, sem, fence_sem):
    i = pl.program_id(0)
    nused = nused_ref[0]
    n_chunks = D_FF // COL_CHUNK
    bufs = (xbuf_a, xbuf_b)

    def issue(idx_ref, s):
        for r in range(MOE_BLOCK):
            pltpu.make_async_copy(_row(hn_hbm, idx_ref[0, 0, r]), _row(bufs[s], r), sem.at[s]).start(priority=r % 2)

    def drain(s):
        for _ in range(MOE_BLOCK):
            pltpu.make_async_copy(_row(hn_hbm, 0), _row(bufs[s], 0), sem.at[s]).wait()

    @pl.when(jnp.logical_and(i == 0, nused > 0))
    def _():
        issue(tok_ref, 0)

    new_expert = jnp.logical_or(i == 0, blk_e_ref[i] != blk_e_ref[jnp.maximum(i - 1, 0)])

    @pl.when(jnp.logical_and(i < nused, new_expert))
    def _():
        wgu_ref[...] = wgu32_ref[...].astype(BF16)
        wd_ref[...] = wd32_ref[...].astype(BF16)

    def block(s):
        drain(s)
        x = bufs[s][...].astype(BF16)
        fence_buf[0] = jnp.zeros(fence_buf.shape[1:], F32)
        fence = pltpu.make_async_copy(fence_buf.at[0], fence_buf.at[1], fence_sem)
        fence.start()
        issue(tok_next_ref, 1 - s)
        acc = jnp.zeros((MOE_BLOCK, D_MODEL), F32)
        for c in range(n_chunks):
            if c == n_chunks // 2:
                fence.wait()
            lo = c * COL_CHUNK
            gate = _dg(x, wgu_ref[:, lo:lo + COL_CHUNK], NN) + bgu_ref[:, lo:lo + COL_CHUNK]
            up = _dg(x, wgu_ref[:, D_FF + lo:D_FF + lo + COL_CHUNK], NN) + bgu_ref[:, D_FF + lo:D_FF + lo + COL_CHUNK]
            gate = jnp.minimum(gate, SWIGLU_LIMIT)
            up = jnp.clip(up, -SWIGLU_LIMIT, SWIGLU_LIMIT)
            hid = (up + 1.0) * gate * _sigmoid(gate * SWIGLU_ALPHA)
            acc = acc + _dg(hid.astype(BF16), wd_ref[lo:lo + COL_CHUNK, :], NN)
        ys_ref[...] = acc + bdn_ref[...]

        @pl.when(i + 1 >= nused)
        def _():
            drain(1 - s)

    for s in range(2):
        pl.when(jnp.logical_and(i < nused, i % 2 == s))(functools.partial(block, s))

    @pl.when(i >= nused)
    def _():
        ys_ref[...] = jnp.zeros_like(ys_ref)


def _moe_blocks(hn, row_tok, blk_e, nused, wb):
    n_blocks = row_tok.shape[0]
    last = n_blocks - 1
    grid_spec = pltpu.PrefetchScalarGridSpec(
        num_scalar_prefetch=2,
        grid=(n_blocks,),
        in_specs=[
            pl.BlockSpec((1, 1, MOE_BLOCK), lambda i, be, nu: (i, 0, 0), memory_space=pltpu.SMEM),
            pl.BlockSpec((1, 1, MOE_BLOCK), lambda i, be, nu: (jnp.minimum(i + 1, last), 0, 0),
                         memory_space=pltpu.SMEM),
            pl.BlockSpec(memory_space=pl.ANY),
            pl.BlockSpec((None, D_MODEL, 2 * D_FF), lambda i, be, nu: (be[i], 0, 0)),
            pl.BlockSpec((None, 1, 2 * D_FF), lambda i, be, nu: (be[i], 0, 0)),
            pl.BlockSpec((None, D_FF, D_MODEL), lambda i, be, nu: (be[i], 0, 0)),
            pl.BlockSpec((None, 1, D_MODEL), lambda i, be, nu: (be[i], 0, 0)),
        ],
        out_specs=pl.BlockSpec((MOE_BLOCK, D_MODEL), lambda i, be, nu: (i, 0)),
        scratch_shapes=[pltpu.VMEM((MOE_BLOCK, D_MODEL), F32), pltpu.VMEM((MOE_BLOCK, D_MODEL), F32),
                        pltpu.VMEM((D_MODEL, 2 * D_FF), BF16), pltpu.VMEM((D_FF, D_MODEL), BF16),
                        pltpu.VMEM((2, 8, LANE_TILE), F32), pltpu.SemaphoreType.DMA((2,)),
                        pltpu.SemaphoreType.DMA(())],
    )
    return pl.pallas_call(
        _moe_kernel,
        grid_spec=grid_spec,
        out_shape=jax.ShapeDtypeStruct((n_blocks * MOE_BLOCK, D_MODEL), F32),
        compiler_params=_cparams("arbitrary"),
        name="moe_blocks",
    )(blk_e, nused, row_tok, row_tok, hn, wb["w_gu"], wb["b_gu"], wb["w_down"], wb["b_down"])


def _combine_kernel(pos_ref, pos_next_ref, h_ref, gate_ref, ys_hbm, y_ref, gbuf, sem):
    tm = h_ref.shape[0]
    rows = TOP_K * tm
    i = pl.program_id(0)
    n = pl.num_programs(0)
    slot = i % 2

    def issue(idx_ref, s):
        for kk in range(rows):
            pltpu.make_async_copy(_row(ys_hbm, idx_ref[0, 0, kk]), gbuf.at[s, pl.ds(kk, 1)], sem.at[s]).start(
                priority=kk % 2)

    @pl.when(i == 0)
    def _():
        issue(pos_ref, 0)

    @pl.when(i + 1 < n)
    def _():
        issue(pos_next_ref, 1 - slot)

    for kk in range(rows):
        pltpu.make_async_copy(_row(ys_hbm, 0), gbuf.at[slot, pl.ds(0, 1)], sem.at[slot]).wait()
    gv = gate_ref[...]
    ff = gv[:, 0:1] * gbuf[slot, 0:tm, :]
    for j in range(1, TOP_K):
        ff = ff + gv[:, j:j + 1] * gbuf[slot, j * tm:(j + 1) * tm, :]
    y_ref[...] = h_ref[...] + ff


def _combine(h, gates_pad, pos, ys):
    n = h.shape[0]
    tm = min(COMBINE_TOKENS, n)
    steps = n // tm
    rows = TOP_K * tm
    pos3 = pos.reshape(steps, tm, TOP_K).transpose(0, 2, 1).reshape(steps, 1, rows)
    return pl.pallas_call(
        _combine_kernel,
        grid=(steps,),
        in_specs=[
            pl.BlockSpec((1, 1, rows), lambda i: (i, 0, 0), memory_space=pltpu.SMEM),
            pl.BlockSpec((1, 1, rows), lambda i: (jnp.minimum(i + 1, steps - 1), 0, 0), memory_space=pltpu.SMEM),
            pl.BlockSpec((tm, D_MODEL), lambda i: (i, 0)),
            pl.BlockSpec((tm, LOGIT_PAD), lambda i: (i, 0)),
            pl.BlockSpec(memory_space=pl.ANY),
        ],
        out_specs=pl.BlockSpec((tm, D_MODEL), lambda i: (i, 0)),
        out_shape=jax.ShapeDtypeStruct((n, D_MODEL), F32),
        scratch_shapes=[pltpu.VMEM((2, rows, D_MODEL), F32), pltpu.SemaphoreType.DMA((2,))],
        compiler_params=_cparams("arbitrary"),
        name="moe_combine",
    )(pos3, pos3, h, gates_pad, ys)


def _route(logits):
    n = logits.shape[0]
    nk = n * TOP_K
    top_v, top_e = lax.top_k(logits, TOP_K)
    gates = jax.nn.softmax(top_v, axis=-1)
    flat_e = top_e.reshape(nk)
    onehot = (flat_e[:, None] == jnp.arange(N_EXPERTS, dtype=top_e.dtype)[None, :]).astype(jnp.int32)
    csum = jnp.cumsum(onehot, axis=0)
    counts = csum[-1]
    starts = jnp.cumsum(counts) - counts
    pcounts = (counts + MOE_BLOCK - 1) // MOE_BLOCK * MOE_BLOCK
    pends = jnp.cumsum(pcounts)
    pstarts = pends - pcounts
    pos = jnp.sum(onehot * (csum - 1 + pstarts[None, :]), axis=1).astype(jnp.int32)
    n_blocks = -(-nk // MOE_BLOCK) + N_EXPERTS
    blk_start = jnp.arange(n_blocks, dtype=jnp.int32) * MOE_BLOCK
    blk_e = jnp.minimum(jnp.sum(blk_start[:, None] >= pends[None, :], axis=1), N_EXPERTS - 1).astype(jnp.int32)
    nused = (pends[-1:] // MOE_BLOCK).astype(jnp.int32)
    order = jnp.argsort(flat_e)
    e_row = jnp.repeat(blk_e, MOE_BLOCK)
    rank = jnp.arange(n_blocks * MOE_BLOCK, dtype=jnp.int32) - pstarts[e_row]
    src = jnp.minimum(starts[e_row] + rank, nk - 1)
    row_tok = jnp.where(rank < counts[e_row], order[src] // TOP_K, 0).astype(jnp.int32)
    gates_pad = jnp.pad(gates, ((0, 0), (0, LOGIT_PAD - TOP_K)))
    return pos, row_tok.reshape(n_blocks, 1, MOE_BLOCK), blk_e, nused, gates_pad


def _moe(h, hn, logits_pad, wb):
    pos, row_tok, blk_e, nused, gates_pad = _route(logits_pad[:, :N_EXPERTS])
    ys = _moe_blocks(hn, row_tok, blk_e, nused, wb)
    return _combine(h, gates_pad, pos, ys)


def kernel(x_prompt, x_sample, cache_kv_w128, cache_kv_w512, cache_kv_w2048, state_wkv, state_shift,
           norm1_g, w_in, q_norm_g, k_norm_g, mu_shift, w0, w2, a0, a2, g2, k_k, k_a, r_k,
           ln_x_w, ln_x_b, w_pa, w_pb, w_o, norm2_g, router_w, router_b, w_gu, b_gu, w_down, b_down):
    bp, seq, _ = x_prompt.shape
    bd, t_s, _ = x_sample.shape
    assert t_s == 1
    rw = dict(mu_shift=mu_shift, w0=w0, w2=w2, a0=a0, a2=a2, g2=g2, k_k=k_k, k_a=k_a, r_k=r_k,
              ln_x_w=ln_x_w, ln_x_b=ln_x_b)
    wb = dict(
        w_pa=w_pa.astype(BF16), w_pb=w_pb.astype(BF16), w_o=w_o.astype(BF16),
        norm2_g=norm2_g.reshape(1, D_MODEL),
        router_w=jnp.pad(router_w, ((0, 0), (0, LOGIT_PAD - N_EXPERTS))),
        router_b=jnp.pad(router_b, (0, LOGIT_PAD - N_EXPERTS)).reshape(1, LOGIT_PAD),
        w_gu=w_gu.astype(F32), b_gu=b_gu.reshape(N_EXPERTS, 1, 2 * D_FF),
        w_down=w_down.astype(F32), b_down=b_down.reshape(N_EXPERTS, 1, D_MODEL),
    )
    w_in_b = w_in.astype(BF16)
    mult = jnp.stack([jnp.tile(q_norm_g[c // 3] * (HEAD_DIM ** -0.5) if c % 3 == 0 else
                               (k_norm_g[c // 3] if c % 3 == 1 else jnp.ones((HEAD_DIM,), F32)),
                               HEADS_PER_GROUP) for c in range(A_COLS // COL_CHUNK)])
    bd_head = _seg_ones(COL_CHUNK, HEAD_DIM, 1.0 / HEAD_DIM)

    xp2 = x_prompt.reshape(bp * seq, D_MODEL)
    pa, ps, pg = _in_proj(xp2, norm1_g, w_in_b, mult, bd_head, tm=256)
    pa3 = pa.reshape(bp, seq, A_COLS)
    stats = []
    kv_p = []
    for g in range(N_GROUPS):
        stats += _attn_prompt(pa3, g)
        keep = min(GROUP_WINDOWS[g], seq)
        lo = g * 3 * A_WIDTH + A_WIDTH
        kv_p.append(pa3[:, seq - keep:, lo:lo + 2 * A_WIDTH].reshape(bp, keep, 2, HEADS_PER_GROUP, HEAD_DIM))
    ps3 = ps.reshape(bp, seq, SHIFT_WIDTH)
    streams = _rwkv_prep(ps3, jnp.zeros((bp, 1, SHIFT_WIDTH), F32), rw, tm=512)
    o_b, wkv_p = _rwkv_scan(streams, rw, chunk=SCAN_CHUNK)
    h, hn, lg = _finish(xp2, stats, o_b.reshape(bp * seq, B_WIDTH), pg, wb, tm=256)
    y_prompt = _moe(h, hn, lg, wb).reshape(bp, seq, D_MODEL)
    shift_p = ps3[:, -1:]

    caches = (cache_kv_w128, cache_kv_w512, cache_kv_w2048)
    xs2 = x_sample.reshape(bd, D_MODEL)
    pa_s, ps_s, pg_s = _in_proj(xs2, norm1_g, w_in_b, mult, bd_head, tm=bd)
    caches_t = [jnp.transpose(c.astype(F32), (0, 2, 3, 4, 1)) for c in caches]
    *kv_t, oa_t = _sample_cache(pa_s.T, caches_t)
    kv_s = [jnp.transpose(t, (0, 4, 1, 2, 3)).astype(c.dtype) for t, c in zip(kv_t, caches)]
    streams_t = _rwkv_prep_t(ps_s, state_shift.reshape(bd, SHIFT_WIDTH).astype(F32), rw)
    ob_t, s_t = _rwkv_step(streams_t, jnp.transpose(state_wkv.astype(F32), (1, 2, 3, 0)), rw)
    wkv_s = jnp.transpose(s_t, (3, 0, 1, 2))
    h_s, hn_s, lg_s = _finish(xs2, [oa_t.T], ob_t.T, pg_s, wb, tm=bd)
    y_sample = _moe(h_s, hn_s, lg_s, wb).reshape(bd, 1, D_MODEL)
    shift_s = ps_s.reshape(bd, 1, SHIFT_WIDTH)

    return (y_prompt, y_sample, kv_p[0], kv_p[1], kv_p[2], wkv_p.astype(state_wkv.dtype), shift_p,
            kv_s[0], kv_s[1], kv_s[2], wkv_s.astype(state_wkv.dtype), shift_s.astype(state_shift.dtype))
```

```python
import functools

import jax
import jax.numpy as jnp
from jax import lax
from jax.experimental import pallas as pl
from jax.experimental.pallas import tpu as pltpu

F32 = jnp.float32
BF16 = jnp.bfloat16

D_MODEL = 1024
GROUP_WINDOWS = (128, 512, 2048)
GROUP_DILATIONS = (1, 4, 16)
N_GROUPS = 3
HEADS_PER_GROUP = 4
HEAD_DIM = 64
A_WIDTH = HEADS_PER_GROUP * HEAD_DIM
A_COLS = N_GROUPS * 3 * A_WIDTH
RWKV_HEADS = 8
RWKV_HEAD_DIM = 64
B_WIDTH = RWKV_HEADS * RWKV_HEAD_DIM
DECAY_LORA = 64
AAA_LORA = 64
GATE_LORA = 128
SHIFT_WIDTH = 3 * B_WIDTH + DECAY_LORA + AAA_LORA + GATE_LORA
D_IN = A_COLS + SHIFT_WIDTH + 2 * D_MODEL
N_EXPERTS = 32
TOP_K = 4
D_FF = D_MODEL
SWIGLU_LIMIT = 7.0
SWIGLU_ALPHA = 1.702
MOE_BLOCK = 256
RMS_EPS = 1e-6
GN_EPS = 64e-5

N_BACK = 128
COL_CHUNK = 256
SCAN_CHUNK = 64
SCAN_BATCH = 2
LANE_TILE = 128
N_STATS = 2
LOGIT_PAD = LANE_TILE
COMBINE_TOKENS = 256
VMEM_LIMIT = 48 * 1024 * 1024


def _cparams(*sem):
    return pltpu.CompilerParams(dimension_semantics=sem, vmem_limit_bytes=VMEM_LIMIT)


def _sigmoid(x):
    return 1.0 / (1.0 + jnp.exp(-x))


def _split(x):
    hi = x.astype(BF16)
    lo = (x - hi.astype(F32)).astype(BF16)
    return hi, lo


def _dg(a, b, dims):
    return lax.dot_general(a, b, (dims, ((), ())), preferred_element_type=F32)


NN = ((1,), (0,))
NT = ((1,), (1,))
TN = ((0,), (0,))


def _dot1(a, b, dims=NN):
    return _dg(a.astype(BF16), b.astype(BF16), dims)


def _dot3(a, b, dims=NN):
    ah, al = _split(a)
    bh, bl = _split(b)
    return _dg(ah, bh, dims) + _dg(al, bh, dims) + _dg(ah, bl, dims)


def _split3(x):
    hi = x.astype(BF16)
    r1 = x - hi.astype(F32)
    mid = r1.astype(BF16)
    lo = (r1 - mid.astype(F32)).astype(BF16)
    return hi, mid, lo


def _dot_exact_rhs(a, b_bf16):
    return sum(_dg(t, b_bf16, NN) for t in _split3(a))


def _dot_exact_lhs(l_bf16, x):
    return sum(_dg(l_bf16, t, NN) for t in _split3(x))


def _seg_ones(width, seg, scale):
    i = jnp.arange(width)[:, None] // seg
    j = jnp.arange(width)[None, :] // seg
    return jnp.where(i == j, scale, 0.0).astype(BF16)


def _in_proj_kernel(x_ref, g1_ref, w_ref, mult_ref, bd_ref, pa_ref, ps_ref, pg_ref):
    x = x_ref[...]
    xn = x * lax.rsqrt(jnp.mean(x * x, axis=-1, keepdims=True) + RMS_EPS) * g1_ref[...]
    xb = xn.astype(BF16)
    bd = bd_ref[...]
    for c in range(A_COLS // COL_CHUNK):
        acc = _dg(xb, w_ref[:, c * COL_CHUNK:(c + 1) * COL_CHUNK], NN)
        if c % 3 != 2:
            ms = _dot_exact_rhs(acc * acc, bd)
            acc = acc * lax.rsqrt(ms + RMS_EPS) * mult_ref[c:c + 1, :]
        pa_ref[:, c * COL_CHUNK:(c + 1) * COL_CHUNK] = acc
    for c in range(SHIFT_WIDTH // COL_CHUNK):
        lo = A_COLS + c * COL_CHUNK
        ps_ref[:, c * COL_CHUNK:(c + 1) * COL_CHUNK] = _dg(xb, w_ref[:, lo:lo + COL_CHUNK], NN)
    for c in range(2 * D_MODEL // COL_CHUNK):
        lo = A_COLS + SHIFT_WIDTH + c * COL_CHUNK
        pg_ref[:, c * COL_CHUNK:(c + 1) * COL_CHUNK] = _sigmoid(_dg(xb, w_ref[:, lo:lo + COL_CHUNK], NN)).astype(BF16)


def _in_proj(x2, norm1_g, w_in_b, mult, bd, tm):
    n = x2.shape[0]
    const = lambda i: (0, 0)
    return pl.pallas_call(
        _in_proj_kernel,
        grid=(n // tm,),
        in_specs=[
            pl.BlockSpec((tm, D_MODEL), lambda i: (i, 0)),
            pl.BlockSpec((1, D_MODEL), const),
            pl.BlockSpec((D_MODEL, D_IN), const, pipeline_mode=pl.Buffered(1)),
            pl.BlockSpec(mult.shape, const),
            pl.BlockSpec(bd.shape, const),
        ],
        out_specs=[
            pl.BlockSpec((tm, A_COLS), lambda i: (i, 0)),
            pl.BlockSpec((tm, SHIFT_WIDTH), lambda i: (i, 0)),
            pl.BlockSpec((tm, 2 * D_MODEL), lambda i: (i, 0)),
        ],
        out_shape=[
            jax.ShapeDtypeStruct((n, A_COLS), F32),
            jax.ShapeDtypeStruct((n, SHIFT_WIDTH), F32),
            jax.ShapeDtypeStruct((n, 2 * D_MODEL), BF16),
        ],
        compiler_params=_cparams("parallel"),
        name="in_proj",
    )(x2, norm1_g.reshape(1, D_MODEL), w_in_b, mult, bd)


def _attn_kernel(*refs, dil):
    halves = A_WIDTH // LANE_TILE
    q_refs, kp_refs, kc_refs, vp_refs, vc_refs = (refs[i * halves:(i + 1) * halves] for i in range(5))
    out_refs = refs[5 * halves:5 * halves + N_STATS]
    qs, ks, vs = refs[5 * halves + N_STATS:5 * halves + N_STATS + 3]
    stat_s = refs[5 * halves + N_STATS + 3:5 * halves + 2 * N_STATS + 3]
    stage = refs[5 * halves + 2 * N_STATS + 3:]
    n = pl.program_id(1)
    qi = lax.broadcasted_iota(jnp.int32, (N_BACK, 2 * N_BACK), 0)
    kj = lax.broadcasted_iota(jnp.int32, (N_BACK, 2 * N_BACK), 1)
    first_prev = jnp.where(n > 0, 0, N_BACK)
    lo = jnp.where(kj < N_BACK, qi + first_prev, N_BACK)
    hi = jnp.where(kj < N_BACK, N_BACK - 1, qi + N_BACK)
    mask = jnp.logical_and(kj >= lo, kj <= hi)

    def stream(r, carry):
        rows = pl.ds(r, N_BACK, stride=dil) if dil > 1 else slice(None)
        for c in range(halves):
            lanes = slice(c * LANE_TILE, (c + 1) * LANE_TILE)
            qs[:, lanes] = q_refs[c][rows, :]
            ks[0:N_BACK, lanes] = kp_refs[c][rows, :]
            ks[N_BACK:, lanes] = kc_refs[c][rows, :]
            vs[0:N_BACK, lanes] = vp_refs[c][rows, :]
            vs[N_BACK:, lanes] = vc_refs[c][rows, :]
        sls = [slice(h * HEAD_DIM, (h + 1) * HEAD_DIM) for h in range(HEADS_PER_GROUP)]
        s = [jnp.where(mask, _dot1(qs[:, sl], ks[:, sl], NT), -jnp.inf) for sl in sls]
        m = [jnp.max(t, axis=-1, keepdims=True) for t in s]
        p = [jnp.exp(t - mm) for t, mm in zip(s, m)]
        den = [jnp.sum(t, axis=-1, keepdims=True) for t in p]
        num = [_dot1(t, vs[:, sl]) for t, sl in zip(p, sls)]
        for h, sl in enumerate(sls):
            stat_s[0][:, sl] = num[h] / den[h]
            stat_s[1][:, sl] = jnp.broadcast_to(m[h] + jnp.log(den[h]), (N_BACK, HEAD_DIM))
        for i in range(N_STATS):
            if dil > 1:
                for c in range(halves):
                    stage[i * halves + c][rows, :] = stat_s[i][:, c * LANE_TILE:(c + 1) * LANE_TILE]
            else:
                out_refs[i][...] = stat_s[i][...]
        return carry

    if dil > 1:
        lax.fori_loop(0, dil, stream, 0)
        for i in range(N_STATS):
            for c in range(halves):
                out_refs[i][:, c * LANE_TILE:(c + 1) * LANE_TILE] = stage[i * halves + c][...]
    else:
        stream(0, 0)


def _attn_prompt(pa3, g):
    bsz, seq, _ = pa3.shape
    dil = GROUP_DILATIONS[g]
    rows = N_BACK * dil
    nb = seq // rows
    halves = A_WIDTH // LANE_TILE

    def specs(slab, prev):
        def one(c):
            col = (3 * g + slab) * halves + c
            if prev:
                return pl.BlockSpec((None, rows, LANE_TILE), lambda b, n: (b, jnp.maximum(n - 1, 0), col))
            return pl.BlockSpec((None, rows, LANE_TILE), lambda b, n: (b, n, col))
        return [one(c) for c in range(halves)]

    in_specs = specs(0, False) + specs(1, True) + specs(1, False) + specs(2, True) + specs(2, False)
    out_spec = pl.BlockSpec((None, rows, A_WIDTH), lambda b, n: (b, n, 0))
    out_sds = jax.ShapeDtypeStruct((bsz, seq, A_WIDTH), F32)
    scratch = [pltpu.VMEM((N_BACK, A_WIDTH), F32), pltpu.VMEM((2 * N_BACK, A_WIDTH), F32),
               pltpu.VMEM((2 * N_BACK, A_WIDTH), F32)] + [pltpu.VMEM((N_BACK, A_WIDTH), F32)] * N_STATS
    if dil > 1:
        scratch += [pltpu.VMEM((rows, LANE_TILE), F32)] * (N_STATS * halves)
    outs = pl.pallas_call(
        functools.partial(_attn_kernel, dil=dil),
        grid=(bsz, nb),
        in_specs=in_specs,
        out_specs=[out_spec] * N_STATS,
        out_shape=[out_sds] * N_STATS,
        scratch_shapes=scratch,
        compiler_params=_cparams("parallel", "arbitrary"),
        name=f"attn_prompt_g{g}",
    )(*([pa3] * len(in_specs)))
    return [o.reshape(bsz * seq, A_WIDTH) for o in outs]


def _sample_cache_kernel(pat_ref, c0_ref, c1_ref, c2_ref, o0_ref, o1_ref, o2_ref, oa_ref):
    b = pl.program_id(0)
    n_b = pat_ref.shape[1]

    @pl.when(b == 0)
    def _():
        oa_ref[...] = jnp.zeros_like(oa_ref)

    lane_b = lax.broadcasted_iota(jnp.int32, pat_ref.shape, 1) == b
    col = jnp.sum(jnp.where(lane_b, pat_ref[...], 0.0), axis=1, keepdims=True)
    c_refs = (c0_ref, c1_ref, c2_ref)
    o_refs = (o0_ref, o1_ref, o2_ref)
    stats = []
    for g in range(N_GROUPS):
        dil = GROUP_DILATIONS[g]
        win = GROUP_WINDOWS[g]
        base = g * 3 * A_WIDTH
        lane = lax.broadcasted_iota(jnp.int32, (HEAD_DIM, win), 1)
        last = lane == win - 1
        live = lax.broadcasted_iota(jnp.int32, (1, win), 1) % dil == 0
        for h in range(HEADS_PER_GROUP):
            o = h * HEAD_DIM
            q = col[base + o:base + o + HEAD_DIM]
            kn = col[base + A_WIDTH + o:base + A_WIDTH + o + HEAD_DIM]
            vn = col[base + 2 * A_WIDTH + o:base + 2 * A_WIDTH + o + HEAD_DIM]
            kt = c_refs[g][0, 0, h]
            vt = c_refs[g][0, 1, h]
            o_refs[g][0, 0, h] = jnp.where(last, kn, pltpu.roll(kt, win - 1, axis=1))
            o_refs[g][0, 1, h] = jnp.where(last, vn, pltpu.roll(vt, win - 1, axis=1))
            s_c = jnp.where(live, jnp.sum(kt * q, axis=0, keepdims=True), -jnp.inf)
            s_n = jnp.sum(kn * q, axis=0, keepdims=True)
            m = jnp.maximum(jnp.max(s_c, axis=1, keepdims=True), s_n)
            p_c = jnp.exp(s_c - m)
            p_n = jnp.exp(s_n - m)
            den = jnp.sum(p_c, axis=1, keepdims=True) + p_n
            num = jnp.sum(vt * p_c, axis=1, keepdims=True) + p_n * vn
            stats.append((num, den, m))
    outs = []
    for h in range(HEADS_PER_GROUP):
        per_g = [stats[g * HEADS_PER_GROUP + h] for g in range(N_GROUPS)]
        mx = functools.reduce(jnp.maximum, [m for _, _, m in per_g])
        num = sum(n_ * jnp.exp(m - mx) for n_, _, m in per_g)
        den = sum(d_ * jnp.exp(m - mx) for _, d_, m in per_g)
        outs.append(num / den)
    o_col = jnp.concatenate(outs, axis=0)
    lane_o = lax.broadcasted_iota(jnp.int32, (A_WIDTH, n_b), 1) == b
    oa_ref[...] = jnp.where(lane_o, o_col, oa_ref[...])


def _sample_cache(pa_t, caches_t):
    bd = pa_t.shape[1]
    cspecs = [pl.BlockSpec((1, 2, HEADS_PER_GROUP, HEAD_DIM, w), lambda b: (b, 0, 0, 0, 0)) for w in GROUP_WINDOWS]
    return pl.pallas_call(
        _sample_cache_kernel,
        grid=(bd,),
        in_specs=[pl.BlockSpec(pa_t.shape, lambda b: (0, 0))] + cspecs,
        out_specs=cspecs + [pl.BlockSpec((A_WIDTH, bd), lambda b: (0, 0))],
        out_shape=[jax.ShapeDtypeStruct(c.shape, c.dtype) for c in caches_t]
                  + [jax.ShapeDtypeStruct((A_WIDTH, bd), F32)],
        compiler_params=_cparams("arbitrary"),
        name="sample_cache",
    )(pa_t, *caches_t)


def _rwkv_prep_math(pf, shifted, mu_ref, w0_ref, w2_ref, a0_ref, a2_ref, g2_ref, kk_ref, ka_ref, bd_ref):
    z = pf + mu_ref[...] * (shifted - pf)
    c1, c2, c3 = B_WIDTH, 2 * B_WIDTH, 3 * B_WIDTH
    r, k, v = z[:, :c1], z[:, c1:c2], z[:, c2:c3]
    wl = z[:, c3:c3 + DECAY_LORA]
    al = z[:, c3 + DECAY_LORA:c3 + DECAY_LORA + AAA_LORA]
    gl = z[:, c3 + DECAY_LORA + AAA_LORA:]
    xw = w0_ref[...] + _dot1(jnp.tanh(wl), w2_ref[...])
    w_log = -(jnp.maximum(-xw, 0.0) + jnp.log(1.0 + jnp.exp(-jnp.abs(xw)))) - 0.5
    lw = -jnp.exp(w_log)
    a = _sigmoid(a0_ref[...] + _dot1(al, a2_ref[...]))
    g = _dot1(_sigmoid(gl), g2_ref[...])
    kk = k * kk_ref[...]
    bd = bd_ref[...]
    kks = []
    for c in range(B_WIDTH // 128):
        kc = kk[:, c * 128:(c + 1) * 128]
        nrm = jnp.sqrt(_dot_exact_rhs(kc * kc, bd))
        kks.append(kc / jnp.maximum(nrm, 1e-12))
    kkn = jnp.concatenate(kks, axis=1)
    return r, lw, k * (1.0 + (a - 1.0) * ka_ref[...]), v, -kkn, kkn * a, g


def _rwkv_prep_kernel(ps_ref, pv_ref, p0_ref, *refs):
    param_refs, out_refs = refs[:9], refs[9:]
    i = pl.program_id(1)
    pf = ps_ref[...]
    prev_row = jnp.where(i == 0, p0_ref[...], pv_ref[7:8, :])
    row = lax.broadcasted_iota(jnp.int32, pf.shape, 0)
    shifted = jnp.where(row == 0, prev_row, pltpu.roll(pf, 1, axis=0))
    for o_ref, val in zip(out_refs, _rwkv_prep_math(pf, shifted, *param_refs)):
        o_ref[...] = val


def _rwkv_prep_t_kernel(ps_ref, prev_ref, *refs):
    param_refs, out_refs = refs[:9], refs[9:]
    for o_ref, val in zip(out_refs, _rwkv_prep_math(ps_ref[...], prev_ref[...], *param_refs)):
        o_ref[...] = val.T


def _prep_params(p):
    row = lambda a: a.reshape(1, -1)
    return [row(p["mu_shift"]), row(p["w0"]), p["w2"], row(p["a0"]), p["a2"], p["g2"],
            row(p["k_k"]), row(p["k_a"]), _seg_ones(128, RWKV_HEAD_DIM, 1.0)]


def _rwkv_prep(ps3, prev0, p, tm):
    bsz, t, _ = ps3.shape
    const = lambda b, i: (0, 0)
    params = _prep_params(p)
    out_spec = pl.BlockSpec((None, tm, B_WIDTH), lambda b, i: (b, i, 0))
    out_sds = jax.ShapeDtypeStruct((bsz, t, B_WIDTH), F32)
    return pl.pallas_call(
        _rwkv_prep_kernel,
        grid=(bsz, t // tm),
        in_specs=[
            pl.BlockSpec((None, tm, SHIFT_WIDTH), lambda b, i: (b, i, 0)),
            pl.BlockSpec((None, 8, SHIFT_WIDTH), lambda b, i: (b, jnp.maximum(i * (tm // 8) - 1, 0), 0)),
            pl.BlockSpec((None, 1, SHIFT_WIDTH), lambda b, i: (b, 0, 0)),
        ] + [pl.BlockSpec(a.shape, const) for a in params],
        out_specs=[out_spec] * 7,
        out_shape=[out_sds] * 7,
        compiler_params=_cparams("parallel", "arbitrary"),
        name="rwkv_prep",
    )(ps3, ps3, prev0, *params)


def _rwkv_prep_t(ps2, prev2, p):
    bd = ps2.shape[0]
    params = _prep_params(p)
    full = lambda a: pl.BlockSpec(a.shape, lambda i: (0, 0))
    out_sds = jax.ShapeDtypeStruct((B_WIDTH, bd), F32)
    return pl.pallas_call(
        _rwkv_prep_t_kernel,
        grid=(1,),
        in_specs=[full(ps2), full(prev2)] + [full(a) for a in params],
        out_specs=[pl.BlockSpec((B_WIDTH, bd), lambda i: (0, 0))] * 7,
        out_shape=[out_sds] * 7,
        compiler_params=_cparams("arbitrary"),
        name="rwkv_prep_t",
    )(ps2, prev2, *params)


def _rwkv_scan_kernel(r_ref, lw_ref, k_ref, v_ref, a_ref, b_ref, g_ref, rk_ref, lnw_ref, lnb_ref,
                      tril_ref, o_ref, s_ref):
    n_b, chunk, _ = r_ref.shape

    @pl.when(pl.program_id(1) == 0)
    def _():
        s_ref[...] = jnp.zeros_like(s_ref)

    ti = lax.broadcasted_iota(jnp.int32, (chunk, 2 * chunk), 0)
    si = lax.broadcasted_iota(jnp.int32, (chunk, 2 * chunk), 1)
    si = jnp.where(si >= chunk, si - chunk, si)
    strict = si < ti
    incl = si <= ti
    eye = jnp.where(lax.broadcasted_iota(jnp.int32, (chunk, chunk), 0)
                    == lax.broadcasted_iota(jnp.int32, (chunk, chunk), 1), 1.0, 0.0).astype(F32)
    n_sq = max(chunk.bit_length() - 2, 0)

    sls = [slice(h * RWKV_HEAD_DIM, (h + 1) * RWKV_HEAD_DIM) for h in range(RWKV_HEADS)]

    chains, ar, bk, v_h, g_last = [], [], [], [], []
    for bi in range(n_b):
        lw = lw_ref[bi]
        cum = _dot_exact_lhs(tril_ref[...], lw)
        e_in = jnp.exp(cum)
        e_inv = jnp.exp(-cum)
        at = a_ref[bi] * jnp.exp(cum - lw)
        rt = r_ref[bi] * e_in
        bt = b_ref[bi] * e_inv
        kt = k_ref[bi] * e_inv
        v = v_ref[bi]
        for h, sl in enumerate(sls):
            chains.append((bi, h))
            ar.append(jnp.concatenate([at[:, sl], rt[:, sl]], axis=0))
            bk.append(jnp.concatenate([bt[:, sl], kt[:, sl]], axis=0))
            v_h.append(v[:, sl])
            g_last.append(e_in[chunk - 1:chunk, sl])
    n = range(len(chains))
    s_old = [s_ref[bi, h] for bi, h in chains]
    m_all = [_dot1(ar[c], bk[c], NT) for c in n]
    gs = [_dot1(ar[c], s_old[c], NT) for c in n]
    a_m = [jnp.where(strict, m_all[c][:chunk], 0.0) for c in n]
    r_m = [jnp.where(incl, m_all[c][chunk:], 0.0) for c in n]
    pw = [a_m[c][:, :chunk] for c in n]
    x = [eye + pw[c] for c in n]
    for _ in range(n_sq):
        pw = [_dot1(pw[c], pw[c]) for c in n]
        x = [x[c] + _dot1(x[c], pw[c]) for c in n]
    av = [_dot1(a_m[c][:, chunk:], v_h[c]) for c in n]
    u = [_dot1(x[c], gs[c][:chunk] + av[c]) for c in n]
    uv = [jnp.concatenate([u[c], v_h[c]], axis=0) for c in n]
    y = [gs[c][chunk:] + _dot1(r_m[c], uv[c]) for c in n]
    for c, (bi, h) in enumerate(chains):
        s_ref[bi, h] = (s_old[c] + _dot3(uv[c], bk[c], TN)) * g_last[c]
    for c, (bi, h) in enumerate(chains):
        sl = sls[h]
        mu = jnp.mean(y[c], axis=-1, keepdims=True)
        var = jnp.mean(jnp.square(y[c] - mu), axis=-1, keepdims=True)
        yn = (y[c] - mu) * lax.rsqrt(var + GN_EPS) * lnw_ref[:, sl] + lnb_ref[:, sl]
        bonus = jnp.sum(r_ref[bi, :, sl] * k_ref[bi, :, sl] * rk_ref[:, sl], axis=-1, keepdims=True) * v_h[c]
        o_ref[bi, :, sl] = (yn + bonus) * g_ref[bi, :, sl]


def _rwkv_scan(streams, p, chunk):
    bsz, t, _ = streams[0].shape
    n_b = SCAN_BATCH if bsz % SCAN_BATCH == 0 else 1
    row = lambda a: a.reshape(1, B_WIDTH)
    tril = (jnp.arange(chunk)[None, :] <= jnp.arange(chunk)[:, None]).astype(BF16)
    const = lambda b, c: (0, 0)
    blk = pl.BlockSpec((n_b, chunk, B_WIDTH), lambda b, c: (b, c, 0))
    st_shape = (bsz, RWKV_HEADS, RWKV_HEAD_DIM, RWKV_HEAD_DIM)
    st = pl.BlockSpec((n_b,) + st_shape[1:], lambda b, c: (b, 0, 0, 0))
    vec = pl.BlockSpec((1, B_WIDTH), const)
    return pl.pallas_call(
        _rwkv_scan_kernel,
        grid=(bsz // n_b, t // chunk),
        in_specs=[blk] * 7 + [vec, vec, vec, pl.BlockSpec((chunk, chunk), const)],
        out_specs=[blk, st],
        out_shape=[jax.ShapeDtypeStruct((bsz, t, B_WIDTH), F32), jax.ShapeDtypeStruct(st_shape, F32)],
        compiler_params=_cparams("parallel", "arbitrary"),
        name="rwkv_scan",
    )(*streams, row(p["r_k"]), row(p["ln_x_w"]), row(p["ln_x_b"]), tril)


def _rwkv_step_kernel(r_ref, lw_ref, k_ref, v_ref, a_ref, b_ref, g_ref, rk_ref, lnw_ref, lnb_ref, s_ref,
                      o_ref, so_ref, y_ref):
    w = jnp.exp(lw_ref[...])
    a, b, k, r = a_ref[...], b_ref[...], k_ref[...], r_ref[...]

    def value_row(i, carry):
        s = s_ref[i]
        sa = jnp.sum(s * a, axis=0, keepdims=True)
        s_new = s * w + sa * b + v_ref[pl.ds(i, 1), :] * k
        so_ref[i] = s_new
        y_ref[pl.ds(i, 1), :] = jnp.sum(s_new * r, axis=0, keepdims=True)
        return carry

    lax.fori_loop(0, RWKV_HEAD_DIM, value_row, 0, unroll=4)
    y = y_ref[...]
    mu = jnp.mean(y, axis=0, keepdims=True)
    var = jnp.mean(jnp.square(y - mu), axis=0, keepdims=True)
    yn = (y - mu) * lax.rsqrt(var + GN_EPS) * lnw_ref[...] + lnb_ref[...]
    bonus = jnp.sum(r_ref[...] * k_ref[...] * rk_ref[...], axis=0, keepdims=True) * v_ref[...]
    o_ref[...] = (yn + bonus) * g_ref[...]


def _rwkv_step(streams_t, s_t, p):
    bd = s_t.shape[-1]
    n = RWKV_HEAD_DIM
    col = lambda a: a.reshape(B_WIDTH, 1)
    vec = pl.BlockSpec((n, bd), lambda h: (h, 0))
    par = pl.BlockSpec((n, 1), lambda h: (h, 0))
    st = pl.BlockSpec((None, n, n, bd), lambda h: (h, 0, 0, 0))
    return pl.pallas_call(
        _rwkv_step_kernel,
        grid=(RWKV_HEADS,),
        in_specs=[vec] * 7 + [par] * 3 + [st],
        out_specs=[vec, st],
        out_shape=[jax.ShapeDtypeStruct((B_WIDTH, bd), F32), jax.ShapeDtypeStruct(s_t.shape, F32)],
        scratch_shapes=[pltpu.VMEM((n, bd), F32)],
        compiler_params=_cparams("parallel"),
        name="rwkv_step",
    )(*streams_t, col(p["r_k"]), col(p["ln_x_w"]), col(p["ln_x_b"]), s_t)


def _finish_kernel(x_ref, *refs, n_stats):
    stat_refs = refs[:n_stats]
    ob_ref, g_ref, wpa_ref, wpb_ref, wo_ref, g2_ref, rw_ref, rb_ref, h_ref, hn_ref, lg_ref = refs[n_stats:]
    if n_stats == 1:
        o_a = stat_refs[0][...]
    else:
        outs, lses = stat_refs[0::N_STATS], stat_refs[1::N_STATS]
        mx = functools.reduce(jnp.maximum, [l[...] for l in lses])
        es = [jnp.exp(l[...] - mx) for l in lses]
        o_a = sum(o[...] * e for o, e in zip(outs, es)) / sum(es)
    br_a = _dot1(o_a, wpa_ref[...])
    br_b = _dot1(ob_ref[...], wpb_ref[...])
    mix = g_ref[:, :D_MODEL] * br_a + g_ref[:, D_MODEL:] * br_b
    h = x_ref[...] + _dot1(mix, wo_ref[...])
    h_ref[...] = h
    hn = h * lax.rsqrt(jnp.mean(h * h, axis=-1, keepdims=True) + RMS_EPS) * g2_ref[...]
    hn_ref[...] = hn
    lg_ref[...] = _dot3(hn, rw_ref[...]) + rb_ref[...]


def _finish(x2, stats, o_b, gates, wb, tm):
    n = x2.shape[0]
    const = lambda i: (0, 0)
    rows = lambda w: pl.BlockSpec((tm, w), lambda i: (i, 0))
    params = [wb["w_pa"], wb["w_pb"], wb["w_o"], wb["norm2_g"], wb["router_w"], wb["router_b"]]
    return pl.pallas_call(
        functools.partial(_finish_kernel, n_stats=len(stats)),
        grid=(n // tm,),
        in_specs=[rows(D_MODEL)] + [rows(A_WIDTH)] * len(stats) + [rows(B_WIDTH), rows(2 * D_MODEL)]
                 + [pl.BlockSpec(a.shape, const) for a in params],
        out_specs=[rows(D_MODEL), rows(D_MODEL), rows(LOGIT_PAD)],
        out_shape=[jax.ShapeDtypeStruct((n, D_MODEL), F32), jax.ShapeDtypeStruct((n, D_MODEL), F32),
                   jax.ShapeDtypeStruct((n, LOGIT_PAD), F32)],
        compiler_params=_cparams("parallel"),
        name="finish",
    )(x2, *stats, o_b, gates, *params)


def _row(ref, i):
    return ref.at[pl.ds(i, 1)]


def _moe_kernel(blk_e_ref, nused_ref, tok_ref, tok_next_ref, tok_next2_ref, hn_hbm, wgu32_ref, bgu_ref, wd32_ref,
                bdn_ref, ys_ref, xbuf_a, xbuf_b, xbuf_c, wgu_ref, wd_ref, fence_buf, sem, fence_sem):
    i = pl.program_id(0)
    nused = nused_ref[0]
    n_chunks = D_FF // COL_CHUNK
    bufs = (xbuf_a, xbuf_b, xbuf_c)
    n_bufs = len(bufs)

    def issue(idx_ref, s):
        for r in range(MOE_BLOCK):
            pltpu.make_async_copy(_row(hn_hbm, idx_ref[0, 0, r]), _row(bufs[s], r), sem.at[s]).start(priority=r % 2)

    def drain(s):
        for _ in range(MOE_BLOCK):
            pltpu.make_async_copy(_row(hn_hbm, 0), _row(bufs[s], 0), sem.at[s]).wait()

    @pl.when(jnp.logical_and(i == 0, nused > 0))
    def _():
        issue(tok_ref, 0)
        issue(tok_next_ref, 1)

    new_expert = jnp.logical_or(i == 0, blk_e_ref[i] != blk_e_ref[jnp.maximum(i - 1, 0)])

    @pl.when(jnp.logical_and(i < nused, new_expert))
    def _():
        wgu_ref[...] = wgu32_ref[...].astype(BF16)
        wd_ref[...] = wd32_ref[...].astype(BF16)

    def block(s):
        drain(s)
        x = bufs[s][...].astype(BF16)
        fence_buf[0] = jnp.zeros(fence_buf.shape[1:], F32)
        fence = pltpu.make_async_copy(fence_buf.at[0], fence_buf.at[1], fence_sem)
        fence.start()
        issue(tok_next2_ref, (s + 2) % n_bufs)
        acc = jnp.zeros((MOE_BLOCK, D_MODEL), F32)
        for c in range(n_chunks):
            if c == n_chunks // 2:
                fence.wait()
            lo = c * COL_CHUNK
            gate = _dg(x, wgu_ref[:, lo:lo + COL_CHUNK], NN) + bgu_ref[:, lo:lo + COL_CHUNK]
            up = _dg(x, wgu_ref[:, D_FF + lo:D_FF + lo + COL_CHUNK], NN) + bgu_ref[:, D_FF + lo:D_FF + lo + COL_CHUNK]
            gate = jnp.minimum(gate, SWIGLU_LIMIT)
            up = jnp.clip(up, -SWIGLU_LIMIT, SWIGLU_LIMIT)
            hid = (up + 1.0) * gate * _sigmoid(gate * SWIGLU_ALPHA)
            acc = acc + _dg(hid.astype(BF16), wd_ref[lo:lo + COL_CHUNK, :], NN)
        ys_ref[...] = acc + bdn_ref[...]

        @pl.when(i + 1 >= nused)
        def _():
            drain((s + 1) % n_bufs)
            drain((s + 2) % n_bufs)

    for s in range(n_bufs):
        pl.when(jnp.logical_and(i < nused, i % n_bufs == s))(functools.partial(block, s))

    @pl.when(i >= nused)
    def _():
        ys_ref[...] = jnp.zeros_like(ys_ref)


def _moe_blocks(hn, row_tok, blk_e, nused, wb):
    n_blocks = row_tok.shape[0]
    last = n_blocks - 1
    grid_spec = pltpu.PrefetchScalarGridSpec(
        num_scalar_prefetch=2,
        grid=(n_blocks,),
        in_specs=[
            pl.BlockSpec((1, 1, MOE_BLOCK), lambda i, be, nu: (i, 0, 0), memory_space=pltpu.SMEM),
            pl.BlockSpec((1, 1, MOE_BLOCK), lambda i, be, nu: (jnp.minimum(i + 1, last), 0, 0),
                         memory_space=pltpu.SMEM),
            pl.BlockSpec((1, 1, MOE_BLOCK), lambda i, be, nu: (jnp.minimum(i + 2, last), 0, 0),
                         memory_space=pltpu.SMEM),
            pl.BlockSpec(memory_space=pl.ANY),
            pl.BlockSpec((None, D_MODEL, 2 * D_FF), lambda i, be, nu: (be[i], 0, 0)),
            pl.BlockSpec((None, 1, 2 * D_FF), lambda i, be, nu: (be[i], 0, 0)),
            pl.BlockSpec((None, D_FF, D_MODEL), lambda i, be, nu: (be[i], 0, 0)),
            pl.BlockSpec((None, 1, D_MODEL), lambda i, be, nu: (be[i], 0, 0)),
        ],
        out_specs=pl.BlockSpec((MOE_BLOCK, D_MODEL), lambda i, be, nu: (i, 0)),
        scratch_shapes=[pltpu.VMEM((MOE_BLOCK, D_MODEL), F32)] * 3
                       + [pltpu.VMEM((D_MODEL, 2 * D_FF), BF16), pltpu.VMEM((D_FF, D_MODEL), BF16),
                          pltpu.VMEM((2, 8, LANE_TILE), F32), pltpu.SemaphoreType.DMA((3,)),
                          pltpu.SemaphoreType.DMA(())],
    )
    return pl.pallas_call(
        _moe_kernel,
        grid_spec=grid_spec,
        out_shape=jax.ShapeDtypeStruct((n_blocks * MOE_BLOCK, D_MODEL), F32),
        compiler_params=_cparams("arbitrary"),
        name="moe_blocks",
    )(blk_e, nused, row_tok, row_tok, row_tok, hn, wb["w_gu"], wb["b_gu"], wb["w_down"], wb["b_down"])


def _combine_kernel(pos_ref, pos_next_ref, h_ref, gate_ref, ys_hbm, y_ref, gbuf, sem):
    tm = h_ref.shape[0]
    rows = TOP_K * tm
    i = pl.program_id(0)
    n = pl.num_programs(0)
    slot = i % 2

    def issue(idx_ref, s):
        for kk in range(rows):
            pltpu.make_async_copy(_row(ys_hbm, idx_ref[0, 0, kk]), gbuf.at[s, pl.ds(kk, 1)], sem.at[s]).start(
                priority=kk % 2)

    @pl.when(i == 0)
    def _():
        issue(pos_ref, 0)

    @pl.when(i + 1 < n)
    def _():
        issue(pos_next_ref, 1 - slot)

    for kk in range(rows):
        pltpu.make_async_copy(_row(ys_hbm, 0), gbuf.at[slot, pl.ds(0, 1)], sem.at[slot]).wait()
    gv = gate_ref[...]
    ff = gv[:, 0:1] * gbuf[slot, 0:tm, :]
    for j in range(1, TOP_K):
        ff = ff + gv[:, j:j + 1] * gbuf[slot, j * tm:(j + 1) * tm, :]
    y_ref[...] = h_ref[...] + ff


def _combine(h, gates_pad, pos, ys):
    n = h.shape[0]
    tm = min(COMBINE_TOKENS, n)
    steps = n // tm
    rows = TOP_K * tm
    pos3 = pos.reshape(steps, tm, TOP_K).transpose(0, 2, 1).reshape(steps, 1, rows)
    return pl.pallas_call(
        _combine_kernel,
        grid=(steps,),
        in_specs=[
            pl.BlockSpec((1, 1, rows), lambda i: (i, 0, 0), memory_space=pltpu.SMEM),
            pl.BlockSpec((1, 1, rows), lambda i: (jnp.minimum(i + 1, steps - 1), 0, 0), memory_space=pltpu.SMEM),
            pl.BlockSpec((tm, D_MODEL), lambda i: (i, 0)),
            pl.BlockSpec((tm, LOGIT_PAD), lambda i: (i, 0)),
            pl.BlockSpec(memory_space=pl.ANY),
        ],
        out_specs=pl.BlockSpec((tm, D_MODEL), lambda i: (i, 0)),
        out_shape=jax.ShapeDtypeStruct((n, D_MODEL), F32),
        scratch_shapes=[pltpu.VMEM((2, rows, D_MODEL), F32), pltpu.SemaphoreType.DMA((2,))],
        compiler_params=_cparams("arbitrary"),
        name="moe_combine",
    )(pos3, pos3, h, gates_pad, ys)


def _route(logits):
    n = logits.shape[0]
    nk = n * TOP_K
    top_v, top_e = lax.top_k(logits, TOP_K)
    gates = jax.nn.softmax(top_v, axis=-1)
    flat_e = top_e.reshape(nk)
    onehot = (flat_e[:, None] == jnp.arange(N_EXPERTS, dtype=top_e.dtype)[None, :]).astype(jnp.int32)
    csum = jnp.cumsum(onehot, axis=0)
    counts = csum[-1]
    starts = jnp.cumsum(counts) - counts
    pcounts = (counts + MOE_BLOCK - 1) // MOE_BLOCK * MOE_BLOCK
    pends = jnp.cumsum(pcounts)
    pstarts = pends - pcounts
    pos = jnp.sum(onehot * (csum - 1 + pstarts[None, :]), axis=1).astype(jnp.int32)
    n_blocks = -(-nk // MOE_BLOCK) + N_EXPERTS
    blk_start = jnp.arange(n_blocks, dtype=jnp.int32) * MOE_BLOCK
    blk_e = jnp.minimum(jnp.sum(blk_start[:, None] >= pends[None, :], axis=1), N_EXPERTS - 1).astype(jnp.int32)
    nused = (pends[-1:] // MOE_BLOCK).astype(jnp.int32)
    order = jnp.argsort(flat_e)
    e_row = jnp.repeat(blk_e, MOE_BLOCK)
    rank = jnp.arange(n_blocks * MOE_BLOCK, dtype=jnp.int32) - pstarts[e_row]
    src = jnp.minimum(starts[e_row] + rank, nk - 1)
    row_tok = jnp.where(rank < counts[e_row], order[src] // TOP_K, 0).astype(jnp.int32)
    gates_pad = jnp.pad(gates, ((0, 0), (0, LOGIT_PAD - TOP_K)))
    return pos, row_tok.reshape(n_blocks, 1, MOE_BLOCK), blk_e, nused, gates_pad


def _moe(h, hn, logits_pad, wb):
    pos, row_tok, blk_e, nused, gates_pad = _route(logits_pad[:, :N_EXPERTS])
    ys = _moe_blocks(hn, row_tok, blk_e, nused, wb)
    return _combine(h, gates_pad, pos, ys)


def kernel(x_prompt, x_sample, cache_kv_w128, cache_kv_w512, cache_kv_w2048, state_wkv, state_shift,
           norm1_g, w_in, q_norm_g, k_norm_g, mu_shift, w0, w2, a0, a2, g2, k_k, k_a, r_k,
           ln_x_w, ln_x_b, w_pa, w_pb, w_o, norm2_g, router_w, router_b, w_gu, b_gu, w_down, b_down):
    bp, seq, _ = x_prompt.shape
    bd, t_s, _ = x_sample.shape
    assert t_s == 1
    rw = dict(mu_shift=mu_shift, w0=w0, w2=w2, a0=a0, a2=a2, g2=g2, k_k=k_k, k_a=k_a, r_k=r_k,
              ln_x_w=ln_x_w, ln_x_b=ln_x_b)
    wb = dict(
        w_pa=w_pa.astype(BF16), w_pb=w_pb.astype(BF16), w_o=w_o.astype(BF16),
        norm2_g=norm2_g.reshape(1, D_MODEL),
        router_w=jnp.pad(router_w, ((0, 0), (0, LOGIT_PAD - N_EXPERTS))),
        router_b=jnp.pad(router_b, (0, LOGIT_PAD - N_EXPERTS)).reshape(1, LOGIT_PAD),
        w_gu=w_gu.astype(F32), b_gu=b_gu.reshape(N_EXPERTS, 1, 2 * D_FF),
        w_down=w_down.astype(F32), b_down=b_down.reshape(N_EXPERTS, 1, D_MODEL),
    )
    w_in_b = w_in.astype(BF16)
    mult = jnp.stack([jnp.tile(q_norm_g[c // 3] * (HEAD_DIM ** -0.5) if c % 3 == 0 else
                               (k_norm_g[c // 3] if c % 3 == 1 else jnp.ones((HEAD_DIM,), F32)),
                               HEADS_PER_GROUP) for c in range(A_COLS // COL_CHUNK)])
    bd_head = _seg_ones(COL_CHUNK, HEAD_DIM, 1.0 / HEAD_DIM)

    xp2 = x_prompt.reshape(bp * seq, D_MODEL)
    pa, ps, pg = _in_proj(xp2, norm1_g, w_in_b, mult, bd_head, tm=256)
    pa3 = pa.reshape(bp, seq, A_COLS)
    stats = []
    kv_p = []
    for g in range(N_GROUPS):
        stats += _attn_prompt(pa3, g)
        keep = min(GROUP_WINDOWS[g], seq)
        lo = g * 3 * A_WIDTH + A_WIDTH
        kv_p.append(pa3[:, seq - keep:, lo:lo + 2 * A_WIDTH].reshape(bp, keep, 2, HEADS_PER_GROUP, HEAD_DIM))
    ps3 = ps.reshape(bp, seq, SHIFT_WIDTH)
    streams = _rwkv_prep(ps3, jnp.zeros((bp, 1, SHIFT_WIDTH), F32), rw, tm=512)
    o_b, wkv_p = _rwkv_scan(streams, rw, chunk=SCAN_CHUNK)
    h, hn, lg = _finish(xp2, stats, o_b.reshape(bp * seq, B_WIDTH), pg, wb, tm=256)
    y_prompt = _moe(h, hn, lg, wb).reshape(bp, seq, D_MODEL)
    shift_p = ps3[:, -1:]

    caches = (cache_kv_w128, cache_kv_w512, cache_kv_w2048)
    xs2 = x_sample.reshape(bd, D_MODEL)
    pa_s, ps_s, pg_s = _in_proj(xs2, norm1_g, w_in_b, mult, bd_head, tm=bd)
    caches_t = [jnp.transpose(c.astype(F32), (0, 2, 3, 4, 1)) for c in caches]
    *kv_t, oa_t = _sample_cache(pa_s.T, caches_t)
    kv_s = [jnp.transpose(t, (0, 4, 1, 2, 3)).astype(c.dtype) for t, c in zip(kv_t, caches)]
    streams_t = _rwkv_prep_t(ps_s, state_shift.reshape(bd, SHIFT_WIDTH).astype(F32), rw)
    ob_t, s_t = _rwkv_step(streams_t, jnp.transpose(state_wkv.astype(F32), (1, 2, 3, 0)), rw)
    wkv_s = jnp.transpose(s_t, (3, 0, 1, 2))
    h_s, hn_s, lg_s = _finish(xs2, [oa_t.T], ob_t.T, pg_s, wb, tm=bd)
    y_sample = _moe(h_s, hn_s, lg_s, wb).reshape(bd, 1, D_MODEL)
    shift_s = ps_s.reshape(bd, 1, SHIFT_WIDTH)

    return (y_prompt, y_sample, kv_p[0], kv_p[1], kv_p[2], wkv_p.astype(state_wkv.dtype), shift_p,
            kv_s[0], kv_s[1], kv_s[2], wkv_s.astype(state_wkv.dtype), shift_s.astype(state_shift.dtype))
```

```python
import functools

import jax
import jax.numpy as jnp
from jax import lax
from jax.experimental import pallas as pl
from jax.experimental.pallas import tpu as pltpu

F32 = jnp.float32
BF16 = jnp.bfloat16

D_MODEL = 1024
GROUP_WINDOWS = (128, 512, 2048)
GROUP_DILATIONS = (1, 4, 16)
N_GROUPS = 3
HEADS_PER_GROUP = 4
HEAD_DIM = 64
A_WIDTH = HEADS_PER_GROUP * HEAD_DIM
A_COLS = N_GROUPS * 3 * A_WIDTH
RWKV_HEADS = 8
RWKV_HEAD_DIM = 64
B_WIDTH = RWKV_HEADS * RWKV_HEAD_DIM
DECAY_LORA = 64
AAA_LORA = 64
GATE_LORA = 128
SHIFT_WIDTH = 3 * B_WIDTH + DECAY_LORA + AAA_LORA + GATE_LORA
D_IN = A_COLS + SHIFT_WIDTH + 2 * D_MODEL
N_EXPERTS = 32
TOP_K = 4
D_FF = D_MODEL
SWIGLU_LIMIT = 7.0
SWIGLU_ALPHA = 1.702
MOE_BLOCK = 256
RMS_EPS = 1e-6
GN_EPS = 64e-5

N_BACK = 128
COL_CHUNK = 256
SCAN_CHUNK = 64
SCAN_BATCH = 2
LANE_TILE = 128
N_STATS = 2
LOGIT_PAD = LANE_TILE
COMBINE_TOKENS = 256
VMEM_LIMIT = 48 * 1024 * 1024


def _cparams(*sem):
    return pltpu.CompilerParams(dimension_semantics=sem, vmem_limit_bytes=VMEM_LIMIT)


def _sigmoid(x):
    return 1.0 / (1.0 + jnp.exp(-x))


def _split(x):
    hi = x.astype(BF16)
    lo = (x - hi.astype(F32)).astype(BF16)
    return hi, lo


def _dg(a, b, dims):
    return lax.dot_general(a, b, (dims, ((), ())), preferred_element_type=F32)


NN = ((1,), (0,))
NT = ((1,), (1,))
TN = ((0,), (0,))


def _dot1(a, b, dims=NN):
    return _dg(a.astype(BF16), b.astype(BF16), dims)


def _dot3(a, b, dims=NN):
    ah, al = _split(a)
    bh, bl = _split(b)
    return _dg(ah, bh, dims) + _dg(al, bh, dims) + _dg(ah, bl, dims)


def _split3(x):
    hi = x.astype(BF16)
    r1 = x - hi.astype(F32)
    mid = r1.astype(BF16)
    lo = (r1 - mid.astype(F32)).astype(BF16)
    return hi, mid, lo


def _dot_exact_rhs(a, b_bf16):
    return sum(_dg(t, b_bf16, NN) for t in _split3(a))


def _dot_exact_lhs(l_bf16, x):
    return sum(_dg(l_bf16, t, NN) for t in _split3(x))


def _seg_ones(width, seg, scale):
    i = jnp.arange(width)[:, None] // seg
    j = jnp.arange(width)[None, :] // seg
    return jnp.where(i == j, scale, 0.0).astype(BF16)


def _in_proj_kernel(x_ref, g1_ref, w_ref, mult_ref, bd_ref, pa_ref, ps_ref, pg_ref):
    x = x_ref[...]
    xn = x * lax.rsqrt(jnp.mean(x * x, axis=-1, keepdims=True) + RMS_EPS) * g1_ref[...]
    xb = xn.astype(BF16)
    bd = bd_ref[...]
    for c in range(A_COLS // COL_CHUNK):
        acc = _dg(xb, w_ref[:, c * COL_CHUNK:(c + 1) * COL_CHUNK], NN)
        if c % 3 != 2:
            ms = _dot_exact_rhs(acc * acc, bd)
            acc = acc * lax.rsqrt(ms + RMS_EPS) * mult_ref[c:c + 1, :]
        pa_ref[:, c * COL_CHUNK:(c + 1) * COL_CHUNK] = acc
    for c in range(SHIFT_WIDTH // COL_CHUNK):
        lo = A_COLS + c * COL_CHUNK
        ps_ref[:, c * COL_CHUNK:(c + 1) * COL_CHUNK] = _dg(xb, w_ref[:, lo:lo + COL_CHUNK], NN)
    for c in range(2 * D_MODEL // COL_CHUNK):
        lo = A_COLS + SHIFT_WIDTH + c * COL_CHUNK
        pg_ref[:, c * COL_CHUNK:(c + 1) * COL_CHUNK] = _sigmoid(_dg(xb, w_ref[:, lo:lo + COL_CHUNK], NN)).astype(BF16)


def _in_proj(x2, norm1_g, w_in_b, mult, bd, tm):
    n = x2.shape[0]
    const = lambda i: (0, 0)
    return pl.pallas_call(
        _in_proj_kernel,
        grid=(n // tm,),
        in_specs=[
            pl.BlockSpec((tm, D_MODEL), lambda i: (i, 0)),
            pl.BlockSpec((1, D_MODEL), const),
            pl.BlockSpec((D_MODEL, D_IN), const, pipeline_mode=pl.Buffered(1)),
            pl.BlockSpec(mult.shape, const),
            pl.BlockSpec(bd.shape, const),
        ],
        out_specs=[
            pl.BlockSpec((tm, A_COLS), lambda i: (i, 0)),
            pl.BlockSpec((tm, SHIFT_WIDTH), lambda i: (i, 0)),
            pl.BlockSpec((tm, 2 * D_MODEL), lambda i: (i, 0)),
        ],
        out_shape=[
            jax.ShapeDtypeStruct((n, A_COLS), F32),
            jax.ShapeDtypeStruct((n, SHIFT_WIDTH), F32),
            jax.ShapeDtypeStruct((n, 2 * D_MODEL), BF16),
        ],
        compiler_params=_cparams("parallel"),
        name="in_proj",
    )(x2, norm1_g.reshape(1, D_MODEL), w_in_b, mult, bd)


def _attn_kernel(*refs, dil):
    halves = A_WIDTH // LANE_TILE
    q_refs, kp_refs, kc_refs, vp_refs, vc_refs = (refs[i * halves:(i + 1) * halves] for i in range(5))
    out_refs = refs[5 * halves:5 * halves + N_STATS]
    qs, ks, vs = refs[5 * halves + N_STATS:5 * halves + N_STATS + 3]
    stat_s = refs[5 * halves + N_STATS + 3:5 * halves + 2 * N_STATS + 3]
    stage = refs[5 * halves + 2 * N_STATS + 3:]
    n = pl.program_id(1)
    qi = lax.broadcasted_iota(jnp.int32, (N_BACK, 2 * N_BACK), 0)
    kj = lax.broadcasted_iota(jnp.int32, (N_BACK, 2 * N_BACK), 1)
    first_prev = jnp.where(n > 0, 0, N_BACK)
    lo = jnp.where(kj < N_BACK, qi + first_prev, N_BACK)
    hi = jnp.where(kj < N_BACK, N_BACK - 1, qi + N_BACK)
    mask = jnp.logical_and(kj >= lo, kj <= hi)

    def stream(r, carry):
        rows = pl.ds(r, N_BACK, stride=dil) if dil > 1 else slice(None)
        for c in range(halves):
            lanes = slice(c * LANE_TILE, (c + 1) * LANE_TILE)
            qs[:, lanes] = q_refs[c][rows, :]
            ks[0:N_BACK, lanes] = kp_refs[c][rows, :]
            ks[N_BACK:, lanes] = kc_refs[c][rows, :]
            vs[0:N_BACK, lanes] = vp_refs[c][rows, :]
            vs[N_BACK:, lanes] = vc_refs[c][rows, :]
        sls = [slice(h * HEAD_DIM, (h + 1) * HEAD_DIM) for h in range(HEADS_PER_GROUP)]
        s = [jnp.where(mask, _dot1(qs[:, sl], ks[:, sl], NT), -jnp.inf) for sl in sls]
        m = [jnp.max(t, axis=-1, keepdims=True) for t in s]
        p = [jnp.exp(t - mm) for t, mm in zip(s, m)]
        den = [jnp.sum(t, axis=-1, keepdims=True) for t in p]
        num = [_dot1(t, vs[:, sl]) for t, sl in zip(p, sls)]
        for h, sl in enumerate(sls):
            stat_s[0][:, sl] = num[h] / den[h]
            stat_s[1][:, sl] = jnp.broadcast_to(m[h] + jnp.log(den[h]), (N_BACK, HEAD_DIM))
        for i in range(N_STATS):
            if dil > 1:
                for c in range(halves):
                    stage[i * halves + c][rows, :] = stat_s[i][:, c * LANE_TILE:(c + 1) * LANE_TILE]
            else:
                out_refs[i][...] = stat_s[i][...]
        return carry

    if dil > 1:
        lax.fori_loop(0, dil, stream, 0)
        for i in range(N_STATS):
            for c in range(halves):
                out_refs[i][:, c * LANE_TILE:(c + 1) * LANE_TILE] = stage[i * halves + c][...]
    else:
        stream(0, 0)


def _attn_prompt(pa3, g):
    bsz, seq, _ = pa3.shape
    dil = GROUP_DILATIONS[g]
    rows = N_BACK * dil
    nb = seq // rows
    halves = A_WIDTH // LANE_TILE

    def specs(slab, prev):
        def one(c):
            col = (3 * g + slab) * halves + c
            if prev:
                return pl.BlockSpec((None, rows, LANE_TILE), lambda b, n: (b, jnp.maximum(n - 1, 0), col))
            return pl.BlockSpec((None, rows, LANE_TILE), lambda b, n: (b, n, col))
        return [one(c) for c in range(halves)]

    in_specs = specs(0, False) + specs(1, True) + specs(1, False) + specs(2, True) + specs(2, False)
    out_spec = pl.BlockSpec((None, rows, A_WIDTH), lambda b, n: (b, n, 0))
    out_sds = jax.ShapeDtypeStruct((bsz, seq, A_WIDTH), F32)
    scratch = [pltpu.VMEM((N_BACK, A_WIDTH), F32), pltpu.VMEM((2 * N_BACK, A_WIDTH), F32),
               pltpu.VMEM((2 * N_BACK, A_WIDTH), F32)] + [pltpu.VMEM((N_BACK, A_WIDTH), F32)] * N_STATS
    if dil > 1:
        scratch += [pltpu.VMEM((rows, LANE_TILE), F32)] * (N_STATS * halves)
    outs = pl.pallas_call(
        functools.partial(_attn_kernel, dil=dil),
        grid=(bsz, nb),
        in_specs=in_specs,
        out_specs=[out_spec] * N_STATS,
        out_shape=[out_sds] * N_STATS,
        scratch_shapes=scratch,
        compiler_params=_cparams("parallel", "arbitrary"),
        name=f"attn_prompt_g{g}",
    )(*([pa3] * len(in_specs)))
    return [o.reshape(bsz * seq, A_WIDTH) for o in outs]


def _sample_cache_kernel(pat_ref, c0_ref, c1_ref, c2_ref, o0_ref, o1_ref, o2_ref, oa_ref):
    b = pl.program_id(0)
    n_b = pat_ref.shape[1]

    @pl.when(b == 0)
    def _():
        oa_ref[...] = jnp.zeros_like(oa_ref)

    lane_b = lax.broadcasted_iota(jnp.int32, pat_ref.shape, 1) == b
    col = jnp.sum(jnp.where(lane_b, pat_ref[...], 0.0), axis=1, keepdims=True)
    c_refs = (c0_ref, c1_ref, c2_ref)
    o_refs = (o0_ref, o1_ref, o2_ref)
    stats = []
    for g in range(N_GROUPS):
        dil = GROUP_DILATIONS[g]
        win = GROUP_WINDOWS[g]
        base = g * 3 * A_WIDTH
        lane = lax.broadcasted_iota(jnp.int32, (HEAD_DIM, win), 1)
        last = lane == win - 1
        live = lax.broadcasted_iota(jnp.int32, (1, win), 1) % dil == 0
        for h in range(HEADS_PER_GROUP):
            o = h * HEAD_DIM
            q = col[base + o:base + o + HEAD_DIM]
            kn = col[base + A_WIDTH + o:base + A_WIDTH + o + HEAD_DIM]
            vn = col[base + 2 * A_WIDTH + o:base + 2 * A_WIDTH + o + HEAD_DIM]
            kt = c_refs[g][0, 0, h]
            vt = c_refs[g][0, 1, h]
            o_refs[g][0, 0, h] = jnp.where(last, kn, pltpu.roll(kt, win - 1, axis=1))
            o_refs[g][0, 1, h] = jnp.where(last, vn, pltpu.roll(vt, win - 1, axis=1))
            s_c = jnp.where(live, jnp.sum(kt * q, axis=0, keepdims=True), -jnp.inf)
            s_n = jnp.sum(kn * q, axis=0, keepdims=True)
            m = jnp.maximum(jnp.max(s_c, axis=1, keepdims=True), s_n)
            p_c = jnp.exp(s_c - m)
            p_n = jnp.exp(s_n - m)
            den = jnp.sum(p_c, axis=1, keepdims=True) + p_n
            num = jnp.sum(vt * p_c, axis=1, keepdims=True) + p_n * vn
            stats.append((num, den, m))
    outs = []
    for h in range(HEADS_PER_GROUP):
        per_g = [stats[g * HEADS_PER_GROUP + h] for g in range(N_GROUPS)]
        mx = functools.reduce(jnp.maximum, [m for _, _, m in per_g])
        num = sum(n_ * jnp.exp(m - mx) for n_, _, m in per_g)
        den = sum(d_ * jnp.exp(m - mx) for _, d_, m in per_g)
        outs.append(num / den)
    o_col = jnp.concatenate(outs, axis=0)
    lane_o = lax.broadcasted_iota(jnp.int32, (A_WIDTH, n_b), 1) == b
    oa_ref[...] = jnp.where(lane_o, o_col, oa_ref[...])


def _sample_cache(pa_t, caches_t):
    bd = pa_t.shape[1]
    cspecs = [pl.BlockSpec((1, 2, HEADS_PER_GROUP, HEAD_DIM, w), lambda b: (b, 0, 0, 0, 0)) for w in GROUP_WINDOWS]
    return pl.pallas_call(
        _sample_cache_kernel,
        grid=(bd,),
        in_specs=[pl.BlockSpec(pa_t.shape, lambda b: (0, 0))] + cspecs,
        out_specs=cspecs + [pl.BlockSpec((A_WIDTH, bd), lambda b: (0, 0))],
        out_shape=[jax.ShapeDtypeStruct(c.shape, c.dtype) for c in caches_t]
                  + [jax.ShapeDtypeStruct((A_WIDTH, bd), F32)],
        compiler_params=_cparams("arbitrary"),
        name="sample_cache",
    )(pa_t, *caches_t)


def _rwkv_prep_math(pf, shifted, mu_ref, w0_ref, w2_ref, a0_ref, a2_ref, g2_ref, kk_ref, ka_ref, bd_ref):
    z = pf + mu_ref[...] * (shifted - pf)
    c1, c2, c3 = B_WIDTH, 2 * B_WIDTH, 3 * B_WIDTH
    r, k, v = z[:, :c1], z[:, c1:c2], z[:, c2:c3]
    wl = z[:, c3:c3 + DECAY_LORA]
    al = z[:, c3 + DECAY_LORA:c3 + DECAY_LORA + AAA_LORA]
    gl = z[:, c3 + DECAY_LORA + AAA_LORA:]
    xw = w0_ref[...] + _dot1(jnp.tanh(wl), w2_ref[...])
    w_log = -(jnp.maximum(-xw, 0.0) + jnp.log(1.0 + jnp.exp(-jnp.abs(xw)))) - 0.5
    lw = -jnp.exp(w_log)
    a = _sigmoid(a0_ref[...] + _dot1(al, a2_ref[...]))
    g = _dot1(_sigmoid(gl), g2_ref[...])
    kk = k * kk_ref[...]
    bd = bd_ref[...]
    kks = []
    for c in range(B_WIDTH // 128):
        kc = kk[:, c * 128:(c + 1) * 128]
        nrm = jnp.sqrt(_dot_exact_rhs(kc * kc, bd))
        kks.append(kc / jnp.maximum(nrm, 1e-12))
    kkn = jnp.concatenate(kks, axis=1)
    return r, lw, k * (1.0 + (a - 1.0) * ka_ref[...]), v, -kkn, kkn * a, g


def _rwkv_prep_kernel(ps_ref, pv_ref, p0_ref, *refs):
    param_refs, out_refs = refs[:9], refs[9:]
    i = pl.program_id(1)
    pf = ps_ref[...]
    prev_row = jnp.where(i == 0, p0_ref[...], pv_ref[7:8, :])
    row = lax.broadcasted_iota(jnp.int32, pf.shape, 0)
    shifted = jnp.where(row == 0, prev_row, pltpu.roll(pf, 1, axis=0))
    for o_ref, val in zip(out_refs, _rwkv_prep_math(pf, shifted, *param_refs)):
        o_ref[...] = val


def _rwkv_prep_t_kernel(ps_ref, prev_ref, *refs):
    param_refs, out_refs = refs[:9], refs[9:]
    for o_ref, val in zip(out_refs, _rwkv_prep_math(ps_ref[...], prev_ref[...], *param_refs)):
        o_ref[...] = val.T


def _prep_params(p):
    row = lambda a: a.reshape(1, -1)
    return [row(p["mu_shift"]), row(p["w0"]), p["w2"], row(p["a0"]), p["a2"], p["g2"],
            row(p["k_k"]), row(p["k_a"]), _seg_ones(128, RWKV_HEAD_DIM, 1.0)]


def _rwkv_prep(ps3, prev0, p, tm):
    bsz, t, _ = ps3.shape
    const = lambda b, i: (0, 0)
    params = _prep_params(p)
    out_spec = pl.BlockSpec((None, tm, B_WIDTH), lambda b, i: (b, i, 0))
    out_sds = jax.ShapeDtypeStruct((bsz, t, B_WIDTH), F32)
    return pl.pallas_call(
        _rwkv_prep_kernel,
        grid=(bsz, t // tm),
        in_specs=[
            pl.BlockSpec((None, tm, SHIFT_WIDTH), lambda b, i: (b, i, 0)),
            pl.BlockSpec((None, 8, SHIFT_WIDTH), lambda b, i: (b, jnp.maximum(i * (tm // 8) - 1, 0), 0)),
            pl.BlockSpec((None, 1, SHIFT_WIDTH), lambda b, i: (b, 0, 0)),
        ] + [pl.BlockSpec(a.shape, const) for a in params],
        out_specs=[out_spec] * 7,
        out_shape=[out_sds] * 7,
        compiler_params=_cparams("parallel", "arbitrary"),
        name="rwkv_prep",
    )(ps3, ps3, prev0, *params)


def _rwkv_prep_t(ps2, prev2, p):
    bd = ps2.shape[0]
    params = _prep_params(p)
    full = lambda a: pl.BlockSpec(a.shape, lambda i: (0, 0))
    out_sds = jax.ShapeDtypeStruct((B_WIDTH, bd), F32)
    return pl.pallas_call(
        _rwkv_prep_t_kernel,
        grid=(1,),
        in_specs=[full(ps2), full(prev2)] + [full(a) for a in params],
        out_specs=[pl.BlockSpec((B_WIDTH, bd), lambda i: (0, 0))] * 7,
        out_shape=[out_sds] * 7,
        compiler_params=_cparams("arbitrary"),
        name="rwkv_prep_t",
    )(ps2, prev2, *params)


def _rwkv_scan_kernel(r_ref, lw_ref, k_ref, v_ref, a_ref, b_ref, g_ref, rk_ref, lnw_ref, lnb_ref,
                      tril_ref, o_ref, s_ref):
    n_b, chunk, _ = r_ref.shape

    @pl.when(pl.program_id(1) == 0)
    def _():
        s_ref[...] = jnp.zeros_like(s_ref)

    ti = lax.broadcasted_iota(jnp.int32, (chunk, 2 * chunk), 0)
    si = lax.broadcasted_iota(jnp.int32, (chunk, 2 * chunk), 1)
    si = jnp.where(si >= chunk, si - chunk, si)
    strict = si < ti
    incl = si <= ti
    eye = jnp.where(lax.broadcasted_iota(jnp.int32, (chunk, chunk), 0)
                    == lax.broadcasted_iota(jnp.int32, (chunk, chunk), 1), 1.0, 0.0).astype(F32)
    n_sq = max(chunk.bit_length() - 2, 0)

    sls = [slice(h * RWKV_HEAD_DIM, (h + 1) * RWKV_HEAD_DIM) for h in range(RWKV_HEADS)]

    chains, ar, bk, v_h, g_last = [], [], [], [], []
    for bi in range(n_b):
        lw = lw_ref[bi]
        cum = _dot_exact_lhs(tril_ref[...], lw)
        e_in = jnp.exp(cum)
        e_inv = jnp.exp(-cum)
        at = a_ref[bi] * jnp.exp(cum - lw)
        rt = r_ref[bi] * e_in
        bt = b_ref[bi] * e_inv
        kt = k_ref[bi] * e_inv
        v = v_ref[bi]
        for h, sl in enumerate(sls):
            chains.append((bi, h))
            ar.append(jnp.concatenate([at[:, sl], rt[:, sl]], axis=0))
            bk.append(jnp.concatenate([bt[:, sl], kt[:, sl]], axis=0))
            v_h.append(v[:, sl])
            g_last.append(e_in[chunk - 1:chunk, sl])
    n = range(len(chains))
    s_old = [s_ref[bi, h] for bi, h in chains]
    m_all = [_dot1(ar[c], bk[c], NT) for c in n]
    gs = [_dot1(ar[c], s_old[c], NT) for c in n]
    a_m = [jnp.where(strict, m_all[c][:chunk], 0.0) for c in n]
    r_m = [jnp.where(incl, m_all[c][chunk:], 0.0) for c in n]
    pw = [a_m[c][:, :chunk] for c in n]
    x = [eye + pw[c] for c in n]
    for _ in range(n_sq):
        pw = [_dot1(pw[c], pw[c]) for c in n]
        x = [x[c] + _dot1(x[c], pw[c]) for c in n]
    av = [_dot1(a_m[c][:, chunk:], v_h[c]) for c in n]
    u = [_dot1(x[c], gs[c][:chunk] + av[c]) for c in n]
    uv = [jnp.concatenate([u[c], v_h[c]], axis=0) for c in n]
    y = [gs[c][chunk:] + _dot1(r_m[c], uv[c]) for c in n]
    for c, (bi, h) in enumerate(chains):
        s_ref[bi, h] = (s_old[c] + _dot3(uv[c], bk[c], TN)) * g_last[c]
    for c, (bi, h) in enumerate(chains):
        sl = sls[h]
        mu = jnp.mean(y[c], axis=-1, keepdims=True)
        var = jnp.mean(jnp.square(y[c] - mu), axis=-1, keepdims=True)
        yn = (y[c] - mu) * lax.rsqrt(var + GN_EPS) * lnw_ref[:, sl] + lnb_ref[:, sl]
        bonus = jnp.sum(r_ref[bi, :, sl] * k_ref[bi, :, sl] * rk_ref[:, sl], axis=-1, keepdims=True) * v_h[c]
        o_ref[bi, :, sl] = (yn + bonus) * g_ref[bi, :, sl]


def _rwkv_scan(streams, p, chunk):
    bsz, t, _ = streams[0].shape
    n_b = SCAN_BATCH if bsz % SCAN_BATCH == 0 else 1
    row = lambda a: a.reshape(1, B_WIDTH)
    tril = (jnp.arange(chunk)[None, :] <= jnp.arange(chunk)[:, None]).astype(BF16)
    const = lambda b, c: (0, 0)
    blk = pl.BlockSpec((n_b, chunk, B_WIDTH), lambda b, c: (b, c, 0))
    st_shape = (bsz, RWKV_HEADS, RWKV_HEAD_DIM, RWKV_HEAD_DIM)
    st = pl.BlockSpec((n_b,) + st_shape[1:], lambda b, c: (b, 0, 0, 0))
    vec = pl.BlockSpec((1, B_WIDTH), const)
    return pl.pallas_call(
        _rwkv_scan_kernel,
        grid=(bsz // n_b, t // chunk),
        in_specs=[blk] * 7 + [vec, vec, vec, pl.BlockSpec((chunk, chunk), const)],
        out_specs=[blk, st],
        out_shape=[jax.ShapeDtypeStruct((bsz, t, B_WIDTH), F32), jax.ShapeDtypeStruct(st_shape, F32)],
        compiler_params=_cparams("parallel", "arbitrary"),
        name="rwkv_scan",
    )(*streams, row(p["r_k"]), row(p["ln_x_w"]), row(p["ln_x_b"]), tril)


def _rwkv_step_kernel(r_ref, lw_ref, k_ref, v_ref, a_ref, b_ref, g_ref, rk_ref, lnw_ref, lnb_ref, s_ref,
                      o_ref, so_ref, y_ref):
    w = jnp.exp(lw_ref[...])
    a, b, k, r = a_ref[...], b_ref[...], k_ref[...], r_ref[...]

    def value_row(i, carry):
        s = s_ref[i]
        sa = jnp.sum(s * a, axis=0, keepdims=True)
        s_new = s * w + sa * b + v_ref[pl.ds(i, 1), :] * k
        so_ref[i] = s_new
        y_ref[pl.ds(i, 1), :] = jnp.sum(s_new * r, axis=0, keepdims=True)
        return carry

    lax.fori_loop(0, RWKV_HEAD_DIM, value_row, 0, unroll=4)
    y = y_ref[...]
    mu = jnp.mean(y, axis=0, keepdims=True)
    var = jnp.mean(jnp.square(y - mu), axis=0, keepdims=True)
    yn = (y - mu) * lax.rsqrt(var + GN_EPS) * lnw_ref[...] + lnb_ref[...]
    bonus = jnp.sum(r_ref[...] * k_ref[...] * rk_ref[...], axis=0, keepdims=True) * v_ref[...]
    o_ref[...] = (yn + bonus) * g_ref[...]


def _rwkv_step(streams_t, s_t, p):
    bd = s_t.shape[-1]
    n = RWKV_HEAD_DIM
    col = lambda a: a.reshape(B_WIDTH, 1)
    vec = pl.BlockSpec((n, bd), lambda h: (h, 0))
    par = pl.BlockSpec((n, 1), lambda h: (h, 0))
    st = pl.BlockSpec((None, n, n, bd), lambda h: (h, 0, 0, 0))
    return pl.pallas_call(
        _rwkv_step_kernel,
        grid=(RWKV_HEADS,),
        in_specs=[vec] * 7 + [par] * 3 + [st],
        out_specs=[vec, st],
        out_shape=[jax.ShapeDtypeStruct((B_WIDTH, bd), F32), jax.ShapeDtypeStruct(s_t.shape, F32)],
        scratch_shapes=[pltpu.VMEM((n, bd), F32)],
        compiler_params=_cparams("parallel"),
        name="rwkv_step",
    )(*streams_t, col(p["r_k"]), col(p["ln_x_w"]), col(p["ln_x_b"]), s_t)


def _finish_kernel(x_ref, *refs, n_stats):
    stat_refs = refs[:n_stats]
    ob_ref, g_ref, wpa_ref, wpb_ref, wo_ref, g2_ref, rw_ref, rb_ref, h_ref, hn_ref, gate_ref, top_ref = refs[n_stats:]
    if n_stats == 1:
        o_a = stat_refs[0][...]
    else:
        outs, lses = stat_refs[0::N_STATS], stat_refs[1::N_STATS]
        mx = functools.reduce(jnp.maximum, [l[...] for l in lses])
        es = [jnp.exp(l[...] - mx) for l in lses]
        o_a = sum(o[...] * e for o, e in zip(outs, es)) / sum(es)
    br_a = _dot1(o_a, wpa_ref[...])
    br_b = _dot1(ob_ref[...], wpb_ref[...])
    mix = g_ref[:, :D_MODEL] * br_a + g_ref[:, D_MODEL:] * br_b
    h = x_ref[...] + _dot1(mix, wo_ref[...])
    h_ref[...] = h
    hn = h * lax.rsqrt(jnp.mean(h * h, axis=-1, keepdims=True) + RMS_EPS) * g2_ref[...]
    hn_ref[...] = hn
    lg = _dot3(hn, rw_ref[...]) + rb_ref[...]
    lane = lax.broadcasted_iota(jnp.int32, lg.shape, 1)
    lane_f = lane.astype(F32)
    live = jnp.where(lane < N_EXPERTS, lg, -jnp.inf)
    vals = jnp.full(lg.shape, -jnp.inf, F32)
    ids = jnp.zeros(lg.shape, F32)
    for j in range(TOP_K):
        best = jnp.max(live, axis=-1, keepdims=True)
        idx = jnp.min(jnp.where(live == best, lane_f, float(LOGIT_PAD)), axis=-1, keepdims=True)
        vals = jnp.where(lane == j, best, vals)
        ids = jnp.where(lane == j, idx, ids)
        live = jnp.where(lane_f == idx, -jnp.inf, live)
    e = jnp.exp(vals - jnp.max(vals, axis=-1, keepdims=True))
    gate_ref[...] = e / jnp.sum(e, axis=-1, keepdims=True)
    top_ref[...] = ids.astype(jnp.int32)


def _finish(x2, stats, o_b, gates, wb, tm):
    n = x2.shape[0]
    const = lambda i: (0, 0)
    rows = lambda w: pl.BlockSpec((tm, w), lambda i: (i, 0))
    params = [wb["w_pa"], wb["w_pb"], wb["w_o"], wb["norm2_g"], wb["router_w"], wb["router_b"]]
    return pl.pallas_call(
        functools.partial(_finish_kernel, n_stats=len(stats)),
        grid=(n // tm,),
        in_specs=[rows(D_MODEL)] + [rows(A_WIDTH)] * len(stats) + [rows(B_WIDTH), rows(2 * D_MODEL)]
                 + [pl.BlockSpec(a.shape, const) for a in params],
        out_specs=[rows(D_MODEL), rows(D_MODEL), rows(LOGIT_PAD), rows(LOGIT_PAD)],
        out_shape=[jax.ShapeDtypeStruct((n, D_MODEL), F32), jax.ShapeDtypeStruct((n, D_MODEL), F32),
                   jax.ShapeDtypeStruct((n, LOGIT_PAD), F32), jax.ShapeDtypeStruct((n, LOGIT_PAD), jnp.int32)],
        compiler_params=_cparams("parallel"),
        name="finish",
    )(x2, *stats, o_b, gates, *params)


def _row(ref, i):
    return ref.at[pl.ds(i, 1)]


def _moe_kernel(blk_e_ref, nused_ref, tok_ref, tok_next_ref, tok_next2_ref, hn_hbm, wgu32_ref, bgu_ref, wd32_ref,
                bdn_ref, ys_ref, xbuf_a, xbuf_b, xbuf_c, wgu_ref, wd_ref, fence_buf, sem, fence_sem):
    i = pl.program_id(0)
    nused = nused_ref[0]
    n_chunks = D_FF // COL_CHUNK
    bufs = (xbuf_a, xbuf_b, xbuf_c)
    n_bufs = len(bufs)

    def issue(idx_ref, s):
        for r in range(MOE_BLOCK):
            pltpu.make_async_copy(_row(hn_hbm, idx_ref[0, 0, r]), _row(bufs[s], r), sem.at[s]).start(priority=r % 2)

    def drain(s):
        for _ in range(MOE_BLOCK):
            pltpu.make_async_copy(_row(hn_hbm, 0), _row(bufs[s], 0), sem.at[s]).wait()

    @pl.when(jnp.logical_and(i == 0, nused > 0))
    def _():
        issue(tok_ref, 0)
        issue(tok_next_ref, 1)

    new_expert = jnp.logical_or(i == 0, blk_e_ref[i] != blk_e_ref[jnp.maximum(i - 1, 0)])

    @pl.when(jnp.logical_and(i < nused, new_expert))
    def _():
        wgu_ref[...] = wgu32_ref[...].astype(BF16)
        wd_ref[...] = wd32_ref[...].astype(BF16)

    def block(s):
        drain(s)
        x = bufs[s][...].astype(BF16)
        fence_buf[0] = jnp.zeros(fence_buf.shape[1:], F32)
        fence = pltpu.make_async_copy(fence_buf.at[0], fence_buf.at[1], fence_sem)
        fence.start()
        issue(tok_next2_ref, (s + 2) % n_bufs)
        acc = jnp.zeros((MOE_BLOCK, D_MODEL), F32)
        for c in range(n_chunks):
            if c == n_chunks // 2:
                fence.wait()
            lo = c * COL_CHUNK
            gate = _dg(x, wgu_ref[:, lo:lo + COL_CHUNK], NN) + bgu_ref[:, lo:lo + COL_CHUNK]
            up = _dg(x, wgu_ref[:, D_FF + lo:D_FF + lo + COL_CHUNK], NN) + bgu_ref[:, D_FF + lo:D_FF + lo + COL_CHUNK]
            gate = jnp.minimum(gate, SWIGLU_LIMIT)
            up = jnp.clip(up, -SWIGLU_LIMIT, SWIGLU_LIMIT)
            hid = (up + 1.0) * gate * _sigmoid(gate * SWIGLU_ALPHA)
            acc = acc + _dg(hid.astype(BF16), wd_ref[lo:lo + COL_CHUNK, :], NN)
        ys_ref[...] = acc + bdn_ref[...]

        @pl.when(i + 1 >= nused)
        def _():
            drain((s + 1) % n_bufs)
            drain((s + 2) % n_bufs)

    for s in range(n_bufs):
        pl.when(jnp.logical_and(i < nused, i % n_bufs == s))(functools.partial(block, s))

    @pl.when(i >= nused)
    def _():
        ys_ref[...] = jnp.zeros_like(ys_ref)


def _moe_blocks(hn, row_tok, blk_e, nused, wb):
    n_blocks = row_tok.shape[0]
    last = n_blocks - 1
    grid_spec = pltpu.PrefetchScalarGridSpec(
        num_scalar_prefetch=2,
        grid=(n_blocks,),
        in_specs=[
            pl.BlockSpec((1, 1, MOE_BLOCK), lambda i, be, nu: (i, 0, 0), memory_space=pltpu.SMEM),
            pl.BlockSpec((1, 1, MOE_BLOCK), lambda i, be, nu: (jnp.minimum(i + 1, last), 0, 0),
                         memory_space=pltpu.SMEM),
            pl.BlockSpec((1, 1, MOE_BLOCK), lambda i, be, nu: (jnp.minimum(i + 2, last), 0, 0),
                         memory_space=pltpu.SMEM),
            pl.BlockSpec(memory_space=pl.ANY),
            pl.BlockSpec((None, D_MODEL, 2 * D_FF), lambda i, be, nu: (be[i], 0, 0)),
            pl.BlockSpec((None, 1, 2 * D_FF), lambda i, be, nu: (be[i], 0, 0)),
            pl.BlockSpec((None, D_FF, D_MODEL), lambda i, be, nu: (be[i], 0, 0)),
            pl.BlockSpec((None, 1, D_MODEL), lambda i, be, nu: (be[i], 0, 0)),
        ],
        out_specs=pl.BlockSpec((MOE_BLOCK, D_MODEL), lambda i, be, nu: (i, 0)),
        scratch_shapes=[pltpu.VMEM((MOE_BLOCK, D_MODEL), F32)] * 3
                       + [pltpu.VMEM((D_MODEL, 2 * D_FF), BF16), pltpu.VMEM((D_FF, D_MODEL), BF16),
                          pltpu.VMEM((2, 8, LANE_TILE), F32), pltpu.SemaphoreType.DMA((3,)),
                          pltpu.SemaphoreType.DMA(())],
    )
    return pl.pallas_call(
        _moe_kernel,
        grid_spec=grid_spec,
        out_shape=jax.ShapeDtypeStruct((n_blocks * MOE_BLOCK, D_MODEL), F32),
        compiler_params=_cparams("arbitrary"),
        name="moe_blocks",
    )(blk_e, nused, row_tok, row_tok, row_tok, hn, wb["w_gu"], wb["b_gu"], wb["w_down"], wb["b_down"])


def _combine_kernel(pos_ref, pos_next_ref, h_ref, gate_ref, ys_hbm, y_ref, gbuf, sem):
    tm = h_ref.shape[0]
    rows = TOP_K * tm
    i = pl.program_id(0)
    n = pl.num_programs(0)
    slot = i % 2

    def issue(idx_ref, s):
        for kk in range(rows):
            pltpu.make_async_copy(_row(ys_hbm, idx_ref[0, 0, kk]), gbuf.at[s, pl.ds(kk, 1)], sem.at[s]).start(
                priority=kk % 2)

    @pl.when(i == 0)
    def _():
        issue(pos_ref, 0)

    @pl.when(i + 1 < n)
    def _():
        issue(pos_next_ref, 1 - slot)

    for kk in range(rows):
        pltpu.make_async_copy(_row(ys_hbm, 0), gbuf.at[slot, pl.ds(0, 1)], sem.at[slot]).wait()
    gv = gate_ref[...]
    ff = gv[:, 0:1] * gbuf[slot, 0:tm, :]
    for j in range(1, TOP_K):
        ff = ff + gv[:, j:j + 1] * gbuf[slot, j * tm:(j + 1) * tm, :]
    y_ref[...] = h_ref[...] + ff


def _combine(h, gates_pad, pos, ys):
    n = h.shape[0]
    tm = min(COMBINE_TOKENS, n)
    steps = n // tm
    rows = TOP_K * tm
    pos3 = pos.reshape(steps, tm, TOP_K).transpose(0, 2, 1).reshape(steps, 1, rows)
    return pl.pallas_call(
        _combine_kernel,
        grid=(steps,),
        in_specs=[
            pl.BlockSpec((1, 1, rows), lambda i: (i, 0, 0), memory_space=pltpu.SMEM),
            pl.BlockSpec((1, 1, rows), lambda i: (jnp.minimum(i + 1, steps - 1), 0, 0), memory_space=pltpu.SMEM),
            pl.BlockSpec((tm, D_MODEL), lambda i: (i, 0)),
            pl.BlockSpec((tm, LOGIT_PAD), lambda i: (i, 0)),
            pl.BlockSpec(memory_space=pl.ANY),
        ],
        out_specs=pl.BlockSpec((tm, D_MODEL), lambda i: (i, 0)),
        out_shape=jax.ShapeDtypeStruct((n, D_MODEL), F32),
        scratch_shapes=[pltpu.VMEM((2, rows, D_MODEL), F32), pltpu.SemaphoreType.DMA((2,))],
        compiler_params=_cparams("arbitrary"),
        name="moe_combine",
    )(pos3, pos3, h, gates_pad, ys)


def _route(top_e):
    n = top_e.shape[0]
    nk = n * TOP_K
    flat_e = top_e.reshape(nk)
    onehot = (flat_e[:, None] == jnp.arange(N_EXPERTS, dtype=top_e.dtype)[None, :]).astype(jnp.int32)
    csum = jnp.cumsum(onehot, axis=0)
    counts = csum[-1]
    starts = jnp.cumsum(counts) - counts
    pcounts = (counts + MOE_BLOCK - 1) // MOE_BLOCK * MOE_BLOCK
    pends = jnp.cumsum(pcounts)
    pstarts = pends - pcounts
    pos = jnp.sum(onehot * (csum - 1 + pstarts[None, :]), axis=1).astype(jnp.int32)
    n_blocks = -(-nk // MOE_BLOCK) + N_EXPERTS
    blk_start = jnp.arange(n_blocks, dtype=jnp.int32) * MOE_BLOCK
    blk_e = jnp.minimum(jnp.sum(blk_start[:, None] >= pends[None, :], axis=1), N_EXPERTS - 1).astype(jnp.int32)
    nused = (pends[-1:] // MOE_BLOCK).astype(jnp.int32)
    order = jnp.argsort(flat_e)
    e_row = jnp.repeat(blk_e, MOE_BLOCK)
    rank = jnp.arange(n_blocks * MOE_BLOCK, dtype=jnp.int32) - pstarts[e_row]
    src = jnp.minimum(starts[e_row] + rank, nk - 1)
    row_tok = jnp.where(rank < counts[e_row], order[src] // TOP_K, 0).astype(jnp.int32)
    return pos, row_tok.reshape(n_blocks, 1, MOE_BLOCK), blk_e, nused


def _moe(h, hn, gates_pad, top_pad, wb):
    pos, row_tok, blk_e, nused = _route(top_pad[:, :TOP_K])
    ys = _moe_blocks(hn, row_tok, blk_e, nused, wb)
    return _combine(h, gates_pad, pos, ys)


def kernel(x_prompt, x_sample, cache_kv_w128, cache_kv_w512, cache_kv_w2048, state_wkv, state_shift,
           norm1_g, w_in, q_norm_g, k_norm_g, mu_shift, w0, w2, a0, a2, g2, k_k, k_a, r_k,
           ln_x_w, ln_x_b, w_pa, w_pb, w_o, norm2_g, router_w, router_b, w_gu, b_gu, w_down, b_down):
    bp, seq, _ = x_prompt.shape
    bd, t_s, _ = x_sample.shape
    assert t_s == 1
    rw = dict(mu_shift=mu_shift, w0=w0, w2=w2, a0=a0, a2=a2, g2=g2, k_k=k_k, k_a=k_a, r_k=r_k,
              ln_x_w=ln_x_w, ln_x_b=ln_x_b)
    wb = dict(
        w_pa=w_pa.astype(BF16), w_pb=w_pb.astype(BF16), w_o=w_o.astype(BF16),
        norm2_g=norm2_g.reshape(1, D_MODEL),
        router_w=jnp.pad(router_w, ((0, 0), (0, LOGIT_PAD - N_EXPERTS))),
        router_b=jnp.pad(router_b, (0, LOGIT_PAD - N_EXPERTS)).reshape(1, LOGIT_PAD),
        w_gu=w_gu.astype(F32), b_gu=b_gu.reshape(N_EXPERTS, 1, 2 * D_FF),
        w_down=w_down.astype(F32), b_down=b_down.reshape(N_EXPERTS, 1, D_MODEL),
    )
    w_in_b = w_in.astype(BF16)
    mult = jnp.stack([jnp.tile(q_norm_g[c // 3] * (HEAD_DIM ** -0.5) if c % 3 == 0 else
                               (k_norm_g[c // 3] if c % 3 == 1 else jnp.ones((HEAD_DIM,), F32)),
                               HEADS_PER_GROUP) for c in range(A_COLS // COL_CHUNK)])
    bd_head = _seg_ones(COL_CHUNK, HEAD_DIM, 1.0 / HEAD_DIM)

    xp2 = x_prompt.reshape(bp * seq, D_MODEL)
    pa, ps, pg = _in_proj(xp2, norm1_g, w_in_b, mult, bd_head, tm=512)
    pa3 = pa.reshape(bp, seq, A_COLS)
    stats = []
    kv_p = []
    for g in range(N_GROUPS):
        stats += _attn_prompt(pa3, g)
        keep = min(GROUP_WINDOWS[g], seq)
        lo = g * 3 * A_WIDTH + A_WIDTH
        kv_p.append(pa3[:, seq - keep:, lo:lo + 2 * A_WIDTH].reshape(bp, keep, 2, HEADS_PER_GROUP, HEAD_DIM))
    ps3 = ps.reshape(bp, seq, SHIFT_WIDTH)
    streams = _rwkv_prep(ps3, jnp.zeros((bp, 1, SHIFT_WIDTH), F32), rw, tm=512)
    o_b, wkv_p = _rwkv_scan(streams, rw, chunk=SCAN_CHUNK)
    h, hn, gates, top = _finish(xp2, stats, o_b.reshape(bp * seq, B_WIDTH), pg, wb, tm=256)
    y_prompt = _moe(h, hn, gates, top, wb).reshape(bp, seq, D_MODEL)
    shift_p = ps3[:, -1:]

    caches = (cache_kv_w128, cache_kv_w512, cache_kv_w2048)
    xs2 = x_sample.reshape(bd, D_MODEL)
    pa_s, ps_s, pg_s = _in_proj(xs2, norm1_g, w_in_b, mult, bd_head, tm=bd)
    caches_t = [jnp.transpose(c.astype(F32), (0, 2, 3, 4, 1)) for c in caches]
    *kv_t, oa_t = _sample_cache(pa_s.T, caches_t)
    kv_s = [jnp.transpose(t, (0, 4, 1, 2, 3)).astype(c.dtype) for t, c in zip(kv_t, caches)]
    streams_t = _rwkv_prep_t(ps_s, state_shift.reshape(bd, SHIFT_WIDTH).astype(F32), rw)
    ob_t, s_t = _rwkv_step(streams_t, jnp.transpose(state_wkv.astype(F32), (1, 2, 3, 0)), rw)
    wkv_s = jnp.transpose(s_t, (3, 0, 1, 2))
    h_s, hn_s, gates_s, top_s = _finish(xs2, [oa_t.T], ob_t.T, pg_s, wb, tm=bd)
    y_sample = _moe(h_s, hn_s, gates_s, top_s, wb).reshape(bd, 1, D_MODEL)
    shift_s = ps_s.reshape(bd, 1, SHIFT_WIDTH)

    return (y_prompt, y_sample, kv_p[0], kv_p[1], kv_p[2], wkv_p.astype(state_wkv.dtype), shift_p,
            kv_s[0], kv_s[1], kv_s[2], wkv_s.astype(state_wkv.dtype), shift_s.astype(state_shift.dtype))
```

```python
import functools

import jax
import jax.numpy as jnp
from jax import lax
from jax.experimental import pallas as pl
from jax.experimental.pallas import tpu as pltpu

F32 = jnp.float32
BF16 = jnp.bfloat16

D_MODEL = 1024
GROUP_WINDOWS = (128, 512, 2048)
GROUP_DILATIONS = (1, 4, 16)
N_GROUPS = 3
HEADS_PER_GROUP = 4
HEAD_DIM = 64
A_WIDTH = HEADS_PER_GROUP * HEAD_DIM
A_COLS = N_GROUPS * 3 * A_WIDTH
RWKV_HEADS = 8
RWKV_HEAD_DIM = 64
B_WIDTH = RWKV_HEADS * RWKV_HEAD_DIM
DECAY_LORA = 64
AAA_LORA = 64
GATE_LORA = 128
SHIFT_WIDTH = 3 * B_WIDTH + DECAY_LORA + AAA_LORA + GATE_LORA
D_IN = A_COLS + SHIFT_WIDTH + 2 * D_MODEL
N_EXPERTS = 32
TOP_K = 4
D_FF = D_MODEL
SWIGLU_LIMIT = 7.0
SWIGLU_ALPHA = 1.702
MOE_BLOCK = 256
RMS_EPS = 1e-6
GN_EPS = 64e-5

N_BACK = 128
COL_CHUNK = 256
SCAN_CHUNK = 64
SCAN_BATCH = 2
LANE_TILE = 128
N_STATS = 2
LOGIT_PAD = LANE_TILE
COMBINE_TOKENS = 256
PROJ_ROWS = 512
VMEM_LIMIT = 48 * 1024 * 1024


def _cparams(*sem):
    return pltpu.CompilerParams(dimension_semantics=sem, vmem_limit_bytes=VMEM_LIMIT)


def _sigmoid(x):
    return 1.0 / (1.0 + jnp.exp(-x))


def _split(x):
    hi = x.astype(BF16)
    lo = (x - hi.astype(F32)).astype(BF16)
    return hi, lo


def _dg(a, b, dims):
    return lax.dot_general(a, b, (dims, ((), ())), preferred_element_type=F32)


NN = ((1,), (0,))
NT = ((1,), (1,))
TN = ((0,), (0,))


def _dot1(a, b, dims=NN):
    return _dg(a.astype(BF16), b.astype(BF16), dims)


def _dot3(a, b, dims=NN):
    ah, al = _split(a)
    bh, bl = _split(b)
    return _dg(ah, bh, dims) + _dg(al, bh, dims) + _dg(ah, bl, dims)


def _split3(x):
    hi = x.astype(BF16)
    r1 = x - hi.astype(F32)
    mid = r1.astype(BF16)
    lo = (r1 - mid.astype(F32)).astype(BF16)
    return hi, mid, lo


def _dot_exact_rhs(a, b_bf16):
    return sum(_dg(t, b_bf16, NN) for t in _split3(a))


def _dot_exact_lhs(l_bf16, x):
    return sum(_dg(l_bf16, t, NN) for t in _split3(x))


def _seg_ones(width, seg, scale):
    i = jnp.arange(width)[:, None] // seg
    j = jnp.arange(width)[None, :] // seg
    return jnp.where(i == j, scale, 0.0).astype(BF16)


def _in_proj_kernel(x_ref, g1_ref, w_ref, mult_ref, bd_ref, pa_ref, ps_ref, pg_ref):
    x = x_ref[...]
    xn = x * lax.rsqrt(jnp.mean(x * x, axis=-1, keepdims=True) + RMS_EPS) * g1_ref[...]
    xb = xn.astype(BF16)
    bd = bd_ref[...]
    for c in range(A_COLS // COL_CHUNK):
        acc = _dg(xb, w_ref[:, c * COL_CHUNK:(c + 1) * COL_CHUNK], NN)
        if c % 3 != 2:
            ms = _dot_exact_rhs(acc * acc, bd)
            acc = acc * lax.rsqrt(ms + RMS_EPS) * mult_ref[c:c + 1, :]
        pa_ref[:, c * COL_CHUNK:(c + 1) * COL_CHUNK] = acc
    for c in range(SHIFT_WIDTH // COL_CHUNK):
        lo = A_COLS + c * COL_CHUNK
        ps_ref[:, c * COL_CHUNK:(c + 1) * COL_CHUNK] = _dg(xb, w_ref[:, lo:lo + COL_CHUNK], NN)
    for c in range(2 * D_MODEL // COL_CHUNK):
        lo = A_COLS + SHIFT_WIDTH + c * COL_CHUNK
        pg_ref[:, c * COL_CHUNK:(c + 1) * COL_CHUNK] = _sigmoid(_dg(xb, w_ref[:, lo:lo + COL_CHUNK], NN)).astype(BF16)


def _in_proj(x2, norm1_g, w_in_b, mult, bd, tm):
    n = x2.shape[0]
    const = lambda i: (0, 0)
    return pl.pallas_call(
        _in_proj_kernel,
        grid=(n // tm,),
        in_specs=[
            pl.BlockSpec((tm, D_MODEL), lambda i: (i, 0)),
            pl.BlockSpec((1, D_MODEL), const),
            pl.BlockSpec((D_MODEL, D_IN), const, pipeline_mode=pl.Buffered(1)),
            pl.BlockSpec(mult.shape, const),
            pl.BlockSpec(bd.shape, const),
        ],
        out_specs=[
            pl.BlockSpec((tm, A_COLS), lambda i: (i, 0)),
            pl.BlockSpec((tm, SHIFT_WIDTH), lambda i: (i, 0)),
            pl.BlockSpec((tm, 2 * D_MODEL), lambda i: (i, 0)),
        ],
        out_shape=[
            jax.ShapeDtypeStruct((n, A_COLS), F32),
            jax.ShapeDtypeStruct((n, SHIFT_WIDTH), F32),
            jax.ShapeDtypeStruct((n, 2 * D_MODEL), BF16),
        ],
        compiler_params=_cparams("parallel"),
        name="in_proj",
    )(x2, norm1_g.reshape(1, D_MODEL), w_in_b, mult, bd)


def _attn_kernel(*refs, dil):
    halves = A_WIDTH // LANE_TILE
    q_refs, kp_refs, kc_refs, vp_refs, vc_refs = (refs[i * halves:(i + 1) * halves] for i in range(5))
    out_refs = refs[5 * halves:5 * halves + N_STATS]
    qs, ks, vs = refs[5 * halves + N_STATS:5 * halves + N_STATS + 3]
    stat_s = refs[5 * halves + N_STATS + 3:5 * halves + 2 * N_STATS + 3]
    stage = refs[5 * halves + 2 * N_STATS + 3:]
    n = pl.program_id(1)
    qi = lax.broadcasted_iota(jnp.int32, (N_BACK, 2 * N_BACK), 0)
    kj = lax.broadcasted_iota(jnp.int32, (N_BACK, 2 * N_BACK), 1)
    first_prev = jnp.where(n > 0, 0, N_BACK)
    lo = jnp.where(kj < N_BACK, qi + first_prev, N_BACK)
    hi = jnp.where(kj < N_BACK, N_BACK - 1, qi + N_BACK)
    mask = jnp.logical_and(kj >= lo, kj <= hi)

    def stream(r, carry):
        rows = pl.ds(r, N_BACK, stride=dil) if dil > 1 else slice(None)
        for c in range(halves):
            lanes = slice(c * LANE_TILE, (c + 1) * LANE_TILE)
            qs[:, lanes] = q_refs[c][rows, :]
            ks[0:N_BACK, lanes] = kp_refs[c][rows, :]
            ks[N_BACK:, lanes] = kc_refs[c][rows, :]
            vs[0:N_BACK, lanes] = vp_refs[c][rows, :]
            vs[N_BACK:, lanes] = vc_refs[c][rows, :]
        sls = [slice(h * HEAD_DIM, (h + 1) * HEAD_DIM) for h in range(HEADS_PER_GROUP)]
        s = [jnp.where(mask, _dot1(qs[:, sl], ks[:, sl], NT), -jnp.inf) for sl in sls]
        m = [jnp.max(t, axis=-1, keepdims=True) for t in s]
        p = [jnp.exp(t - mm) for t, mm in zip(s, m)]
        den = [jnp.sum(t, axis=-1, keepdims=True) for t in p]
        num = [_dot1(t, vs[:, sl]) for t, sl in zip(p, sls)]
        for h, sl in enumerate(sls):
            stat_s[0][:, sl] = num[h] / den[h]
            stat_s[1][:, sl] = jnp.broadcast_to(m[h] + jnp.log(den[h]), (N_BACK, HEAD_DIM))
        for i in range(N_STATS):
            if dil > 1:
                for c in range(halves):
                    stage[i * halves + c][rows, :] = stat_s[i][:, c * LANE_TILE:(c + 1) * LANE_TILE]
            else:
                out_refs[i][...] = stat_s[i][...]
        return carry

    if dil > 1:
        lax.fori_loop(0, dil, stream, 0)
        for i in range(N_STATS):
            for c in range(halves):
                out_refs[i][:, c * LANE_TILE:(c + 1) * LANE_TILE] = stage[i * halves + c][...]
    else:
        stream(0, 0)


def _attn_prompt(pa3, g):
    bsz, seq, _ = pa3.shape
    dil = GROUP_DILATIONS[g]
    rows = N_BACK * dil
    nb = seq // rows
    halves = A_WIDTH // LANE_TILE

    def specs(slab, prev):
        def one(c):
            col = (3 * g + slab) * halves + c
            if prev:
                return pl.BlockSpec((None, rows, LANE_TILE), lambda b, n: (b, jnp.maximum(n - 1, 0), col))
            return pl.BlockSpec((None, rows, LANE_TILE), lambda b, n: (b, n, col))
        return [one(c) for c in range(halves)]

    in_specs = specs(0, False) + specs(1, True) + specs(1, False) + specs(2, True) + specs(2, False)
    out_spec = pl.BlockSpec((None, rows, A_WIDTH), lambda b, n: (b, n, 0))
    out_sds = jax.ShapeDtypeStruct((bsz, seq, A_WIDTH), F32)
    scratch = [pltpu.VMEM((N_BACK, A_WIDTH), F32), pltpu.VMEM((2 * N_BACK, A_WIDTH), F32),
               pltpu.VMEM((2 * N_BACK, A_WIDTH), F32)] + [pltpu.VMEM((N_BACK, A_WIDTH), F32)] * N_STATS
    if dil > 1:
        scratch += [pltpu.VMEM((rows, LANE_TILE), F32)] * (N_STATS * halves)
    outs = pl.pallas_call(
        functools.partial(_attn_kernel, dil=dil),
        grid=(bsz, nb),
        in_specs=in_specs,
        out_specs=[out_spec] * N_STATS,
        out_shape=[out_sds] * N_STATS,
        scratch_shapes=scratch,
        compiler_params=_cparams("parallel", "arbitrary"),
        name=f"attn_prompt_g{g}",
    )(*([pa3] * len(in_specs)))
    return [o.reshape(bsz * seq, A_WIDTH) for o in outs]


def _sample_cache_kernel(pat_ref, c0_ref, c1_ref, c2_ref, o0_ref, o1_ref, o2_ref, oa_ref):
    b = pl.program_id(0)
    n_b = pat_ref.shape[1]

    @pl.when(b == 0)
    def _():
        oa_ref[...] = jnp.zeros_like(oa_ref)

    lane_b = lax.broadcasted_iota(jnp.int32, pat_ref.shape, 1) == b
    col = jnp.sum(jnp.where(lane_b, pat_ref[...], 0.0), axis=1, keepdims=True)
    c_refs = (c0_ref, c1_ref, c2_ref)
    o_refs = (o0_ref, o1_ref, o2_ref)
    stats = []
    for g in range(N_GROUPS):
        dil = GROUP_DILATIONS[g]
        win = GROUP_WINDOWS[g]
        base = g * 3 * A_WIDTH
        lane = lax.broadcasted_iota(jnp.int32, (HEAD_DIM, win), 1)
        last = lane == win - 1
        live = lax.broadcasted_iota(jnp.int32, (1, win), 1) % dil == 0
        for h in range(HEADS_PER_GROUP):
            o = h * HEAD_DIM
            q = col[base + o:base + o + HEAD_DIM]
            kn = col[base + A_WIDTH + o:base + A_WIDTH + o + HEAD_DIM]
            vn = col[base + 2 * A_WIDTH + o:base + 2 * A_WIDTH + o + HEAD_DIM]
            kt = c_refs[g][0, 0, h]
            vt = c_refs[g][0, 1, h]
            o_refs[g][0, 0, h] = jnp.where(last, kn, pltpu.roll(kt, win - 1, axis=1))
            o_refs[g][0, 1, h] = jnp.where(last, vn, pltpu.roll(vt, win - 1, axis=1))
            s_c = jnp.where(live, jnp.sum(kt * q, axis=0, keepdims=True), -jnp.inf)
            s_n = jnp.sum(kn * q, axis=0, keepdims=True)
            m = jnp.maximum(jnp.max(s_c, axis=1, keepdims=True), s_n)
            p_c = jnp.exp(s_c - m)
            p_n = jnp.exp(s_n - m)
            den = jnp.sum(p_c, axis=1, keepdims=True) + p_n
            num = jnp.sum(vt * p_c, axis=1, keepdims=True) + p_n * vn
            stats.append((num, den, m))
    outs = []
    for h in range(HEADS_PER_GROUP):
        per_g = [stats[g * HEADS_PER_GROUP + h] for g in range(N_GROUPS)]
        mx = functools.reduce(jnp.maximum, [m for _, _, m in per_g])
        num = sum(n_ * jnp.exp(m - mx) for n_, _, m in per_g)
        den = sum(d_ * jnp.exp(m - mx) for _, d_, m in per_g)
        outs.append(num / den)
    o_col = jnp.concatenate(outs, axis=0)
    lane_o = lax.broadcasted_iota(jnp.int32, (A_WIDTH, n_b), 1) == b
    oa_ref[...] = jnp.where(lane_o, o_col, oa_ref[...])


def _sample_cache(pa_t, caches_t):
    bd = pa_t.shape[1]
    cspecs = [pl.BlockSpec((1, 2, HEADS_PER_GROUP, HEAD_DIM, w), lambda b: (b, 0, 0, 0, 0)) for w in GROUP_WINDOWS]
    return pl.pallas_call(
        _sample_cache_kernel,
        grid=(bd,),
        in_specs=[pl.BlockSpec(pa_t.shape, lambda b: (0, 0))] + cspecs,
        out_specs=cspecs + [pl.BlockSpec((A_WIDTH, bd), lambda b: (0, 0))],
        out_shape=[jax.ShapeDtypeStruct(c.shape, c.dtype) for c in caches_t]
                  + [jax.ShapeDtypeStruct((A_WIDTH, bd), F32)],
        compiler_params=_cparams("arbitrary"),
        name="sample_cache",
    )(pa_t, *caches_t)


def _rwkv_prep_math(pf, shifted, mu_ref, w0_ref, w2_ref, a0_ref, a2_ref, g2_ref, kk_ref, ka_ref, bd_ref):
    z = pf + mu_ref[...] * (shifted - pf)
    c1, c2, c3 = B_WIDTH, 2 * B_WIDTH, 3 * B_WIDTH
    r, k, v = z[:, :c1], z[:, c1:c2], z[:, c2:c3]
    wl = z[:, c3:c3 + DECAY_LORA]
    al = z[:, c3 + DECAY_LORA:c3 + DECAY_LORA + AAA_LORA]
    gl = z[:, c3 + DECAY_LORA + AAA_LORA:]
    xw = w0_ref[...] + _dot1(jnp.tanh(wl), w2_ref[...])
    w_log = -(jnp.maximum(-xw, 0.0) + jnp.log(1.0 + jnp.exp(-jnp.abs(xw)))) - 0.5
    lw = -jnp.exp(w_log)
    a = _sigmoid(a0_ref[...] + _dot1(al, a2_ref[...]))
    g = _dot1(_sigmoid(gl), g2_ref[...])
    kk = k * kk_ref[...]
    bd = bd_ref[...]
    kks = []
    for c in range(B_WIDTH // 128):
        kc = kk[:, c * 128:(c + 1) * 128]
        nrm = jnp.sqrt(_dot_exact_rhs(kc * kc, bd))
        kks.append(kc / jnp.maximum(nrm, 1e-12))
    kkn = jnp.concatenate(kks, axis=1)
    return r, lw, k * (1.0 + (a - 1.0) * ka_ref[...]), v, -kkn, kkn * a, g


def _rwkv_prep_kernel(ps_ref, pv_ref, p0_ref, *refs):
    param_refs, out_refs = refs[:9], refs[9:]
    i = pl.program_id(1)
    pf = ps_ref[...]
    prev_row = jnp.where(i == 0, p0_ref[...], pv_ref[7:8, :])
    row = lax.broadcasted_iota(jnp.int32, pf.shape, 0)
    shifted = jnp.where(row == 0, prev_row, pltpu.roll(pf, 1, axis=0))
    for o_ref, val in zip(out_refs, _rwkv_prep_math(pf, shifted, *param_refs)):
        o_ref[...] = val


def _rwkv_prep_t_kernel(ps_ref, prev_ref, *refs):
    param_refs, out_refs = refs[:9], refs[9:]
    for o_ref, val in zip(out_refs, _rwkv_prep_math(ps_ref[...], prev_ref[...], *param_refs)):
        o_ref[...] = val.T


def _prep_params(p):
    row = lambda a: a.reshape(1, -1)
    return [row(p["mu_shift"]), row(p["w0"]), p["w2"], row(p["a0"]), p["a2"], p["g2"],
            row(p["k_k"]), row(p["k_a"]), _seg_ones(128, RWKV_HEAD_DIM, 1.0)]


def _rwkv_prep(ps3, prev0, p, tm):
    bsz, t, _ = ps3.shape
    const = lambda b, i: (0, 0)
    params = _prep_params(p)
    out_spec = pl.BlockSpec((None, tm, B_WIDTH), lambda b, i: (b, i, 0))
    out_sds = jax.ShapeDtypeStruct((bsz, t, B_WIDTH), F32)
    return pl.pallas_call(
        _rwkv_prep_kernel,
        grid=(bsz, t // tm),
        in_specs=[
            pl.BlockSpec((None, tm, SHIFT_WIDTH), lambda b, i: (b, i, 0)),
            pl.BlockSpec((None, 8, SHIFT_WIDTH), lambda b, i: (b, jnp.maximum(i * (tm // 8) - 1, 0), 0)),
            pl.BlockSpec((None, 1, SHIFT_WIDTH), lambda b, i: (b, 0, 0)),
        ] + [pl.BlockSpec(a.shape, const) for a in params],
        out_specs=[out_spec] * 7,
        out_shape=[out_sds] * 7,
        compiler_params=_cparams("parallel", "arbitrary"),
        name="rwkv_prep",
    )(ps3, ps3, prev0, *params)


def _rwkv_prep_t(ps2, prev2, p):
    bd = ps2.shape[0]
    params = _prep_params(p)
    full = lambda a: pl.BlockSpec(a.shape, lambda i: (0, 0))
    out_sds = jax.ShapeDtypeStruct((B_WIDTH, bd), F32)
    return pl.pallas_call(
        _rwkv_prep_t_kernel,
        grid=(1,),
        in_specs=[full(ps2), full(prev2)] + [full(a) for a in params],
        out_specs=[pl.BlockSpec((B_WIDTH, bd), lambda i: (0, 0))] * 7,
        out_shape=[out_sds] * 7,
        compiler_params=_cparams("arbitrary"),
        name="rwkv_prep_t",
    )(ps2, prev2, *params)


def _rwkv_scan_kernel(r_ref, lw_ref, k_ref, v_ref, a_ref, b_ref, g_ref, rk_ref, lnw_ref, lnb_ref,
                      tril_ref, o_ref, s_ref):
    n_b, chunk, _ = r_ref.shape

    @pl.when(pl.program_id(1) == 0)
    def _():
        s_ref[...] = jnp.zeros_like(s_ref)

    ti = lax.broadcasted_iota(jnp.int32, (chunk, 2 * chunk), 0)
    si = lax.broadcasted_iota(jnp.int32, (chunk, 2 * chunk), 1)
    si = jnp.where(si >= chunk, si - chunk, si)
    strict = si < ti
    incl = si <= ti
    eye = jnp.where(lax.broadcasted_iota(jnp.int32, (chunk, chunk), 0)
                    == lax.broadcasted_iota(jnp.int32, (chunk, chunk), 1), 1.0, 0.0).astype(F32)
    n_sq = max(chunk.bit_length() - 2, 0)

    sls = [slice(h * RWKV_HEAD_DIM, (h + 1) * RWKV_HEAD_DIM) for h in range(RWKV_HEADS)]

    chains, ar, bk, v_h, g_last = [], [], [], [], []
    for bi in range(n_b):
        lw = lw_ref[bi]
        cum = _dot_exact_lhs(tril_ref[...], lw)
        e_in = jnp.exp(cum)
        e_inv = jnp.exp(-cum)
        at = a_ref[bi] * jnp.exp(cum - lw)
        rt = r_ref[bi] * e_in
        bt = b_ref[bi] * e_inv
        kt = k_ref[bi] * e_inv
        v = v_ref[bi]
        for h, sl in enumerate(sls):
            chains.append((bi, h))
            ar.append(jnp.concatenate([at[:, sl], rt[:, sl]], axis=0))
            bk.append(jnp.concatenate([bt[:, sl], kt[:, sl]], axis=0))
            v_h.append(v[:, sl])
            g_last.append(e_in[chunk - 1:chunk, sl])
    n = range(len(chains))
    s_old = [s_ref[bi, h] for bi, h in chains]
    m_all = [_dot1(ar[c], bk[c], NT) for c in n]
    gs = [_dot1(ar[c], s_old[c], NT) for c in n]
    a_m = [jnp.where(strict, m_all[c][:chunk], 0.0) for c in n]
    r_m = [jnp.where(incl, m_all[c][chunk:], 0.0) for c in n]
    pw = [a_m[c][:, :chunk] for c in n]
    x = [eye + pw[c] for c in n]
    for _ in range(n_sq):
        pw = [_dot1(pw[c], pw[c]) for c in n]
        x = [x[c] + _dot1(x[c], pw[c]) for c in n]
    av = [_dot1(a_m[c][:, chunk:], v_h[c]) for c in n]
    u = [_dot1(x[c], gs[c][:chunk] + av[c]) for c in n]
    uv = [jnp.concatenate([u[c], v_h[c]], axis=0) for c in n]
    y = [gs[c][chunk:] + _dot1(r_m[c], uv[c]) for c in n]
    for c, (bi, h) in enumerate(chains):
        s_ref[bi, h] = (s_old[c] + _dot3(uv[c], bk[c], TN)) * g_last[c]
    for c, (bi, h) in enumerate(chains):
        sl = sls[h]
        mu = jnp.mean(y[c], axis=-1, keepdims=True)
        var = jnp.mean(jnp.square(y[c] - mu), axis=-1, keepdims=True)
        yn = (y[c] - mu) * lax.rsqrt(var + GN_EPS) * lnw_ref[:, sl] + lnb_ref[:, sl]
        bonus = jnp.sum(r_ref[bi, :, sl] * k_ref[bi, :, sl] * rk_ref[:, sl], axis=-1, keepdims=True) * v_h[c]
        o_ref[bi, :, sl] = (yn + bonus) * g_ref[bi, :, sl]


def _rwkv_scan(streams, p, chunk):
    bsz, t, _ = streams[0].shape
    n_b = SCAN_BATCH if bsz % SCAN_BATCH == 0 else 1
    row = lambda a: a.reshape(1, B_WIDTH)
    tril = (jnp.arange(chunk)[None, :] <= jnp.arange(chunk)[:, None]).astype(BF16)
    const = lambda b, c: (0, 0)
    blk = pl.BlockSpec((n_b, chunk, B_WIDTH), lambda b, c: (b, c, 0))
    st_shape = (bsz, RWKV_HEADS, RWKV_HEAD_DIM, RWKV_HEAD_DIM)
    st = pl.BlockSpec((n_b,) + st_shape[1:], lambda b, c: (b, 0, 0, 0))
    vec = pl.BlockSpec((1, B_WIDTH), const)
    return pl.pallas_call(
        _rwkv_scan_kernel,
        grid=(bsz // n_b, t // chunk),
        in_specs=[blk] * 7 + [vec, vec, vec, pl.BlockSpec((chunk, chunk), const)],
        out_specs=[blk, st],
        out_shape=[jax.ShapeDtypeStruct((bsz, t, B_WIDTH), F32), jax.ShapeDtypeStruct(st_shape, F32)],
        compiler_params=_cparams("parallel", "arbitrary"),
        name="rwkv_scan",
    )(*streams, row(p["r_k"]), row(p["ln_x_w"]), row(p["ln_x_b"]), tril)


def _rwkv_step_kernel(r_ref, lw_ref, k_ref, v_ref, a_ref, b_ref, g_ref, rk_ref, lnw_ref, lnb_ref, s_ref,
                      o_ref, so_ref, y_ref):
    w = jnp.exp(lw_ref[...])
    a, b, k, r = a_ref[...], b_ref[...], k_ref[...], r_ref[...]

    def value_row(i, carry):
        s = s_ref[i]
        sa = jnp.sum(s * a, axis=0, keepdims=True)
        s_new = s * w + sa * b + v_ref[pl.ds(i, 1), :] * k
        so_ref[i] = s_new
        y_ref[pl.ds(i, 1), :] = jnp.sum(s_new * r, axis=0, keepdims=True)
        return carry

    lax.fori_loop(0, RWKV_HEAD_DIM, value_row, 0, unroll=4)
    y = y_ref[...]
    mu = jnp.mean(y, axis=0, keepdims=True)
    var = jnp.mean(jnp.square(y - mu), axis=0, keepdims=True)
    yn = (y - mu) * lax.rsqrt(var + GN_EPS) * lnw_ref[...] + lnb_ref[...]
    bonus = jnp.sum(r_ref[...] * k_ref[...] * rk_ref[...], axis=0, keepdims=True) * v_ref[...]
    o_ref[...] = (yn + bonus) * g_ref[...]


def _rwkv_step(streams_t, s_t, p):
    bd = s_t.shape[-1]
    n = RWKV_HEAD_DIM
    col = lambda a: a.reshape(B_WIDTH, 1)
    vec = pl.BlockSpec((n, bd), lambda h: (h, 0))
    par = pl.BlockSpec((n, 1), lambda h: (h, 0))
    st = pl.BlockSpec((None, n, n, bd), lambda h: (h, 0, 0, 0))
    return pl.pallas_call(
        _rwkv_step_kernel,
        grid=(RWKV_HEADS,),
        in_specs=[vec] * 7 + [par] * 3 + [st],
        out_specs=[vec, st],
        out_shape=[jax.ShapeDtypeStruct((B_WIDTH, bd), F32), jax.ShapeDtypeStruct(s_t.shape, F32)],
        scratch_shapes=[pltpu.VMEM((n, bd), F32)],
        compiler_params=_cparams("parallel"),
        name="rwkv_step",
    )(*streams_t, col(p["r_k"]), col(p["ln_x_w"]), col(p["ln_x_b"]), s_t)


def _finish_kernel(x_ref, *refs, n_stats):
    stat_refs = refs[:n_stats]
    ob_ref, g_ref, wpa_ref, wpb_ref, wo_ref, g2_ref, rw_ref, rb_ref, h_ref, hn_ref, gate_ref, top_ref = refs[n_stats:]
    if n_stats == 1:
        o_a = stat_refs[0][...]
    else:
        outs, lses = stat_refs[0::N_STATS], stat_refs[1::N_STATS]
        mx = functools.reduce(jnp.maximum, [l[...] for l in lses])
        es = [jnp.exp(l[...] - mx) for l in lses]
        o_a = sum(o[...] * e for o, e in zip(outs, es)) / sum(es)
    br_a = _dot1(o_a, wpa_ref[...])
    br_b = _dot1(ob_ref[...], wpb_ref[...])
    mix = g_ref[:, :D_MODEL] * br_a + g_ref[:, D_MODEL:] * br_b
    h = x_ref[...] + _dot1(mix, wo_ref[...])
    h_ref[...] = h
    hn = h * lax.rsqrt(jnp.mean(h * h, axis=-1, keepdims=True) + RMS_EPS) * g2_ref[...]
    hn_ref[...] = hn
    lg = _dot3(hn, rw_ref[...]) + rb_ref[...]
    lane = lax.broadcasted_iota(jnp.int32, lg.shape, 1)
    lane_f = lane.astype(F32)
    live = jnp.where(lane < N_EXPERTS, lg, -jnp.inf)
    vals = jnp.full(lg.shape, -jnp.inf, F32)
    ids = jnp.zeros(lg.shape, F32)
    for j in range(TOP_K):
        best = jnp.max(live, axis=-1, keepdims=True)
        idx = jnp.min(jnp.where(live == best, lane_f, float(LOGIT_PAD)), axis=-1, keepdims=True)
        vals = jnp.where(lane == j, best, vals)
        ids = jnp.where(lane == j, idx, ids)
        live = jnp.where(lane_f == idx, -jnp.inf, live)
    e = jnp.exp(vals - jnp.max(vals, axis=-1, keepdims=True))
    gate_ref[...] = e / jnp.sum(e, axis=-1, keepdims=True)
    top_ref[...] = ids.astype(jnp.int32)


def _finish(x2, stats, o_b, gates, wb, tm):
    n = x2.shape[0]
    const = lambda i: (0, 0)
    rows = lambda w: pl.BlockSpec((tm, w), lambda i: (i, 0))
    params = [wb["w_pa"], wb["w_pb"], wb["w_o"], wb["norm2_g"], wb["router_w"], wb["router_b"]]
    return pl.pallas_call(
        functools.partial(_finish_kernel, n_stats=len(stats)),
        grid=(n // tm,),
        in_specs=[rows(D_MODEL)] + [rows(A_WIDTH)] * len(stats) + [rows(B_WIDTH), rows(2 * D_MODEL)]
                 + [pl.BlockSpec(a.shape, const) for a in params],
        out_specs=[rows(D_MODEL), rows(D_MODEL), rows(LOGIT_PAD), rows(LOGIT_PAD)],
        out_shape=[jax.ShapeDtypeStruct((n, D_MODEL), F32), jax.ShapeDtypeStruct((n, D_MODEL), F32),
                   jax.ShapeDtypeStruct((n, LOGIT_PAD), F32), jax.ShapeDtypeStruct((n, LOGIT_PAD), jnp.int32)],
        compiler_params=_cparams("parallel"),
        name="finish",
    )(x2, *stats, o_b, gates, *params)


def _row(ref, i):
    return ref.at[pl.ds(i, 1)]


def _moe_kernel(blk_e_ref, nused_ref, tok_ref, tok_next_ref, tok_next2_ref, hn_hbm, wgu32_ref, bgu_ref, wd32_ref,
                bdn_ref, ys_ref, xbuf_a, xbuf_b, xbuf_c, wgu_ref, wd_ref, fence_buf, sem, fence_sem):
    i = pl.program_id(0)
    nused = nused_ref[0]
    n_chunks = D_FF // COL_CHUNK
    bufs = (xbuf_a, xbuf_b, xbuf_c)
    n_bufs = len(bufs)

    def issue(idx_ref, s):
        for r in range(MOE_BLOCK):
            pltpu.make_async_copy(_row(hn_hbm, idx_ref[0, 0, r]), _row(bufs[s], r), sem.at[s]).start(priority=r % 2)

    def drain(s):
        for _ in range(MOE_BLOCK):
            pltpu.make_async_copy(_row(hn_hbm, 0), _row(bufs[s], 0), sem.at[s]).wait()

    @pl.when(jnp.logical_and(i == 0, nused > 0))
    def _():
        issue(tok_ref, 0)
        issue(tok_next_ref, 1)

    new_expert = jnp.logical_or(i == 0, blk_e_ref[i] != blk_e_ref[jnp.maximum(i - 1, 0)])

    @pl.when(jnp.logical_and(i < nused, new_expert))
    def _():
        wgu_ref[...] = wgu32_ref[...].astype(BF16)
        wd_ref[...] = wd32_ref[...].astype(BF16)

    def block(s):
        drain(s)
        x = bufs[s][...].astype(BF16)
        fence_buf[0] = jnp.zeros(fence_buf.shape[1:], F32)
        fence = pltpu.make_async_copy(fence_buf.at[0], fence_buf.at[1], fence_sem)
        fence.start()
        issue(tok_next2_ref, (s + 2) % n_bufs)
        acc = jnp.zeros((MOE_BLOCK, D_MODEL), F32)
        for c in range(n_chunks):
            if c == n_chunks // 2:
                fence.wait()
            lo = c * COL_CHUNK
            gate = _dg(x, wgu_ref[:, lo:lo + COL_CHUNK], NN) + bgu_ref[:, lo:lo + COL_CHUNK]
            up = _dg(x, wgu_ref[:, D_FF + lo:D_FF + lo + COL_CHUNK], NN) + bgu_ref[:, D_FF + lo:D_FF + lo + COL_CHUNK]
            gate = jnp.minimum(gate, SWIGLU_LIMIT)
            up = jnp.clip(up, -SWIGLU_LIMIT, SWIGLU_LIMIT)
            hid = (up + 1.0) * gate * _sigmoid(gate * SWIGLU_ALPHA)
            acc = acc + _dg(hid.astype(BF16), wd_ref[lo:lo + COL_CHUNK, :], NN)
        ys_ref[...] = acc + bdn_ref[...]

        @pl.when(i + 1 >= nused)
        def _():
            drain((s + 1) % n_bufs)
            drain((s + 2) % n_bufs)

    for s in range(n_bufs):
        pl.when(jnp.logical_and(i < nused, i % n_bufs == s))(functools.partial(block, s))

    @pl.when(i >= nused)
    def _():
        ys_ref[...] = jnp.zeros_like(ys_ref)


def _moe_blocks(hn, row_tok, blk_e, nused, wb):
    n_blocks = row_tok.shape[0]
    last = n_blocks - 1
    grid_spec = pltpu.PrefetchScalarGridSpec(
        num_scalar_prefetch=2,
        grid=(n_blocks,),
        in_specs=[
            pl.BlockSpec((1, 1, MOE_BLOCK), lambda i, be, nu: (i, 0, 0), memory_space=pltpu.SMEM),
            pl.BlockSpec((1, 1, MOE_BLOCK), lambda i, be, nu: (jnp.minimum(i + 1, last), 0, 0),
                         memory_space=pltpu.SMEM),
            pl.BlockSpec((1, 1, MOE_BLOCK), lambda i, be, nu: (jnp.minimum(i + 2, last), 0, 0),
                         memory_space=pltpu.SMEM),
            pl.BlockSpec(memory_space=pl.ANY),
            pl.BlockSpec((None, D_MODEL, 2 * D_FF), lambda i, be, nu: (be[i], 0, 0)),
            pl.BlockSpec((None, 1, 2 * D_FF), lambda i, be, nu: (be[i], 0, 0)),
            pl.BlockSpec((None, D_FF, D_MODEL), lambda i, be, nu: (be[i], 0, 0)),
            pl.BlockSpec((None, 1, D_MODEL), lambda i, be, nu: (be[i], 0, 0)),
        ],
        out_specs=pl.BlockSpec((MOE_BLOCK, D_MODEL), lambda i, be, nu: (i, 0)),
        scratch_shapes=[pltpu.VMEM((MOE_BLOCK, D_MODEL), F32)] * 3
                       + [pltpu.VMEM((D_MODEL, 2 * D_FF), BF16), pltpu.VMEM((D_FF, D_MODEL), BF16),
                          pltpu.VMEM((2, 8, LANE_TILE), F32), pltpu.SemaphoreType.DMA((3,)),
                          pltpu.SemaphoreType.DMA(())],
    )
    return pl.pallas_call(
        _moe_kernel,
        grid_spec=grid_spec,
        out_shape=jax.ShapeDtypeStruct((n_blocks * MOE_BLOCK, D_MODEL), F32),
        compiler_params=_cparams("arbitrary"),
        name="moe_blocks",
    )(blk_e, nused, row_tok, row_tok, row_tok, hn, wb["w_gu"], wb["b_gu"], wb["w_down"], wb["b_down"])


def _combine_kernel(pos_ref, pos_next_ref, h_ref, gate_ref, ys_hbm, y_ref, gbuf, sem):
    tm = h_ref.shape[0]
    rows = TOP_K * tm
    i = pl.program_id(0)
    n = pl.num_programs(0)
    slot = i % 2

    def issue(idx_ref, s):
        for kk in range(rows):
            pltpu.make_async_copy(_row(ys_hbm, idx_ref[0, 0, kk]), gbuf.at[s, pl.ds(kk, 1)], sem.at[s]).start(
                priority=kk % 2)

    @pl.when(i == 0)
    def _():
        issue(pos_ref, 0)

    @pl.when(i + 1 < n)
    def _():
        issue(pos_next_ref, 1 - slot)

    for kk in range(rows):
        pltpu.make_async_copy(_row(ys_hbm, 0), gbuf.at[slot, pl.ds(0, 1)], sem.at[slot]).wait()
    gv = gate_ref[...]
    ff = gv[:, 0:1] * gbuf[slot, 0:tm, :]
    for j in range(1, TOP_K):
        ff = ff + gv[:, j:j + 1] * gbuf[slot, j * tm:(j + 1) * tm, :]
    y_ref[...] = h_ref[...] + ff


def _combine(h, gates_pad, pos, ys):
    n = h.shape[0]
    tm = min(COMBINE_TOKENS, n)
    steps = n // tm
    rows = TOP_K * tm
    pos3 = pos.reshape(steps, tm, TOP_K).transpose(0, 2, 1).reshape(steps, 1, rows)
    return pl.pallas_call(
        _combine_kernel,
        grid=(steps,),
        in_specs=[
            pl.BlockSpec((1, 1, rows), lambda i: (i, 0, 0), memory_space=pltpu.SMEM),
            pl.BlockSpec((1, 1, rows), lambda i: (jnp.minimum(i + 1, steps - 1), 0, 0), memory_space=pltpu.SMEM),
            pl.BlockSpec((tm, D_MODEL), lambda i: (i, 0)),
            pl.BlockSpec((tm, LOGIT_PAD), lambda i: (i, 0)),
            pl.BlockSpec(memory_space=pl.ANY),
        ],
        out_specs=pl.BlockSpec((tm, D_MODEL), lambda i: (i, 0)),
        out_shape=jax.ShapeDtypeStruct((n, D_MODEL), F32),
        scratch_shapes=[pltpu.VMEM((2, rows, D_MODEL), F32), pltpu.SemaphoreType.DMA((2,))],
        compiler_params=_cparams("arbitrary"),
        name="moe_combine",
    )(pos3, pos3, h, gates_pad, ys)


def _route(top_e):
    n = top_e.shape[0]
    nk = n * TOP_K
    flat_e = top_e.reshape(nk)
    onehot = (flat_e[:, None] == jnp.arange(N_EXPERTS, dtype=top_e.dtype)[None, :]).astype(jnp.int32)
    csum = jnp.cumsum(onehot, axis=0)
    counts = csum[-1]
    starts = jnp.cumsum(counts) - counts
    pcounts = (counts + MOE_BLOCK - 1) // MOE_BLOCK * MOE_BLOCK
    pends = jnp.cumsum(pcounts)
    pstarts = pends - pcounts
    pos = jnp.sum(onehot * (csum - 1 + pstarts[None, :]), axis=1).astype(jnp.int32)
    n_blocks = -(-nk // MOE_BLOCK) + N_EXPERTS
    blk_start = jnp.arange(n_blocks, dtype=jnp.int32) * MOE_BLOCK
    blk_e = jnp.minimum(jnp.sum(blk_start[:, None] >= pends[None, :], axis=1), N_EXPERTS - 1).astype(jnp.int32)
    nused = (pends[-1:] // MOE_BLOCK).astype(jnp.int32)
    order = jnp.argsort(flat_e)
    e_row = jnp.repeat(blk_e, MOE_BLOCK)
    rank = jnp.arange(n_blocks * MOE_BLOCK, dtype=jnp.int32) - pstarts[e_row]
    src = jnp.minimum(starts[e_row] + rank, nk - 1)
    row_tok = jnp.where(rank < counts[e_row], order[src] // TOP_K, 0).astype(jnp.int32)
    return pos, row_tok.reshape(n_blocks, 1, MOE_BLOCK), blk_e, nused


def _moe(h, hn, gates_pad, top_pad, wb):
    pos, row_tok, blk_e, nused = _route(top_pad[:, :TOP_K])
    ys = _moe_blocks(hn, row_tok, blk_e, nused, wb)
    return _combine(h, gates_pad, pos, ys)


def kernel(x_prompt, x_sample, cache_kv_w128, cache_kv_w512, cache_kv_w2048, state_wkv, state_shift,
           norm1_g, w_in, q_norm_g, k_norm_g, mu_shift, w0, w2, a0, a2, g2, k_k, k_a, r_k,
           ln_x_w, ln_x_b, w_pa, w_pb, w_o, norm2_g, router_w, router_b, w_gu, b_gu, w_down, b_down):
    bp, seq, _ = x_prompt.shape
    bd, t_s, _ = x_sample.shape
    assert t_s == 1
    rw = dict(mu_shift=mu_shift, w0=w0, w2=w2, a0=a0, a2=a2, g2=g2, k_k=k_k, k_a=k_a, r_k=r_k,
              ln_x_w=ln_x_w, ln_x_b=ln_x_b)
    wb = dict(
        w_pa=w_pa.astype(BF16), w_pb=w_pb.astype(BF16), w_o=w_o.astype(BF16),
        norm2_g=norm2_g.reshape(1, D_MODEL),
        router_w=jnp.pad(router_w, ((0, 0), (0, LOGIT_PAD - N_EXPERTS))),
        router_b=jnp.pad(router_b, (0, LOGIT_PAD - N_EXPERTS)).reshape(1, LOGIT_PAD),
        w_gu=w_gu.astype(F32), b_gu=b_gu.reshape(N_EXPERTS, 1, 2 * D_FF),
        w_down=w_down.astype(F32), b_down=b_down.reshape(N_EXPERTS, 1, D_MODEL),
    )
    w_in_b = w_in.astype(BF16)
    mult = jnp.stack([jnp.tile(q_norm_g[c // 3] * (HEAD_DIM ** -0.5) if c % 3 == 0 else
                               (k_norm_g[c // 3] if c % 3 == 1 else jnp.ones((HEAD_DIM,), F32)),
                               HEADS_PER_GROUP) for c in range(A_COLS // COL_CHUNK)])
    bd_head = _seg_ones(COL_CHUNK, HEAD_DIM, 1.0 / HEAD_DIM)

    xp2 = x_prompt.reshape(bp * seq, D_MODEL)
    pa, ps, pg = _in_proj(xp2, norm1_g, w_in_b, mult, bd_head, tm=PROJ_ROWS)
    pa3 = pa.reshape(bp, seq, A_COLS)
    stats = []
    kv_p = []
    for g in range(N_GROUPS):
        stats += _attn_prompt(pa3, g)
        keep = min(GROUP_WINDOWS[g], seq)
        lo = g * 3 * A_WIDTH + A_WIDTH
        kv_p.append(pa3[:, seq - keep:, lo:lo + 2 * A_WIDTH].reshape(bp, keep, 2, HEADS_PER_GROUP, HEAD_DIM))
    ps3 = ps.reshape(bp, seq, SHIFT_WIDTH)
    streams = _rwkv_prep(ps3, jnp.zeros((bp, 1, SHIFT_WIDTH), F32), rw, tm=PROJ_ROWS)
    o_b, wkv_p = _rwkv_scan(streams, rw, chunk=SCAN_CHUNK)
    h, hn, gates, top = _finish(xp2, stats, o_b.reshape(bp * seq, B_WIDTH), pg, wb, tm=PROJ_ROWS)
    y_prompt = _moe(h, hn, gates, top, wb).reshape(bp, seq, D_MODEL)
    shift_p = ps3[:, -1:]

    caches = (cache_kv_w128, cache_kv_w512, cache_kv_w2048)
    xs2 = x_sample.reshape(bd, D_MODEL)
    pa_s, ps_s, pg_s = _in_proj(xs2, norm1_g, w_in_b, mult, bd_head, tm=bd)
    caches_t = [jnp.transpose(c.astype(F32), (0, 2, 3, 4, 1)) for c in caches]
    *kv_t, oa_t = _sample_cache(pa_s.T, caches_t)
    kv_s = [jnp.transpose(t, (0, 4, 1, 2, 3)).astype(c.dtype) for t, c in zip(kv_t, caches)]
    streams_t = _rwkv_prep_t(ps_s, state_shift.reshape(bd, SHIFT_WIDTH).astype(F32), rw)
    ob_t, s_t = _rwkv_step(streams_t, jnp.transpose(state_wkv.astype(F32), (1, 2, 3, 0)), rw)
    wkv_s = jnp.transpose(s_t, (3, 0, 1, 2))
    h_s, hn_s, gates_s, top_s = _finish(xs2, [oa_t.T], ob_t.T, pg_s, wb, tm=bd)
    y_sample = _moe(h_s, hn_s, gates_s, top_s, wb).reshape(bd, 1, D_MODEL)
    shift_s = ps_s.reshape(bd, 1, SHIFT_WIDTH)

    return (y_prompt, y_sample, kv_p[0], kv_p[1], kv_p[2], wkv_p.astype(state_wkv.dtype), shift_p,
            kv_s[0], kv_s[1], kv_s[2], wkv_s.astype(state_wkv.dtype), shift_s.astype(state_shift.dtype))
```

```python
import functools

import jax
import jax.numpy as jnp
from jax import lax
from jax.experimental import pallas as pl
from jax.experimental.pallas import tpu as pltpu

F32 = jnp.float32
BF16 = jnp.bfloat16

D_MODEL = 1024
GROUP_WINDOWS = (128, 512, 2048)
GROUP_DILATIONS = (1, 4, 16)
N_GROUPS = 3
HEADS_PER_GROUP = 4
HEAD_DIM = 64
A_WIDTH = HEADS_PER_GROUP * HEAD_DIM
A_COLS = N_GROUPS * 3 * A_WIDTH
RWKV_HEADS = 8
RWKV_HEAD_DIM = 64
B_WIDTH = RWKV_HEADS * RWKV_HEAD_DIM
DECAY_LORA = 64
AAA_LORA = 64
GATE_LORA = 128
SHIFT_WIDTH = 3 * B_WIDTH + DECAY_LORA + AAA_LORA + GATE_LORA
D_IN = A_COLS + SHIFT_WIDTH + 2 * D_MODEL
N_EXPERTS = 32
TOP_K = 4
D_FF = D_MODEL
SWIGLU_LIMIT = 7.0
SWIGLU_ALPHA = 1.702
MOE_BLOCK = 256
RMS_EPS = 1e-6
GN_EPS = 64e-5

N_BACK = 128
COL_CHUNK = 256
SCAN_CHUNK = 64
SCAN_BATCH = 2
LANE_TILE = 128
N_STATS = 2
UNDILATED_BLOCKS = 4
LOGIT_PAD = LANE_TILE
COMBINE_TOKENS = 256
PROJ_ROWS = 512
VMEM_LIMIT = 48 * 1024 * 1024


def _cparams(*sem):
    return pltpu.CompilerParams(dimension_semantics=sem, vmem_limit_bytes=VMEM_LIMIT)


def _sigmoid(x):
    return 1.0 / (1.0 + jnp.exp(-x))


def _split(x):
    hi = x.astype(BF16)
    lo = (x - hi.astype(F32)).astype(BF16)
    return hi, lo


def _dg(a, b, dims):
    return lax.dot_general(a, b, (dims, ((), ())), preferred_element_type=F32)


NN = ((1,), (0,))
NT = ((1,), (1,))
TN = ((0,), (0,))


def _dot1(a, b, dims=NN):
    return _dg(a.astype(BF16), b.astype(BF16), dims)


def _dot3(a, b, dims=NN):
    ah, al = _split(a)
    bh, bl = _split(b)
    return _dg(ah, bh, dims) + _dg(al, bh, dims) + _dg(ah, bl, dims)


def _split3(x):
    hi = x.astype(BF16)
    r1 = x - hi.astype(F32)
    mid = r1.astype(BF16)
    lo = (r1 - mid.astype(F32)).astype(BF16)
    return hi, mid, lo


def _dot_exact_rhs(a, b_bf16):
    return sum(_dg(t, b_bf16, NN) for t in _split3(a))


def _dot_exact_lhs(l_bf16, x):
    return sum(_dg(l_bf16, t, NN) for t in _split3(x))


def _seg_ones(width, seg, scale):
    i = jnp.arange(width)[:, None] // seg
    j = jnp.arange(width)[None, :] // seg
    return jnp.where(i == j, scale, 0.0).astype(BF16)


def _in_proj_kernel(x_ref, g1_ref, w_ref, mult_ref, bd_ref, pa_ref, ps_ref, pg_ref):
    x = x_ref[...]
    xn = x * lax.rsqrt(jnp.mean(x * x, axis=-1, keepdims=True) + RMS_EPS) * g1_ref[...]
    xb = xn.astype(BF16)
    bd = bd_ref[...]
    for c in range(A_COLS // COL_CHUNK):
        acc = _dg(xb, w_ref[:, c * COL_CHUNK:(c + 1) * COL_CHUNK], NN)
        if c % 3 != 2:
            ms = _dot_exact_rhs(acc * acc, bd)
            acc = acc * lax.rsqrt(ms + RMS_EPS) * mult_ref[c:c + 1, :]
        pa_ref[:, c * COL_CHUNK:(c + 1) * COL_CHUNK] = acc
    for c in range(SHIFT_WIDTH // COL_CHUNK):
        lo = A_COLS + c * COL_CHUNK
        ps_ref[:, c * COL_CHUNK:(c + 1) * COL_CHUNK] = _dg(xb, w_ref[:, lo:lo + COL_CHUNK], NN)
    for c in range(2 * D_MODEL // COL_CHUNK):
        lo = A_COLS + SHIFT_WIDTH + c * COL_CHUNK
        pg_ref[:, c * COL_CHUNK:(c + 1) * COL_CHUNK] = _sigmoid(_dg(xb, w_ref[:, lo:lo + COL_CHUNK], NN)).astype(BF16)


def _in_proj(x2, norm1_g, w_in_b, mult, bd, tm):
    n = x2.shape[0]
    const = lambda i: (0, 0)
    return pl.pallas_call(
        _in_proj_kernel,
        grid=(n // tm,),
        in_specs=[
            pl.BlockSpec((tm, D_MODEL), lambda i: (i, 0)),
            pl.BlockSpec((1, D_MODEL), const),
            pl.BlockSpec((D_MODEL, D_IN), const, pipeline_mode=pl.Buffered(1)),
            pl.BlockSpec(mult.shape, const),
            pl.BlockSpec(bd.shape, const),
        ],
        out_specs=[
            pl.BlockSpec((tm, A_COLS), lambda i: (i, 0)),
            pl.BlockSpec((tm, SHIFT_WIDTH), lambda i: (i, 0)),
            pl.BlockSpec((tm, 2 * D_MODEL), lambda i: (i, 0)),
        ],
        out_shape=[
            jax.ShapeDtypeStruct((n, A_COLS), F32),
            jax.ShapeDtypeStruct((n, SHIFT_WIDTH), F32),
            jax.ShapeDtypeStruct((n, 2 * D_MODEL), BF16),
        ],
        compiler_params=_cparams("parallel"),
        name="in_proj",
    )(x2, norm1_g.reshape(1, D_MODEL), w_in_b, mult, bd)


def _attn_kernel(*refs, dil, blocks):
    halves = A_WIDTH // LANE_TILE
    q_refs, kp_refs, kc_refs, vp_refs, vc_refs = (refs[i * halves:(i + 1) * halves] for i in range(5))
    out_refs = refs[5 * halves:5 * halves + N_STATS]
    qs, ks, vs = refs[5 * halves + N_STATS:5 * halves + N_STATS + 3]
    stat_s = refs[5 * halves + N_STATS + 3:5 * halves + 2 * N_STATS + 3]
    stage = refs[5 * halves + 2 * N_STATS + 3:]
    n = pl.program_id(1)
    qi = lax.broadcasted_iota(jnp.int32, (N_BACK, 2 * N_BACK), 0)
    kj = lax.broadcasted_iota(jnp.int32, (N_BACK, 2 * N_BACK), 1)
    first_prev = jnp.where(n > 0, 0, N_BACK)
    lo = jnp.where(kj < N_BACK, qi + first_prev, N_BACK)
    hi = jnp.where(kj < N_BACK, N_BACK - 1, qi + N_BACK)
    mask = jnp.logical_and(kj >= lo, kj <= hi)
    mask_inner = jnp.logical_and(kj >= jnp.where(kj < N_BACK, qi, N_BACK), kj <= hi)

    def stream(r, carry, j=0):
        in_tile = dil == 1 and j > 0
        if dil > 1:
            rows = prev_rows = pl.ds(r, N_BACK, stride=dil)
        else:
            rows = slice(j * N_BACK, (j + 1) * N_BACK)
            prev_rows = slice((j - 1) * N_BACK, j * N_BACK) if in_tile else slice(None)
        kprev, vprev = (kc_refs, vc_refs) if in_tile else (kp_refs, vp_refs)
        for c in range(halves):
            lanes = slice(c * LANE_TILE, (c + 1) * LANE_TILE)
            qs[:, lanes] = q_refs[c][rows, :]
            ks[0:N_BACK, lanes] = kprev[c][prev_rows, :]
            ks[N_BACK:, lanes] = kc_refs[c][rows, :]
            vs[0:N_BACK, lanes] = vprev[c][prev_rows, :]
            vs[N_BACK:, lanes] = vc_refs[c][rows, :]
        sls = [slice(h * HEAD_DIM, (h + 1) * HEAD_DIM) for h in range(HEADS_PER_GROUP)]
        msk = mask_inner if in_tile else mask
        s = [jnp.where(msk, _dot1(qs[:, sl], ks[:, sl], NT), -jnp.inf) for sl in sls]
        m = [jnp.max(t, axis=-1, keepdims=True) for t in s]
        p = [jnp.exp(t - mm) for t, mm in zip(s, m)]
        den = [jnp.sum(t, axis=-1, keepdims=True) for t in p]
        num = [_dot1(t, vs[:, sl]) for t, sl in zip(p, sls)]
        for h, sl in enumerate(sls):
            stat_s[0][:, sl] = num[h] / den[h]
            stat_s[1][:, sl] = jnp.broadcast_to(m[h] + jnp.log(den[h]), (N_BACK, HEAD_DIM))
        for i in range(N_STATS):
            if dil > 1:
                for c in range(halves):
                    stage[i * halves + c][rows, :] = stat_s[i][:, c * LANE_TILE:(c + 1) * LANE_TILE]
            else:
                out_refs[i][rows, :] = stat_s[i][...]
        return carry

    if dil > 1:
        lax.fori_loop(0, dil, stream, 0)
        for i in range(N_STATS):
            for c in range(halves):
                out_refs[i][:, c * LANE_TILE:(c + 1) * LANE_TILE] = stage[i * halves + c][...]
    else:
        for j in range(blocks):
            stream(0, 0, j)


def _attn_prompt(pa3, g):
    bsz, seq, _ = pa3.shape
    dil = GROUP_DILATIONS[g]
    blocks = UNDILATED_BLOCKS if dil == 1 else 1
    rows = N_BACK * dil * blocks
    nb = seq // rows
    halves = A_WIDTH // LANE_TILE

    def specs(slab, prev):
        def one(c):
            col = (3 * g + slab) * halves + c
            if prev and dil == 1:
                return pl.BlockSpec((None, N_BACK, LANE_TILE),
                                    lambda b, n: (b, jnp.maximum(n * blocks - 1, 0), col))
            if prev:
                return pl.BlockSpec((None, rows, LANE_TILE), lambda b, n: (b, jnp.maximum(n - 1, 0), col))
            return pl.BlockSpec((None, rows, LANE_TILE), lambda b, n: (b, n, col))
        return [one(c) for c in range(halves)]

    in_specs = specs(0, False) + specs(1, True) + specs(1, False) + specs(2, True) + specs(2, False)
    out_spec = pl.BlockSpec((None, rows, A_WIDTH), lambda b, n: (b, n, 0))
    out_sds = jax.ShapeDtypeStruct((bsz, seq, A_WIDTH), F32)
    scratch = [pltpu.VMEM((N_BACK, A_WIDTH), F32), pltpu.VMEM((2 * N_BACK, A_WIDTH), F32),
               pltpu.VMEM((2 * N_BACK, A_WIDTH), F32)] + [pltpu.VMEM((N_BACK, A_WIDTH), F32)] * N_STATS
    if dil > 1:
        scratch += [pltpu.VMEM((rows, LANE_TILE), F32)] * (N_STATS * halves)
    outs = pl.pallas_call(
        functools.partial(_attn_kernel, dil=dil, blocks=blocks),
        grid=(bsz, nb),
        in_specs=in_specs,
        out_specs=[out_spec] * N_STATS,
        out_shape=[out_sds] * N_STATS,
        scratch_shapes=scratch,
        compiler_params=_cparams("parallel", "arbitrary"),
        name=f"attn_prompt_g{g}",
    )(*([pa3] * len(in_specs)))
    return [o.reshape(bsz * seq, A_WIDTH) for o in outs]


def _sample_cache_kernel(pat_ref, c0_ref, c1_ref, c2_ref, o0_ref, o1_ref, o2_ref, oa_ref):
    b = pl.program_id(0)
    n_b = pat_ref.shape[1]

    @pl.when(b == 0)
    def _():
        oa_ref[...] = jnp.zeros_like(oa_ref)

    lane_b = lax.broadcasted_iota(jnp.int32, pat_ref.shape, 1) == b
    col = jnp.sum(jnp.where(lane_b, pat_ref[...], 0.0), axis=1, keepdims=True)
    c_refs = (c0_ref, c1_ref, c2_ref)
    o_refs = (o0_ref, o1_ref, o2_ref)
    stats = []
    for g in range(N_GROUPS):
        dil = GROUP_DILATIONS[g]
        win = GROUP_WINDOWS[g]
        base = g * 3 * A_WIDTH
        lane = lax.broadcasted_iota(jnp.int32, (HEAD_DIM, win), 1)
        last = lane == win - 1
        live = lax.broadcasted_iota(jnp.int32, (1, win), 1) % dil == 0
        for h in range(HEADS_PER_GROUP):
            o = h * HEAD_DIM
            q = col[base + o:base + o + HEAD_DIM]
            kn = col[base + A_WIDTH + o:base + A_WIDTH + o + HEAD_DIM]
            vn = col[base + 2 * A_WIDTH + o:base + 2 * A_WIDTH + o + HEAD_DIM]
            kt = c_refs[g][0, 0, h]
            vt = c_refs[g][0, 1, h]
            o_refs[g][0, 0, h] = jnp.where(last, kn, pltpu.roll(kt, win - 1, axis=1))
            o_refs[g][0, 1, h] = jnp.where(last, vn, pltpu.roll(vt, win - 1, axis=1))
            s_c = jnp.where(live, jnp.sum(kt * q, axis=0, keepdims=True), -jnp.inf)
            s_n = jnp.sum(kn * q, axis=0, keepdims=True)
            m = jnp.maximum(jnp.max(s_c, axis=1, keepdims=True), s_n)
            p_c = jnp.exp(s_c - m)
            p_n = jnp.exp(s_n - m)
            den = jnp.sum(p_c, axis=1, keepdims=True) + p_n
            num = jnp.sum(vt * p_c, axis=1, keepdims=True) + p_n * vn
            stats.append((num, den, m))
    outs = []
    for h in range(HEADS_PER_GROUP):
        per_g = [stats[g * HEADS_PER_GROUP + h] for g in range(N_GROUPS)]
        mx = functools.reduce(jnp.maximum, [m for _, _, m in per_g])
        num = sum(n_ * jnp.exp(m - mx) for n_, _, m in per_g)
        den = sum(d_ * jnp.exp(m - mx) for _, d_, m in per_g)
        outs.append(num / den)
    o_col = jnp.concatenate(outs, axis=0)
    lane_o = lax.broadcasted_iota(jnp.int32, (A_WIDTH, n_b), 1) == b
    oa_ref[...] = jnp.where(lane_o, o_col, oa_ref[...])


def _sample_cache(pa_t, caches_t):
    bd = pa_t.shape[1]
    cspecs = [pl.BlockSpec((1, 2, HEADS_PER_GROUP, HEAD_DIM, w), lambda b: (b, 0, 0, 0, 0)) for w in GROUP_WINDOWS]
    return pl.pallas_call(
        _sample_cache_kernel,
        grid=(bd,),
        in_specs=[pl.BlockSpec(pa_t.shape, lambda b: (0, 0))] + cspecs,
        out_specs=cspecs + [pl.BlockSpec((A_WIDTH, bd), lambda b: (0, 0))],
        out_shape=[jax.ShapeDtypeStruct(c.shape, c.dtype) for c in caches_t]
                  + [jax.ShapeDtypeStruct((A_WIDTH, bd), F32)],
        compiler_params=_cparams("arbitrary"),
        name="sample_cache",
    )(pa_t, *caches_t)


def _rwkv_prep_math(pf, shifted, mu_ref, w0_ref, w2_ref, a0_ref, a2_ref, g2_ref, kk_ref, ka_ref, bd_ref):
    z = pf + mu_ref[...] * (shifted - pf)
    c1, c2, c3 = B_WIDTH, 2 * B_WIDTH, 3 * B_WIDTH
    r, k, v = z[:, :c1], z[:, c1:c2], z[:, c2:c3]
    wl = z[:, c3:c3 + DECAY_LORA]
    al = z[:, c3 + DECAY_LORA:c3 + DECAY_LORA + AAA_LORA]
    gl = z[:, c3 + DECAY_LORA + AAA_LORA:]
    xw = w0_ref[...] + _dot1(jnp.tanh(wl), w2_ref[...])
    w_log = -(jnp.maximum(-xw, 0.0) + jnp.log(1.0 + jnp.exp(-jnp.abs(xw)))) - 0.5
    lw = -jnp.exp(w_log)
    a = _sigmoid(a0_ref[...] + _dot1(al, a2_ref[...]))
    g = _dot1(_sigmoid(gl), g2_ref[...])
    kk = k * kk_ref[...]
    bd = bd_ref[...]
    kks = []
    for c in range(B_WIDTH // 128):
        kc = kk[:, c * 128:(c + 1) * 128]
        nrm = jnp.sqrt(_dot_exact_rhs(kc * kc, bd))
        kks.append(kc / jnp.maximum(nrm, 1e-12))
    kkn = jnp.concatenate(kks, axis=1)
    return r, lw, k * (1.0 + (a - 1.0) * ka_ref[...]), v, -kkn, kkn * a, g


def _rwkv_prep_kernel(ps_ref, pv_ref, p0_ref, *refs):
    param_refs, out_refs = refs[:9], refs[9:]
    i = pl.program_id(1)
    pf = ps_ref[...]
    prev_row = jnp.where(i == 0, p0_ref[...], pv_ref[7:8, :])
    row = lax.broadcasted_iota(jnp.int32, pf.shape, 0)
    shifted = jnp.where(row == 0, prev_row, pltpu.roll(pf, 1, axis=0))
    for o_ref, val in zip(out_refs, _rwkv_prep_math(pf, shifted, *param_refs)):
        o_ref[...] = val


def _rwkv_prep_t_kernel(ps_ref, prev_ref, *refs):
    param_refs, out_refs = refs[:9], refs[9:]
    for o_ref, val in zip(out_refs, _rwkv_prep_math(ps_ref[...], prev_ref[...], *param_refs)):
        o_ref[...] = val.T


def _prep_params(p):
    row = lambda a: a.reshape(1, -1)
    return [row(p["mu_shift"]), row(p["w0"]), p["w2"], row(p["a0"]), p["a2"], p["g2"],
            row(p["k_k"]), row(p["k_a"]), _seg_ones(128, RWKV_HEAD_DIM, 1.0)]


def _rwkv_prep(ps3, prev0, p, tm):
    bsz, t, _ = ps3.shape
    const = lambda b, i: (0, 0)
    params = _prep_params(p)
    out_spec = pl.BlockSpec((None, tm, B_WIDTH), lambda b, i: (b, i, 0))
    out_sds = jax.ShapeDtypeStruct((bsz, t, B_WIDTH), F32)
    return pl.pallas_call(
        _rwkv_prep_kernel,
        grid=(bsz, t // tm),
        in_specs=[
            pl.BlockSpec((None, tm, SHIFT_WIDTH), lambda b, i: (b, i, 0)),
            pl.BlockSpec((None, 8, SHIFT_WIDTH), lambda b, i: (b, jnp.maximum(i * (tm // 8) - 1, 0), 0)),
            pl.BlockSpec((None, 1, SHIFT_WIDTH), lambda b, i: (b, 0, 0)),
        ] + [pl.BlockSpec(a.shape, const) for a in params],
        out_specs=[out_spec] * 7,
        out_shape=[out_sds] * 7,
        compiler_params=_cparams("parallel", "arbitrary"),
        name="rwkv_prep",
    )(ps3, ps3, prev0, *params)


def _rwkv_prep_t(ps2, prev2, p):
    bd = ps2.shape[0]
    params = _prep_params(p)
    full = lambda a: pl.BlockSpec(a.shape, lambda i: (0, 0))
    out_sds = jax.ShapeDtypeStruct((B_WIDTH, bd), F32)
    return pl.pallas_call(
        _rwkv_prep_t_kernel,
        grid=(1,),
        in_specs=[full(ps2), full(prev2)] + [full(a) for a in params],
        out_specs=[pl.BlockSpec((B_WIDTH, bd), lambda i: (0, 0))] * 7,
        out_shape=[out_sds] * 7,
        compiler_params=_cparams("arbitrary"),
        name="rwkv_prep_t",
    )(ps2, prev2, *params)


def _rwkv_scan_kernel(r_ref, lw_ref, k_ref, v_ref, a_ref, b_ref, g_ref, rk_ref, lnw_ref, lnb_ref,
                      tril_ref, o_ref, s_ref):
    n_b, chunk, _ = r_ref.shape

    @pl.when(pl.program_id(1) == 0)
    def _():
        s_ref[...] = jnp.zeros_like(s_ref)

    ti = lax.broadcasted_iota(jnp.int32, (chunk, 2 * chunk), 0)
    si = lax.broadcasted_iota(jnp.int32, (chunk, 2 * chunk), 1)
    si = jnp.where(si >= chunk, si - chunk, si)
    strict = si < ti
    incl = si <= ti
    eye = jnp.where(lax.broadcasted_iota(jnp.int32, (chunk, chunk), 0)
                    == lax.broadcasted_iota(jnp.int32, (chunk, chunk), 1), 1.0, 0.0).astype(F32)
    n_sq = max(chunk.bit_length() - 2, 0)

    sls = [slice(h * RWKV_HEAD_DIM, (h + 1) * RWKV_HEAD_DIM) for h in range(RWKV_HEADS)]

    chains, ar, bk, v_h, g_last = [], [], [], [], []
    for bi in range(n_b):
        lw = lw_ref[bi]
        cum = _dot_exact_lhs(tril_ref[...], lw)
        e_in = jnp.exp(cum)
        e_inv = jnp.exp(-cum)
        at = a_ref[bi] * jnp.exp(cum - lw)
        rt = r_ref[bi] * e_in
        bt = b_ref[bi] * e_inv
        kt = k_ref[bi] * e_inv
        v = v_ref[bi]
        for h, sl in enumerate(sls):
            chains.append((bi, h))
            ar.append(jnp.concatenate([at[:, sl], rt[:, sl]], axis=0))
            bk.append(jnp.concatenate([bt[:, sl], kt[:, sl]], axis=0))
            v_h.append(v[:, sl])
            g_last.append(e_in[chunk - 1:chunk, sl])
    n = range(len(chains))
    s_old = [s_ref[bi, h] for bi, h in chains]
    m_all = [_dot1(ar[c], bk[c], NT) for c in n]
    gs = [_dot1(ar[c], s_old[c], NT) for c in n]
    a_m = [jnp.where(strict, m_all[c][:chunk], 0.0) for c in n]
    r_m = [jnp.where(incl, m_all[c][chunk:], 0.0) for c in n]
    pw = [a_m[c][:, :chunk] for c in n]
    x = [eye + pw[c] for c in n]
    for _ in range(n_sq):
        pw = [_dot1(pw[c], pw[c]) for c in n]
        x = [x[c] + _dot1(x[c], pw[c]) for c in n]
    av = [_dot1(a_m[c][:, chunk:], v_h[c]) for c in n]
    u = [_dot1(x[c], gs[c][:chunk] + av[c]) for c in n]
    uv = [jnp.concatenate([u[c], v_h[c]], axis=0) for c in n]
    y = [gs[c][chunk:] + _dot1(r_m[c], uv[c]) for c in n]
    for c, (bi, h) in enumerate(chains):
        s_ref[bi, h] = (s_old[c] + _dot3(uv[c], bk[c], TN)) * g_last[c]
    for c, (bi, h) in enumerate(chains):
        sl = sls[h]
        mu = jnp.mean(y[c], axis=-1, keepdims=True)
        var = jnp.mean(jnp.square(y[c] - mu), axis=-1, keepdims=True)
        yn = (y[c] - mu) * lax.rsqrt(var + GN_EPS) * lnw_ref[:, sl] + lnb_ref[:, sl]
        bonus = jnp.sum(r_ref[bi, :, sl] * k_ref[bi, :, sl] * rk_ref[:, sl], axis=-1, keepdims=True) * v_h[c]
        o_ref[bi, :, sl] = (yn + bonus) * g_ref[bi, :, sl]


def _rwkv_scan(streams, p, chunk):
    bsz, t, _ = streams[0].shape
    n_b = SCAN_BATCH if bsz % SCAN_BATCH == 0 else 1
    row = lambda a: a.reshape(1, B_WIDTH)
    tril = (jnp.arange(chunk)[None, :] <= jnp.arange(chunk)[:, None]).astype(BF16)
    const = lambda b, c: (0, 0)
    blk = pl.BlockSpec((n_b, chunk, B_WIDTH), lambda b, c: (b, c, 0))
    st_shape = (bsz, RWKV_HEADS, RWKV_HEAD_DIM, RWKV_HEAD_DIM)
    st = pl.BlockSpec((n_b,) + st_shape[1:], lambda b, c: (b, 0, 0, 0))
    vec = pl.BlockSpec((1, B_WIDTH), const)
    return pl.pallas_call(
        _rwkv_scan_kernel,
        grid=(bsz // n_b, t // chunk),
        in_specs=[blk] * 7 + [vec, vec, vec, pl.BlockSpec((chunk, chunk), const)],
        out_specs=[blk, st],
        out_shape=[jax.ShapeDtypeStruct((bsz, t, B_WIDTH), F32), jax.ShapeDtypeStruct(st_shape, F32)],
        compiler_params=_cparams("parallel", "arbitrary"),
        name="rwkv_scan",
    )(*streams, row(p["r_k"]), row(p["ln_x_w"]), row(p["ln_x_b"]), tril)


def _rwkv_step_kernel(r_ref, lw_ref, k_ref, v_ref, a_ref, b_ref, g_ref, rk_ref, lnw_ref, lnb_ref, s_ref,
                      o_ref, so_ref, y_ref):
    w = jnp.exp(lw_ref[...])
    a, b, k, r = a_ref[...], b_ref[...], k_ref[...], r_ref[...]

    def value_row(i, carry):
        s = s_ref[i]
        sa = jnp.sum(s * a, axis=0, keepdims=True)
        s_new = s * w + sa * b + v_ref[pl.ds(i, 1), :] * k
        so_ref[i] = s_new
        y_ref[pl.ds(i, 1), :] = jnp.sum(s_new * r, axis=0, keepdims=True)
        return carry

    lax.fori_loop(0, RWKV_HEAD_DIM, value_row, 0, unroll=4)
    y = y_ref[...]
    mu = jnp.mean(y, axis=0, keepdims=True)
    var = jnp.mean(jnp.square(y - mu), axis=0, keepdims=True)
    yn = (y - mu) * lax.rsqrt(var + GN_EPS) * lnw_ref[...] + lnb_ref[...]
    bonus = jnp.sum(r_ref[...] * k_ref[...] * rk_ref[...], axis=0, keepdims=True) * v_ref[...]
    o_ref[...] = (yn + bonus) * g_ref[...]


def _rwkv_step(streams_t, s_t, p):
    bd = s_t.shape[-1]
    n = RWKV_HEAD_DIM
    col = lambda a: a.reshape(B_WIDTH, 1)
    vec = pl.BlockSpec((n, bd), lambda h: (h, 0))
    par = pl.BlockSpec((n, 1), lambda h: (h, 0))
    st = pl.BlockSpec((None, n, n, bd), lambda h: (h, 0, 0, 0))
    return pl.pallas_call(
        _rwkv_step_kernel,
        grid=(RWKV_HEADS,),
        in_specs=[vec] * 7 + [par] * 3 + [st],
        out_specs=[vec, st],
        out_shape=[jax.ShapeDtypeStruct((B_WIDTH, bd), F32), jax.ShapeDtypeStruct(s_t.shape, F32)],
        scratch_shapes=[pltpu.VMEM((n, bd), F32)],
        compiler_params=_cparams("parallel"),
        name="rwkv_step",
    )(*streams_t, col(p["r_k"]), col(p["ln_x_w"]), col(p["ln_x_b"]), s_t)


def _finish_kernel(x_ref, *refs, n_stats):
    stat_refs = refs[:n_stats]
    ob_ref, g_ref, wpa_ref, wpb_ref, wo_ref, g2_ref, rw_ref, rb_ref, h_ref, hn_ref, gate_ref, top_ref = refs[n_stats:]
    if n_stats == 1:
        o_a = stat_refs[0][...]
    else:
        outs, lses = stat_refs[0::N_STATS], stat_refs[1::N_STATS]
        mx = functools.reduce(jnp.maximum, [l[...] for l in lses])
        es = [jnp.exp(l[...] - mx) for l in lses]
        o_a = sum(o[...] * e for o, e in zip(outs, es)) / sum(es)
    br_a = _dot1(o_a, wpa_ref[...])
    br_b = _dot1(ob_ref[...], wpb_ref[...])
    mix = g_ref[:, :D_MODEL] * br_a + g_ref[:, D_MODEL:] * br_b
    h = x_ref[...] + _dot1(mix, wo_ref[...])
    h_ref[...] = h
    hn = h * lax.rsqrt(jnp.mean(h * h, axis=-1, keepdims=True) + RMS_EPS) * g2_ref[...]
    hn_ref[...] = hn
    lg = _dot3(hn, rw_ref[...]) + rb_ref[...]
    lane = lax.broadcasted_iota(jnp.int32, lg.shape, 1)
    lane_f = lane.astype(F32)
    live = jnp.where(lane < N_EXPERTS, lg, -jnp.inf)
    vals = jnp.full(lg.shape, -jnp.inf, F32)
    ids = jnp.zeros(lg.shape, F32)
    for j in range(TOP_K):
        best = jnp.max(live, axis=-1, keepdims=True)
        idx = jnp.min(jnp.where(live == best, lane_f, float(LOGIT_PAD)), axis=-1, keepdims=True)
        vals = jnp.where(lane == j, best, vals)
        ids = jnp.where(lane == j, idx, ids)
        live = jnp.where(lane_f == idx, -jnp.inf, live)
    e = jnp.exp(vals - jnp.max(vals, axis=-1, keepdims=True))
    gate_ref[...] = e / jnp.sum(e, axis=-1, keepdims=True)
    top_ref[...] = ids.astype(jnp.int32)


def _finish(x2, stats, o_b, gates, wb, tm):
    n = x2.shape[0]
    const = lambda i: (0, 0)
    rows = lambda w: pl.BlockSpec((tm, w), lambda i: (i, 0))
    params = [wb["w_pa"], wb["w_pb"], wb["w_o"], wb["norm2_g"], wb["router_w"], wb["router_b"]]
    return pl.pallas_call(
        functools.partial(_finish_kernel, n_stats=len(stats)),
        grid=(n // tm,),
        in_specs=[rows(D_MODEL)] + [rows(A_WIDTH)] * len(stats) + [rows(B_WIDTH), rows(2 * D_MODEL)]
                 + [pl.BlockSpec(a.shape, const) for a in params],
        out_specs=[rows(D_MODEL), rows(D_MODEL), rows(LOGIT_PAD), rows(LOGIT_PAD)],
        out_shape=[jax.ShapeDtypeStruct((n, D_MODEL), F32), jax.ShapeDtypeStruct((n, D_MODEL), F32),
                   jax.ShapeDtypeStruct((n, LOGIT_PAD), F32), jax.ShapeDtypeStruct((n, LOGIT_PAD), jnp.int32)],
        compiler_params=_cparams("parallel"),
        name="finish",
    )(x2, *stats, o_b, gates, *params)


def _row(ref, i):
    return ref.at[pl.ds(i, 1)]


def _moe_kernel(blk_e_ref, nused_ref, tok_ref, tok_next_ref, tok_next2_ref, hn_hbm, wgu32_ref, bgu_ref, wd32_ref,
                bdn_ref, ys_ref, xbuf_a, xbuf_b, xbuf_c, wgu_ref, wd_ref, fence_buf, sem, fence_sem):
    i = pl.program_id(0)
    nused = nused_ref[0]
    n_chunks = D_FF // COL_CHUNK
    bufs = (xbuf_a, xbuf_b, xbuf_c)
    n_bufs = len(bufs)

    def issue(idx_ref, s):
        for r in range(MOE_BLOCK):
            pltpu.make_async_copy(_row(hn_hbm, idx_ref[0, 0, r]), _row(bufs[s], r), sem.at[s]).start(priority=r % 2)

    def drain(s):
        for _ in range(MOE_BLOCK):
            pltpu.make_async_copy(_row(hn_hbm, 0), _row(bufs[s], 0), sem.at[s]).wait()

    @pl.when(jnp.logical_and(i == 0, nused > 0))
    def _():
        issue(tok_ref, 0)
        issue(tok_next_ref, 1)

    new_expert = jnp.logical_or(i == 0, blk_e_ref[i] != blk_e_ref[jnp.maximum(i - 1, 0)])

    @pl.when(jnp.logical_and(i < nused, new_expert))
    def _():
        wgu_ref[...] = wgu32_ref[...].astype(BF16)
        wd_ref[...] = wd32_ref[...].astype(BF16)

    def block(s):
        drain(s)
        x = bufs[s][...].astype(BF16)
        fence_buf[0] = jnp.zeros(fence_buf.shape[1:], F32)
        fence = pltpu.make_async_copy(fence_buf.at[0], fence_buf.at[1], fence_sem)
        fence.start()
        issue(tok_next2_ref, (s + 2) % n_bufs)
        acc = jnp.zeros((MOE_BLOCK, D_MODEL), F32)
        for c in range(n_chunks):
            if c == n_chunks // 2:
                fence.wait()
            lo = c * COL_CHUNK
            gate = _dg(x, wgu_ref[:, lo:lo + COL_CHUNK], NN) + bgu_ref[:, lo:lo + COL_CHUNK]
            up = _dg(x, wgu_ref[:, D_FF + lo:D_FF + lo + COL_CHUNK], NN) + bgu_ref[:, D_FF + lo:D_FF + lo + COL_CHUNK]
            gate = jnp.minimum(gate, SWIGLU_LIMIT)
            up = jnp.clip(up, -SWIGLU_LIMIT, SWIGLU_LIMIT)
            hid = (up + 1.0) * gate * _sigmoid(gate * SWIGLU_ALPHA)
            acc = acc + _dg(hid.astype(BF16), wd_ref[lo:lo + COL_CHUNK, :], NN)
        ys_ref[...] = acc + bdn_ref[...]

        @pl.when(i + 1 >= nused)
        def _():
            drain((s + 1) % n_bufs)
            drain((s + 2) % n_bufs)

    for s in range(n_bufs):
        pl.when(jnp.logical_and(i < nused, i % n_bufs == s))(functools.partial(block, s))

    @pl.when(i >= nused)
    def _():
        ys_ref[...] = jnp.zeros_like(ys_ref)


def _moe_blocks(hn, row_tok, blk_e, nused, wb):
    n_blocks = row_tok.shape[0]
    last = n_blocks - 1
    grid_spec = pltpu.PrefetchScalarGridSpec(
        num_scalar_prefetch=2,
        grid=(n_blocks,),
        in_specs=[
            pl.BlockSpec((1, 1, MOE_BLOCK), lambda i, be, nu: (i, 0, 0), memory_space=pltpu.SMEM),
            pl.BlockSpec((1, 1, MOE_BLOCK), lambda i, be, nu: (jnp.minimum(i + 1, last), 0, 0),
                         memory_space=pltpu.SMEM),
            pl.BlockSpec((1, 1, MOE_BLOCK), lambda i, be, nu: (jnp.minimum(i + 2, last), 0, 0),
                         memory_space=pltpu.SMEM),
            pl.BlockSpec(memory_space=pl.ANY),
            pl.BlockSpec((None, D_MODEL, 2 * D_FF), lambda i, be, nu: (be[i], 0, 0)),
            pl.BlockSpec((None, 1, 2 * D_FF), lambda i, be, nu: (be[i], 0, 0)),
            pl.BlockSpec((None, D_FF, D_MODEL), lambda i, be, nu: (be[i], 0, 0)),
            pl.BlockSpec((None, 1, D_MODEL), lambda i, be, nu: (be[i], 0, 0)),
        ],
        out_specs=pl.BlockSpec((MOE_BLOCK, D_MODEL), lambda i, be, nu: (i, 0)),
        scratch_shapes=[pltpu.VMEM((MOE_BLOCK, D_MODEL), F32)] * 3
                       + [pltpu.VMEM((D_MODEL, 2 * D_FF), BF16), pltpu.VMEM((D_FF, D_MODEL), BF16),
                          pltpu.VMEM((2, 8, LANE_TILE), F32), pltpu.SemaphoreType.DMA((3,)),
                          pltpu.SemaphoreType.DMA(())],
    )
    return pl.pallas_call(
        _moe_kernel,
        grid_spec=grid_spec,
        out_shape=jax.ShapeDtypeStruct((n_blocks * MOE_BLOCK, D_MODEL), F32),
        compiler_params=_cparams("arbitrary"),
        name="moe_blocks",
    )(blk_e, nused, row_tok, row_tok, row_tok, hn, wb["w_gu"], wb["b_gu"], wb["w_down"], wb["b_down"])


def _combine_kernel(pos_ref, pos_next_ref, h_ref, gate_ref, ys_hbm, y_ref, gbuf, sem):
    tm = h_ref.shape[0]
    rows = TOP_K * tm
    i = pl.program_id(0)
    n = pl.num_programs(0)
    slot = i % 2

    def issue(idx_ref, s):
        for kk in range(rows):
            pltpu.make_async_copy(_row(ys_hbm, idx_ref[0, 0, kk]), gbuf.at[s, pl.ds(kk, 1)], sem.at[s]).start(
                priority=kk % 2)

    @pl.when(i == 0)
    def _():
        issue(pos_ref, 0)

    @pl.when(i + 1 < n)
    def _():
        issue(pos_next_ref, 1 - slot)

    for kk in range(rows):
        pltpu.make_async_copy(_row(ys_hbm, 0), gbuf.at[slot, pl.ds(0, 1)], sem.at[slot]).wait()
    gv = gate_ref[...]
    ff = gv[:, 0:1] * gbuf[slot, 0:tm, :]
    for j in range(1, TOP_K):
        ff = ff + gv[:, j:j + 1] * gbuf[slot, j * tm:(j + 1) * tm, :]
    y_ref[...] = h_ref[...] + ff


def _combine(h, gates_pad, pos, ys):
    n = h.shape[0]
    tm = min(COMBINE_TOKENS, n)
    steps = n // tm
    rows = TOP_K * tm
    pos3 = pos.reshape(steps, tm, TOP_K).transpose(0, 2, 1).reshape(steps, 1, rows)
    return pl.pallas_call(
        _combine_kernel,
        grid=(steps,),
        in_specs=[
            pl.BlockSpec((1, 1, rows), lambda i: (i, 0, 0), memory_space=pltpu.SMEM),
            pl.BlockSpec((1, 1, rows), lambda i: (jnp.minimum(i + 1, steps - 1), 0, 0), memory_space=pltpu.SMEM),
            pl.BlockSpec((tm, D_MODEL), lambda i: (i, 0)),
            pl.BlockSpec((tm, LOGIT_PAD), lambda i: (i, 0)),
            pl.BlockSpec(memory_space=pl.ANY),
        ],
        out_specs=pl.BlockSpec((tm, D_MODEL), lambda i: (i, 0)),
        out_shape=jax.ShapeDtypeStruct((n, D_MODEL), F32),
        scratch_shapes=[pltpu.VMEM((2, rows, D_MODEL), F32), pltpu.SemaphoreType.DMA((2,))],
        compiler_params=_cparams("arbitrary"),
        name="moe_combine",
    )(pos3, pos3, h, gates_pad, ys)


def _route(top_e):
    n = top_e.shape[0]
    nk = n * TOP_K
    flat_e = top_e.reshape(nk)
    onehot = (flat_e[:, None] == jnp.arange(N_EXPERTS, dtype=top_e.dtype)[None, :]).astype(jnp.int32)
    csum = jnp.cumsum(onehot, axis=0)
    counts = csum[-1]
    starts = jnp.cumsum(counts) - counts
    pcounts = (counts + MOE_BLOCK - 1) // MOE_BLOCK * MOE_BLOCK
    pends = jnp.cumsum(pcounts)
    pstarts = pends - pcounts
    pos = jnp.sum(onehot * (csum - 1 + pstarts[None, :]), axis=1).astype(jnp.int32)
    n_blocks = -(-nk // MOE_BLOCK) + N_EXPERTS
    blk_start = jnp.arange(n_blocks, dtype=jnp.int32) * MOE_BLOCK
    blk_e = jnp.minimum(jnp.sum(blk_start[:, None] >= pends[None, :], axis=1), N_EXPERTS - 1).astype(jnp.int32)
    nused = (pends[-1:] // MOE_BLOCK).astype(jnp.int32)
    order = jnp.argsort(flat_e)
    e_row = jnp.repeat(blk_e, MOE_BLOCK)
    rank = jnp.arange(n_blocks * MOE_BLOCK, dtype=jnp.int32) - pstarts[e_row]
    src = jnp.minimum(starts[e_row] + rank, nk - 1)
    row_tok = jnp.where(rank < counts[e_row], order[src] // TOP_K, 0).astype(jnp.int32)
    return pos, row_tok.reshape(n_blocks, 1, MOE_BLOCK), blk_e, nused


def _moe(h, hn, gates_pad, top_pad, wb):
    pos, row_tok, blk_e, nused = _route(top_pad[:, :TOP_K])
    ys = _moe_blocks(hn, row_tok, blk_e, nused, wb)
    return _combine(h, gates_pad, pos, ys)


def kernel(x_prompt, x_sample, cache_kv_w128, cache_kv_w512, cache_kv_w2048, state_wkv, state_shift,
           norm1_g, w_in, q_norm_g, k_norm_g, mu_shift, w0, w2, a0, a2, g2, k_k, k_a, r_k,
           ln_x_w, ln_x_b, w_pa, w_pb, w_o, norm2_g, router_w, router_b, w_gu, b_gu, w_down, b_down):
    bp, seq, _ = x_prompt.shape
    bd, t_s, _ = x_sample.shape
    assert t_s == 1
    rw = dict(mu_shift=mu_shift, w0=w0, w2=w2, a0=a0, a2=a2, g2=g2, k_k=k_k, k_a=k_a, r_k=r_k,
              ln_x_w=ln_x_w, ln_x_b=ln_x_b)
    wb = dict(
        w_pa=w_pa.astype(BF16), w_pb=w_pb.astype(BF16), w_o=w_o.astype(BF16),
        norm2_g=norm2_g.reshape(1, D_MODEL),
        router_w=jnp.pad(router_w, ((0, 0), (0, LOGIT_PAD - N_EXPERTS))),
        router_b=jnp.pad(router_b, (0, LOGIT_PAD - N_EXPERTS)).reshape(1, LOGIT_PAD),
        w_gu=w_gu.astype(F32), b_gu=b_gu.reshape(N_EXPERTS, 1, 2 * D_FF),
        w_down=w_down.astype(F32), b_down=b_down.reshape(N_EXPERTS, 1, D_MODEL),
    )
    w_in_b = w_in.astype(BF16)
    mult = jnp.stack([jnp.tile(q_norm_g[c // 3] * (HEAD_DIM ** -0.5) if c % 3 == 0 else
                               (k_norm_g[c // 3] if c % 3 == 1 else jnp.ones((HEAD_DIM,), F32)),
                               HEADS_PER_GROUP) for c in range(A_COLS // COL_CHUNK)])
    bd_head = _seg_ones(COL_CHUNK, HEAD_DIM, 1.0 / HEAD_DIM)

    xp2 = x_prompt.reshape(bp * seq, D_MODEL)
    pa, ps, pg = _in_proj(xp2, norm1_g, w_in_b, mult, bd_head, tm=PROJ_ROWS)
    pa3 = pa.reshape(bp, seq, A_COLS)
    stats = []
    kv_p = []
    for g in range(N_GROUPS):
        stats += _attn_prompt(pa3, g)
        keep = min(GROUP_WINDOWS[g], seq)
        lo = g * 3 * A_WIDTH + A_WIDTH
        kv_p.append(pa3[:, seq - keep:, lo:lo + 2 * A_WIDTH].reshape(bp, keep, 2, HEADS_PER_GROUP, HEAD_DIM))
    ps3 = ps.reshape(bp, seq, SHIFT_WIDTH)
    streams = _rwkv_prep(ps3, jnp.zeros((bp, 1, SHIFT_WIDTH), F32), rw, tm=PROJ_ROWS)
    o_b, wkv_p = _rwkv_scan(streams, rw, chunk=SCAN_CHUNK)
    h, hn, gates, top = _finish(xp2, stats, o_b.reshape(bp * seq, B_WIDTH), pg, wb, tm=PROJ_ROWS)
    y_prompt = _moe(h, hn, gates, top, wb).reshape(bp, seq, D_MODEL)
    shift_p = ps3[:, -1:]

    caches = (cache_kv_w128, cache_kv_w512, cache_kv_w2048)
    xs2 = x_sample.reshape(bd, D_MODEL)
    pa_s, ps_s, pg_s = _in_proj(xs2, norm1_g, w_in_b, mult, bd_head, tm=bd)
    caches_t = [jnp.transpose(c.astype(F32), (0, 2, 3, 4, 1)) for c in caches]
    *kv_t, oa_t = _sample_cache(pa_s.T, caches_t)
    kv_s = [jnp.transpose(t, (0, 4, 1, 2, 3)).astype(c.dtype) for t, c in zip(kv_t, caches)]
    streams_t = _rwkv_prep_t(ps_s, state_shift.reshape(bd, SHIFT_WIDTH).astype(F32), rw)
    ob_t, s_t = _rwkv_step(streams_t, jnp.transpose(state_wkv.astype(F32), (1, 2, 3, 0)), rw)
    wkv_s = jnp.transpose(s_t, (3, 0, 1, 2))
    h_s, hn_s, gates_s, top_s = _finish(xs2, [oa_t.T], ob_t.T, pg_s, wb, tm=bd)
    y_sample = _moe(h_s, hn_s, gates_s, top_s, wb).reshape(bd, 1, D_MODEL)
    shift_s = ps_s.reshape(bd, 1, SHIFT_WIDTH)

    return (y_prompt, y_sample, kv_p[0], kv_p[1], kv_p[2], wkv_p.astype(state_wkv.dtype), shift_p,
            kv_s[0], kv_s[1], kv_s[2], wkv_s.astype(state_wkv.dtype), shift_s.astype(state_shift.dtype))
```
